```python
import math
import jax
import jax.numpy as jnp
from jax import lax
import numpy as np

D_MODEL = 1024
BATCH = 32
SEQ = 256
DEPTH = 4
DEC_BATCH = 8
DEC_SEQ = 1024
PAST_LEN = 512

GRID_W = 64
HEAD_DIM = 64
A_HEADS = 8
A_KV_HEADS = 4
B_HEADS = 8
MIX_WIDTH = (A_HEADS + B_HEADS) * HEAD_DIM
ATTN_IN = (A_HEADS + 2 * A_KV_HEADS + 3 * B_HEADS) * HEAD_DIM
Q_BLOCK = 128
ROPE_THETA = 10000.0
NA_WIN_H = 8
NA_WIN_W = 16
NA_QCOLS = 16
NA_KCOLS = NA_QCOLS + NA_WIN_W
GLA_HEADS = 4
GLA_DK = D_MODEL // 2 // GLA_HEADS
GLA_DV = D_MODEL // GLA_HEADS
GLA_KDIM = GLA_HEADS * GLA_DK
GLA_VDIM = GLA_HEADS * GLA_DV
GLA_RANK = 16
GLA_GATE_NORM = 16.0
GLA_CHUNK = 64
GLA_IN = 2 * GLA_KDIM + 2 * GLA_VDIM + 2 * GLA_RANK
D_FF = ((8 * D_MODEL // 3 + 127) // 128) * 128
FFN_RES = 0.5
N_MOD = 9
N_ATTN = (DEPTH + 1) // 2
N_GLA = DEPTH // 2
EPS = 1e-6

kernel_name = 'hybrid_flow_prefix_trunk_step'


def rms_norm(x, g):
    xf = x.astype(jnp.float32)
    y = xf * lax.rsqrt(jnp.mean(xf * xf, axis=-1, keepdims=True) + EPS)
    return (y * g.astype(jnp.float32)).astype(x.dtype)


def modulate(h, shift, scale):
    return h * (1 + scale) + shift


def adaln(cond, w, b):
    m = jax.nn.silu(cond) @ w + b
    return m.reshape(m.shape[0], 1, N_MOD, D_MODEL)


def swiglu(h, wg, wu, wd):
    return (jax.nn.silu(h @ wg) * (h @ wu)) @ wd


def ffn_sublayer(x, shift, scale, gate, g, wg, wu, wd):
    h = modulate(rms_norm(x, g), shift, scale)
    return x + FFN_RES * gate * swiglu(h, wg, wu, wd)


def split_last(a, sizes):
    return jnp.split(a, np.cumsum(sizes)[:-1].tolist(), axis=-1)


def axial_rope(x, t):
    half = x.shape[-1] // 2
    freqs = ROPE_THETA ** (-jnp.arange(0, half, 2, dtype=jnp.float32) / half)

    def rot(xp, pos):
        ang = pos.astype(jnp.float32)[:, None] * freqs[None, :]
        cos = jnp.concatenate([jnp.cos(ang), jnp.cos(ang)], -1)[:, None, :]
        sin = jnp.concatenate([jnp.sin(ang), jnp.sin(ang)], -1)[:, None, :]
        xf = xp.astype(jnp.float32)
        x1, x2 = jnp.split(xf, 2, axis=-1)
        return xf * cos + jnp.concatenate([-x2, x1], -1) * sin

    out = jnp.concatenate([rot(x[..., :half], t // GRID_W), rot(x[..., half:], t % GRID_W)], -1)
    return out.astype(x.dtype)


def blocked_attention(q, k, v):
    bsz, t, hq, d = q.shape
    hkv = k.shape[2]
    grp = hq // hkv
    nb = t // Q_BLOCK
    scale = d ** -0.5
    qb = q.reshape(bsz, nb, Q_BLOCK, hkv, grp, d).transpose(1, 0, 3, 4, 2, 5)
    kt = k.transpose(0, 2, 1, 3)
    vt = v.transpose(0, 2, 1, 3)

    def one_block(qi):
        s = jnp.einsum('bhgqd,bhkd->bhgqk', qi, kt).astype(jnp.float32) * scale
        p = jax.nn.softmax(s, axis=-1).astype(v.dtype)
        return jnp.einsum('bhgqk,bhkd->bhgqd', p, vt)

    o = lax.map(one_block, qb)
    return o.transpose(1, 0, 4, 2, 3, 5).reshape(bsz, t, hq, d)


def neighbourhood_attention(q, k, v, k_ctx, v_ctx, rel_bias):
    bsz, t, h, d = q.shape
    rows = t // GRID_W
    wh = min(NA_WIN_H, rows)
    ncb = GRID_W // NA_QCOLS
    scale = d ** -0.5
    r = jnp.arange(rows)
    row_start = jnp.clip(r - wh // 2, 0, rows - wh)
    row_idx = row_start[:, None] + jnp.arange(wh)
    qcol = jnp.arange(GRID_W).reshape(ncb, NA_QCOLS)
    blk_start = jnp.clip(jnp.arange(ncb) * NA_QCOLS - NA_WIN_W // 2, 0, GRID_W - NA_KCOLS)
    col_idx = blk_start[:, None] + jnp.arange(NA_KCOLS)
    win_start = jnp.clip(qcol - NA_WIN_W // 2, 0, GRID_W - NA_WIN_W)
    kc = col_idx[:, None, :]
    in_win = (kc >= win_start[..., None]) & (kc < win_start[..., None] + NA_WIN_W)

    def gather(a):
        g = jnp.take(a.reshape(bsz, rows, GRID_W, h, d), row_idx, axis=1)
        g = jnp.take(g, col_idx, axis=3)
        return g.transpose(0, 5, 1, 3, 2, 4, 6).reshape(bsz, h, rows, ncb, wh * NA_KCOLS, d)

    kg = gather(k)
    vg = gather(v)
    qg = q.reshape(bsz, rows, ncb, NA_QCOLS, h, d).transpose(0, 4, 1, 2, 3, 5)

    dr = row_idx - r[:, None] + NA_WIN_H - 1
    dc = jnp.clip(kc - qcol[..., None] + NA_WIN_W - 1, 0, 2 * NA_WIN_W - 2)
    bias = rel_bias[:, dr[:, None, None, :, None], dc[None, :, :, None, :]]
    bias = bias.reshape(h, rows, ncb, NA_QCOLS, wh * NA_KCOLS).astype(jnp.float32)
    mask = jnp.broadcast_to(in_win[None, :, :, None, :], (rows, ncb, NA_QCOLS, wh, NA_KCOLS))
    mask = mask.reshape(rows, ncb, NA_QCOLS, wh * NA_KCOLS)

    s_win = jnp.einsum('bhrnqd,bhrnkd->bhrnqk', qg, kg).astype(jnp.float32) * scale + bias[None]
    s_win = jnp.where(mask[None, None], s_win, -jnp.inf)
    s_ctx = jnp.einsum('bhrnqd,bkhd->bhrnqk', qg, k_ctx).astype(jnp.float32) * scale
    n_win = s_win.shape[-1]
    p = jax.nn.softmax(jnp.concatenate([s_win, s_ctx], axis=-1), axis=-1).astype(v.dtype)
    o = (jnp.einsum('bhrnqk,bhrnkd->bhrnqd', p[..., :n_win], vg)
         + jnp.einsum('bhrnqk,bkhd->bhrnqd', p[..., n_win:], v_ctx))
    return o.transpose(0, 2, 3, 4, 1, 5).reshape(bsz, t, h, d)


def attn_project(h, w_in):
    bsz, t, _ = h.shape
    parts = split_last(h @ w_in, [A_HEADS * HEAD_DIM, A_KV_HEADS * HEAD_DIM, A_KV_HEADS * HEAD_DIM,
                                  B_HEADS * HEAD_DIM, B_HEADS * HEAD_DIM, B_HEADS * HEAD_DIM])
    return [p.reshape(bsz, t, -1, HEAD_DIM) for p in parts]


def attn_merge(oa, ob, w_out):
    bsz, t = oa.shape[:2]
    return jnp.concatenate([oa.reshape(bsz, t, -1), ob.reshape(bsz, t, -1)], axis=-1) @ w_out


def attn_mixer_ctx(h, w_in, w_out, gq, gk):
    qa, ka, va, qb, kb, vb = attn_project(h, w_in)
    qa = rms_norm(qa, gq)
    ka = rms_norm(ka, gk)
    oa = blocked_attention(qa, ka, va)
    ob = blocked_attention(qb, kb, vb)
    return attn_merge(oa, ob, w_out), ka, va, kb, vb


def attn_mixer_lat(h, t_lat, ka_ctx, va_ctx, kb_ctx, vb_ctx, w_in, w_out, gq, gk, rel_bias):
    qa, ka, va, qb, kb, vb = attn_project(h, w_in)
    qa = axial_rope(rms_norm(qa, gq), t_lat)
    ka = axial_rope(rms_norm(ka, gk), t_lat)
    oa = blocked_attention(qa, jnp.concatenate([ka, ka_ctx], axis=1), jnp.concatenate([va, va_ctx], axis=1))
    ob = neighbourhood_attention(qb, kb, vb, kb_ctx, vb_ctx, rel_bias)
    return attn_merge(oa, ob, w_out)


def gla_chunked(q, k, v, log_g, s0):
    bsz, t, h, _ = q.shape
    n = t // GLA_CHUNK

    def chunks(a):
        return a.reshape(bsz, n, GLA_CHUNK, h, a.shape[-1]).transpose(1, 0, 3, 2, 4)

    qc, kc, vc, gc = chunks(q), chunks(k), chunks(v), chunks(log_g)
    b = jnp.cumsum(gc.astype(jnp.float32), axis=-2)
    b_last = b[..., -1:, :]
    q_in = qc.astype(jnp.float32) * jnp.exp(b)
    k_in = kc.astype(jnp.float32) * jnp.exp(-b)
    k_out = kc.astype(jnp.float32) * jnp.exp(b_last - b)
    causal = jnp.tril(jnp.ones((GLA_CHUNK, GLA_CHUNK), dtype=bool))
    a = jnp.where(causal, jnp.einsum('nbhqd,nbhkd->nbhqk', q_in, k_in), 0.0)
    o_intra = jnp.einsum('nbhqk,nbhkv->nbhqv', a, vc.astype(jnp.float32))
    u = jnp.einsum('nbhkd,nbhkv->nbhdv', k_out, vc.astype(jnp.float32))
    decay = jnp.exp(b_last[..., 0, :])

    def step(s, inp):
        q_i, dec_i, u_i = inp
        o_i = jnp.einsum('bhqd,bhdv->bhqv', q_i, s)
        return s * dec_i[..., None] + u_i, o_i

    s_fin, o_inter = lax.scan(step, s0.astype(jnp.float32), (q_in, decay, u))
    o = (o_intra + o_inter).transpose(1, 0, 3, 2, 4).reshape(bsz, t, h, -1)
    return o.astype(q.dtype), s_fin.astype(q.dtype)


def gla_mixer(h, s0_f, s0_b, w_in, w_gup, b_gup, g_norm, w_out):
    bsz, t, _ = h.shape
    q, k, v, g, d_f, d_b = split_last(h @ w_in, [GLA_KDIM, GLA_KDIM, GLA_VDIM, GLA_VDIM, GLA_RANK, GLA_RANK])
    lg_f = jax.nn.log_sigmoid((d_f @ w_gup[0] + b_gup[0]).astype(jnp.float32)) / GLA_GATE_NORM
    lg_b = jax.nn.log_sigmoid((d_b @ w_gup[1] + b_gup[1]).astype(jnp.float32)) / GLA_GATE_NORM
    heads_k = lambda a: a.reshape(bsz, t, GLA_HEADS, GLA_DK)
    q = heads_k(q) * (GLA_DK ** -0.5)
    k = heads_k(k)
    v = v.reshape(bsz, t, GLA_HEADS, GLA_DV)
    lg_f, lg_b = heads_k(lg_f), heads_k(lg_b)
    rev = lambda a: a[:, ::-1]
    o_f, s_f = gla_chunked(q, k, v, lg_f, s0_f)
    o_b, s_b = gla_chunked(rev(q), rev(k), rev(v), rev(lg_b), s0_b)
    o = rms_norm(o_f + rev(o_b), g_norm).reshape(bsz, t, GLA_VDIM) * jax.nn.silu(g)
    return o @ w_out, s_f, s_b


def setup_inputs(seed: int = 0) -> dict:
    key = jax.random.key(seed)
    ks = jax.random.split(key, 26)

    def nrm(k, shape, s=1.0):
        return jax.random.normal(k, shape, jnp.float32) * s

    def gain(k, shape):
        return 1.0 + nrm(k, shape, 0.05)

    return {
        'x_prompt': nrm(ks[0], (BATCH, SEQ, D_MODEL)),
        'x_sample': nrm(ks[1], (DEC_BATCH, DEC_SEQ, D_MODEL)),
        'cache_a_k': nrm(ks[2], (DEC_BATCH, N_ATTN, PAST_LEN, A_KV_HEADS, HEAD_DIM)),
        'cache_a_v': nrm(ks[3], (DEC_BATCH, N_ATTN, PAST_LEN, A_KV_HEADS, HEAD_DIM)),
        'cache_b_k': nrm(ks[4], (DEC_BATCH, N_ATTN, PAST_LEN, B_HEADS, HEAD_DIM)),
        'cache_b_v': nrm(ks[5], (DEC_BATCH, N_ATTN, PAST_LEN, B_HEADS, HEAD_DIM)),
        'state_gla': nrm(ks[6], (DEC_BATCH, N_GLA, 2, GLA_HEADS, GLA_DK, GLA_DV)),
        'c': nrm(ks[7], (DEC_BATCH, D_MODEL)),
        'c_ctx': nrm(ks[8], (D_MODEL,)),
        'w_mod': nrm(ks[9], (DEPTH, D_MODEL, N_MOD * D_MODEL), 0.5 * D_MODEL ** -0.5),
        'b_mod': nrm(ks[10], (DEPTH, N_MOD * D_MODEL), 0.02),
        'g_norm': gain(ks[11], (DEPTH, 3, D_MODEL)),
        'w_ffn_gate': nrm(ks[12], (DEPTH, 2, D_MODEL, D_FF), D_MODEL ** -0.5),
        'w_ffn_up': nrm(ks[13], (DEPTH, 2, D_MODEL, D_FF), D_MODEL ** -0.5),
        'w_ffn_down': nrm(ks[14], (DEPTH, 2, D_FF, D_MODEL), D_FF ** -0.5),
        'w_attn_in': nrm(ks[15], (N_ATTN, D_MODEL, ATTN_IN), D_MODEL ** -0.5),
        'w_attn_out': nrm(ks[16], (N_ATTN, MIX_WIDTH, D_MODEL), MIX_WIDTH ** -0.5),
        'g_qnorm': gain(ks[17], (N_ATTN, HEAD_DIM)),
        'g_knorm': gain(ks[18], (N_ATTN, HEAD_DIM)),
        'na_rel_bias': nrm(ks[19], (N_ATTN, B_HEADS, 2 * NA_WIN_H - 1, 2 * NA_WIN_W - 1), 0.1),
        'w_gla_in': nrm(ks[20], (N_GLA, D_MODEL, GLA_IN), D_MODEL ** -0.5),
        'w_gla_gup': nrm(ks[21], (N_GLA, 2, GLA_RANK, GLA_KDIM), GLA_RANK ** -0.5),
        'b_gla_gup': nrm(ks[22], (N_GLA, 2, GLA_KDIM), 0.1),
        'g_gla_norm': gain(ks[23], (N_GLA, GLA_DV)),
        'w_gla_out': nrm(ks[24], (N_GLA, GLA_VDIM, D_MODEL), GLA_VDIM ** -0.5),
        'g_final': gain(ks[25], (D_MODEL,)),
    }


def reference(x_prompt, x_sample, cache_a_k, cache_a_v, cache_b_k, cache_b_v, state_gla, c, c_ctx,
              w_mod, b_mod, g_norm, w_ffn_gate, w_ffn_up, w_ffn_down, w_attn_in, w_attn_out,
              g_qnorm, g_knorm, na_rel_bias, w_gla_in, w_gla_gup, b_gla_gup, g_gla_norm, w_gla_out, g_final):
    t_lat = jnp.arange(x_sample.shape[1])
    xp, xs = x_prompt, x_sample
    new_a_k, new_a_v, new_b_k, new_b_v, new_gla = [], [], [], [], []
    for l in range(DEPTH):
        mp = adaln(c_ctx[None, :], w_mod[l], b_mod[l])
        ms = adaln(c, w_mod[l], b_mod[l])
        xp = ffn_sublayer(xp, mp[:, :, 0], mp[:, :, 1], mp[:, :, 2], g_norm[l, 0],
                          w_ffn_gate[l, 0], w_ffn_up[l, 0], w_ffn_down[l, 0])
        xs = ffn_sublayer(xs, ms[:, :, 0], ms[:, :, 1], ms[:, :, 2], g_norm[l, 0],
                          w_ffn_gate[l, 0], w_ffn_up[l, 0], w_ffn_down[l, 0])
        hp = modulate(rms_norm(xp, g_norm[l, 1]), mp[:, :, 3], mp[:, :, 4])
        hs = modulate(rms_norm(xs, g_norm[l, 1]), ms[:, :, 3], ms[:, :, 4])
        if l % 2 == 0:
            i = l // 2
            op, ka, va, kb, vb = attn_mixer_ctx(hp, w_attn_in[i], w_attn_out[i], g_qnorm[i], g_knorm[i])
            os_ = attn_mixer_lat(hs, t_lat, cache_a_k[:, i], cache_a_v[:, i], cache_b_k[:, i], cache_b_v[:, i],
                                 w_attn_in[i], w_attn_out[i], g_qnorm[i], g_knorm[i], na_rel_bias[i])
            new_a_k.append(ka)
            new_a_v.append(va)
            new_b_k.append(kb)
            new_b_v.append(vb)
        else:
            j = l // 2
            zero = jnp.zeros((xp.shape[0], GLA_HEADS, GLA_DK, GLA_DV), xp.dtype)
            op, s_f, s_b = gla_mixer(hp, zero, zero, w_gla_in[j], w_gla_gup[j], b_gla_gup[j],
                                     g_gla_norm[j], w_gla_out[j])
            os_, _, _ = gla_mixer(hs, state_gla[:, j, 0], state_gla[:, j, 1], w_gla_in[j], w_gla_gup[j],
                                  b_gla_gup[j], g_gla_norm[j], w_gla_out[j])
            new_gla.append(jnp.stack([s_f, s_b], axis=1))
        xp = xp + mp[:, :, 5] * op
        xs = xs + ms[:, :, 5] * os_
        xp = ffn_sublayer(xp, mp[:, :, 6], mp[:, :, 7], mp[:, :, 8], g_norm[l, 2],
                          w_ffn_gate[l, 1], w_ffn_up[l, 1], w_ffn_down[l, 1])
        xs = ffn_sublayer(xs, ms[:, :, 6], ms[:, :, 7], ms[:, :, 8], g_norm[l, 2],
                          w_ffn_gate[l, 1], w_ffn_up[l, 1], w_ffn_down[l, 1])
    y_prompt = rms_norm(xp, g_final)
    y_sample = rms_norm(xs, g_final)
    return (y_prompt, y_sample, jnp.stack(new_a_k, axis=1), jnp.stack(new_a_v, axis=1),
            jnp.stack(new_b_k, axis=1), jnp.stack(new_b_v, axis=1), jnp.stack(new_gla, axis=1))
```

```python
import functools

import numpy as np
import jax
import jax.numpy as jnp
from jax import lax
from jax.experimental import pallas as pl
from jax.experimental.pallas import tpu as pltpu

D_MODEL = 1024
BATCH = 32
SEQ = 256
DEPTH = 4
DEC_BATCH = 8
DEC_SEQ = 1024
PAST_LEN = 512
GRID_W = 64
HEAD_DIM = 64
A_HEADS = 8
A_KV_HEADS = 4
B_HEADS = 8
ROPE_THETA = 10000.0
NA_WIN_H = 8
NA_WIN_W = 16
GLA_HEADS = 4
GLA_DK = D_MODEL // 2 // GLA_HEADS
GLA_DV = D_MODEL // GLA_HEADS
GLA_KDIM = GLA_HEADS * GLA_DK
GLA_VDIM = GLA_HEADS * GLA_DV
GLA_RANK = 16
GLA_GATE_NORM = 16.0
GLA_CHUNK = 64
D_FF = ((8 * D_MODEL // 3 + 127) // 128) * 128
FFN_RES = 0.5
N_MOD = 9
EPS = 1e-6

N_CTX = BATCH * SEQ
N_LAT = DEC_BATCH * DEC_SEQ
N_TOK = N_CTX + N_LAT
COND_ROWS = 16
LANES = 128
PAIR = 2 * HEAD_DIM
GLA_D_PAD = LANES
VMEM_LIMIT = 56 * 1024 * 1024

F32 = jnp.float32
BF16 = jnp.bfloat16

TM_FFN = 1024
TF_FFN = 256
TM_PROJ = 512
TQ_LAT = 256


def _cparams(sem):
    return pltpu.CompilerParams(dimension_semantics=sem, vmem_limit_bytes=VMEM_LIMIT)


def _cond_row(i, tm):
    nct = N_CTX // tm
    per_batch = DEC_SEQ // tm
    return jnp.where(i < nct, 0, 1 + (i - nct) // per_batch)


def _rms(x, g):
    ms = jnp.mean(x * x, axis=-1, keepdims=True)
    return x * lax.rsqrt(ms + EPS) * g


def _dot(a, b):
    return jnp.dot(a, b, preferred_element_type=F32)


def _dot_nt(a, b):
    return lax.dot_general(a, b, (((1,), (1,)), ((), ())), preferred_element_type=F32)


def _dot_tn(a, b):
    return lax.dot_general(a, b, (((0,), (0,)), ((), ())), preferred_element_type=F32)


def _mod_kernel(c_ref, w_ref, b_ref, o_ref):
    c = c_ref[...]
    s = c * jax.nn.sigmoid(c)
    o_ref[...] = jnp.dot(s, w_ref[...], precision=lax.Precision.HIGHEST,
                         preferred_element_type=F32) + b_ref[...]


def _modulation(cond, w_mod, b_mod):
    tn = D_MODEL
    n_mod = N_MOD * D_MODEL
    out = pl.pallas_call(
        _mod_kernel,
        grid=(DEPTH, n_mod // tn),
        in_specs=[
            pl.BlockSpec((COND_ROWS, D_MODEL), lambda l, j: (0, 0)),
            pl.BlockSpec((None, D_MODEL, tn), lambda l, j: (l, 0, j)),
            pl.BlockSpec((None, 1, tn), lambda l, j: (l, 0, j)),
        ],
        out_specs=pl.BlockSpec((None, COND_ROWS, tn), lambda l, j: (l, 0, j)),
        out_shape=jax.ShapeDtypeStruct((DEPTH, COND_ROWS, n_mod), F32),
        compiler_params=_cparams(("parallel", "parallel")),
        name="adaln_mod",
    )(cond, w_mod, b_mod.reshape(DEPTH, 1, n_mod))
    return out.reshape(DEPTH, COND_ROWS, 3, 3, D_MODEL)


def _ffn_kernel(x_ref, mod_ref, g_ref, wg_ref, wu_ref, wd_ref, o_ref, h_ref, acc_ref):
    j = pl.program_id(1)

    @pl.when(j == 0)
    def _():
        y = _rms(x_ref[...], g_ref[...])
        h_ref[...] = (y * (1.0 + mod_ref[1:2, :]) + mod_ref[0:1, :]).astype(BF16)
        acc_ref[...] = jnp.zeros_like(acc_ref)

    h = h_ref[...]
    g = _dot(h, wg_ref[...])
    u = _dot(h, wu_ref[...])
    a = (g * jax.nn.sigmoid(g) * u).astype(BF16)
    acc_ref[...] += _dot(a, wd_ref[...])

    @pl.when(j == pl.num_programs(1) - 1)
    def _():
        o_ref[...] = x_ref[...] + FFN_RES * mod_ref[2:3, :] * acc_ref[...]


def _ffn(x, mod, layer, sub, g, wg, wu, wd):
    tm, tf = TM_FFN, TF_FFN
    return pl.pallas_call(
        _ffn_kernel,
        grid=(N_TOK // tm, D_FF // tf),
        in_specs=[
            pl.BlockSpec((tm, D_MODEL), lambda i, j: (i, 0)),
            pl.BlockSpec((None, None, None, 3, D_MODEL),
                         lambda i, j: (layer, _cond_row(i, tm), sub, 0, 0)),
            pl.BlockSpec((1, D_MODEL), lambda i, j: (0, 0)),
            pl.BlockSpec((D_MODEL, tf), lambda i, j: (0, j)),
            pl.BlockSpec((D_MODEL, tf), lambda i, j: (0, j)),
            pl.BlockSpec((tf, D_MODEL), lambda i, j: (j, 0)),
        ],
        out_specs=pl.BlockSpec((tm, D_MODEL), lambda i, j: (i, 0)),
        out_shape=jax.ShapeDtypeStruct((N_TOK, D_MODEL), F32),
        scratch_shapes=[pltpu.VMEM((tm, D_MODEL), BF16), pltpu.VMEM((tm, D_MODEL), F32)],
        compiler_params=_cparams(("parallel", "arbitrary")),
        name="ffn",
    )(x, mod, g.reshape(1, D_MODEL), wg, wu, wd)


def _mixer_h(x_ref, mod_ref, g_ref):
    y = _rms(x_ref[...], g_ref[...])
    return (y * (1.0 + mod_ref[1:2, :]) + mod_ref[0:1, :]).astype(BF16)


def _attn_in_kernel(widths, x_ref, mod_ref, g_ref, w_ref, *o_refs):
    h = _mixer_h(x_ref, mod_ref, g_ref)
    off = 0
    for o_ref, wd in zip(o_refs, widths):
        o_ref[...] = _dot(h, w_ref[:, off:off + wd])
        off += wd


def _gla_in_kernel(widths, x_ref, mod_ref, g_ref, w_ref, wgup_ref, bgup_ref, *o_refs):
    h = _mixer_h(x_ref, mod_ref, g_ref)
    off = 0
    for o_ref, wd in zip(o_refs[:4], widths):
        o_ref[...] = _dot(h, w_ref[:, off:off + wd])
        off += wd
    d = _dot(h, w_ref[:, off:off + GLA_D_PAD]).astype(BF16)
    pre = _dot(d, wgup_ref[...]) + bgup_ref[...]
    lg = (jnp.minimum(pre, 0.0) - jnp.log1p(jnp.exp(-jnp.abs(pre)))) * (1.0 / GLA_GATE_NORM)
    o_refs[4][...] = lg[:, :GLA_KDIM]
    o_refs[5][...] = lg[:, GLA_KDIM:]


def _mixer_in_specs(layer, tm, n_in):
    return [
        pl.BlockSpec((tm, D_MODEL), lambda i: (i, 0)),
        pl.BlockSpec((None, None, None, 3, D_MODEL), lambda i: (layer, _cond_row(i, tm), 1, 0, 0)),
        pl.BlockSpec((1, D_MODEL), lambda i: (0, 0)),
        pl.BlockSpec((D_MODEL, n_in), lambda i: (0, 0)),
    ]


def _attn_in(x, mod, layer, g, w_in):
    tm = TM_PROJ
    widths = (A_HEADS * HEAD_DIM, A_KV_HEADS * HEAD_DIM, A_KV_HEADS * HEAD_DIM,
              B_HEADS * HEAD_DIM, B_HEADS * HEAD_DIM, B_HEADS * HEAD_DIM)
    return pl.pallas_call(
        functools.partial(_attn_in_kernel, widths),
        grid=(N_TOK // tm,),
        in_specs=_mixer_in_specs(layer, tm, sum(widths)),
        out_specs=[pl.BlockSpec((tm, wd), lambda i: (i, 0)) for wd in widths],
        out_shape=[jax.ShapeDtypeStruct((N_TOK, wd), F32) for wd in widths],
        compiler_params=_cparams(("parallel",)),
        name="attn_in",
    )(x, mod, g.reshape(1, D_MODEL), w_in)


def _gla_in(x, mod, layer, g, w_in, w_gup, b_gup):
    tm = TM_PROJ
    widths = (GLA_KDIM, GLA_KDIM, GLA_VDIM, GLA_VDIM)
    out_w = widths + (GLA_KDIM, GLA_KDIM)
    n_in = sum(widths) + GLA_D_PAD
    return pl.pallas_call(
        functools.partial(_gla_in_kernel, widths),
        grid=(N_TOK // tm,),
        in_specs=_mixer_in_specs(layer, tm, n_in) + [
            pl.BlockSpec((GLA_D_PAD, 2 * GLA_KDIM), lambda i: (0, 0)),
            pl.BlockSpec((1, 2 * GLA_KDIM), lambda i: (0, 0)),
        ],
        out_specs=[pl.BlockSpec((tm, wd), lambda i: (i, 0)) for wd in out_w],
        out_shape=[jax.ShapeDtypeStruct((N_TOK, wd), F32) for wd in out_w],
        compiler_params=_cparams(("parallel",)),
        name="gla_in",
    )(x, mod, g.reshape(1, D_MODEL), w_in, w_gup, b_gup)


def _head_masks():
    lane = lax.broadcasted_iota(jnp.int32, (1, PAIR), 1)
    return lane < HEAD_DIM


def _pair_rms(x, g2, first):
    sq = x * x
    s0 = jnp.sum(jnp.where(first, sq, 0.0), axis=-1, keepdims=True)
    s1 = jnp.sum(jnp.where(first, 0.0, sq), axis=-1, keepdims=True)
    ms = jnp.where(first, s0, s1) * (1.0 / HEAD_DIM)
    return x * lax.rsqrt(ms + EPS) * g2


def _pair_rope(x, cos, sin_signed):
    lane = lax.broadcasted_iota(jnp.int32, x.shape, 1)
    quarter = HEAD_DIM // 4
    partner = jnp.where(lane % (2 * quarter) < quarter,
                        pltpu.roll(x, PAIR - quarter, 1), pltpu.roll(x, quarter, 1))
    return x * cos + partner * sin_signed


def _dup_head(x, j, first):
    sw = pltpu.roll(x, HEAD_DIM, 1)
    if j == 0:
        return jnp.where(first, x, sw)
    return jnp.where(first, sw, x)


def _attn_kernel(cfg, *refs):
    norm, rope, gqa, ctx, bias, emit_k = cfg
    it = iter(refs)
    q_ref, k_ref, v_ref = next(it), next(it), next(it)
    kc_ref = vc_ref = b_ref = gq_ref = gk_ref = cq_ref = sq_ref = ck_ref = sk_ref = kn_ref = None
    if ctx:
        kc_ref, vc_ref = next(it), next(it)
    if bias:
        b_ref = next(it)
    if norm:
        gq_ref, gk_ref = next(it), next(it)
    if rope:
        cq_ref, sq_ref, ck_ref, sk_ref = next(it), next(it), next(it), next(it)
    o_ref = next(it)
    if emit_k:
        kn_ref = next(it)

    first = _head_masks()
    tq = q_ref.shape[0]
    n_half = q_ref.shape[1] // PAIR
    scale = HEAD_DIM ** -0.5

    k = k_ref[...]
    if norm:
        k = _pair_rms(k, gk_ref[...], first)
    if rope:
        k = _pair_rope(k, ck_ref[...], sk_ref[...])
    if emit_k:
        kn_ref[...] = k
    v = v_ref[...]
    kc = kc_ref[...] if ctx else None
    vc = vc_ref[...] if ctx else None

    for j in range(n_half):
        q = q_ref[:, j * PAIR:(j + 1) * PAIR]
        if norm:
            q = _pair_rms(q, gq_ref[...], first)
        if rope:
            q = _pair_rope(q, cq_ref[...], sq_ref[...])
        q = q * scale
        qs = jnp.concatenate([jnp.where(first, q, 0.0), jnp.where(first, 0.0, q)], axis=0).astype(BF16)
        if gqa:
            kj, vj = _dup_head(k, j, first), _dup_head(v, j, first)
            kcj = _dup_head(kc, j, first) if ctx else None
            vcj = _dup_head(vc, j, first) if ctx else None
        else:
            kj, vj, kcj, vcj = k, v, kc, vc
        s = _dot_nt(qs, kj.astype(BF16))
        if bias:
            s = s + b_ref[...].reshape(2 * tq, s.shape[1])
        m = jnp.max(s, axis=-1, keepdims=True)
        if ctx:
            sc = _dot_nt(qs, kcj.astype(BF16))
            m = jnp.maximum(m, jnp.max(sc, axis=-1, keepdims=True))
        p = jnp.exp(s - m)
        l = jnp.sum(p, axis=-1, keepdims=True)
        o = _dot(p.astype(BF16), vj.astype(BF16))
        if ctx:
            pc = jnp.exp(sc - m)
            l = l + jnp.sum(pc, axis=-1, keepdims=True)
            o = o + _dot(pc.astype(BF16), vcj.astype(BF16))
        o = o / l
        o_ref[:, j * PAIR:(j + 1) * PAIR] = jnp.where(first, o[:tq], o[tq:])


def _attention(q, k, v, *, lat, gqa, layer_i=0, cache_k=None, cache_v=None, bias=None,
               gq=None, gk=None, rope_tabs=None):
    norm = gq is not None
    rope = rope_tabs is not None
    has_bias = bias is not None
    emit_k = norm and not lat
    n_half = 2 if gqa else 1
    n_pairs = k.shape[1] // PAIR
    if lat:
        nb, t, tq = DEC_BATCH, DEC_SEQ, TQ_LAT
        row0 = N_CTX
    else:
        nb, t, tq = BATCH, SEQ, SEQ
        row0 = 0
    nq = t // tq
    qoff, koff = row0 // tq, row0 // t
    grid = (n_pairs, nq, nb)
    in_specs = [
        pl.BlockSpec((tq, n_half * PAIR), lambda p, qi, b: (qoff + b * nq + qi, p)),
        pl.BlockSpec((t, PAIR), lambda p, qi, b: (koff + b, p)),
        pl.BlockSpec((t, PAIR), lambda p, qi, b: (koff + b, p)),
    ]
    args = [q, k, v]
    if lat:
        in_specs += [pl.BlockSpec((None, None, PAST_LEN, PAIR), lambda p, qi, b: (b, layer_i, 0, p))] * 2
        args += [cache_k, cache_v]
    if has_bias:
        in_specs.append(pl.BlockSpec((2, tq, t), lambda p, qi, b: (p, qi, 0)))
        args.append(bias)
    if norm:
        in_specs += [pl.BlockSpec((1, PAIR), lambda p, qi, b: (0, 0))] * 2
        args += [gq, gk]
    if rope:
        cos, sin = rope_tabs
        in_specs += [pl.BlockSpec((tq, PAIR), lambda p, qi, b: (qi, 0))] * 2
        in_specs += [pl.BlockSpec((t, PAIR), lambda p, qi, b: (0, 0))] * 2
        args += [cos, sin, cos, sin]
    n_rows = nb * t
    out_specs = [pl.BlockSpec((tq, n_half * PAIR), lambda p, qi, b: (b * nq + qi, p))]
    out_shape = [jax.ShapeDtypeStruct((n_rows, q.shape[1]), F32)]
    if emit_k:
        out_specs.append(pl.BlockSpec((t, PAIR), lambda p, qi, b: (b, p)))
        out_shape.append(jax.ShapeDtypeStruct((n_rows, k.shape[1]), F32))
    cfg = (norm, rope, gqa, lat, has_bias, emit_k)
    return pl.pallas_call(
        functools.partial(_attn_kernel, cfg),
        grid=grid,
        in_specs=in_specs,
        out_specs=out_specs,
        out_shape=out_shape,
        compiler_params=_cparams(("parallel", "parallel", "parallel")),
        name="attn_" + ("lat" if lat else "ctx") + ("_a" if gqa else "_b"),
    )(*args)


def _rope_tables():
    half = HEAD_DIM // 2
    freqs = ROPE_THETA ** (-np.arange(0, half, 2, dtype=np.float32) / half)
    t = np.arange(DEC_SEQ)

    def tab(pos):
        ang = pos.astype(np.float32)[:, None] * freqs[None, :]
        cos = np.concatenate([np.cos(ang), np.cos(ang)], -1)
        sin = np.concatenate([-np.sin(ang), np.sin(ang)], -1)
        return cos, sin

    cr, sr = tab(t // GRID_W)
    cc, sc = tab(t % GRID_W)
    cos = np.concatenate([cr, cc] * 2, -1).astype(np.float32)
    sin = np.concatenate([sr, sc] * 2, -1).astype(np.float32)
    return jnp.asarray(cos), jnp.asarray(sin)


def _na_bias(rel_bias):
    rows = DEC_SEQ // GRID_W
    wh = min(NA_WIN_H, rows)
    t = np.arange(DEC_SEQ)
    r, c = t // GRID_W, t % GRID_W
    rs = np.clip(r - wh // 2, 0, rows - wh)
    ws = np.clip(c - NA_WIN_W // 2, 0, GRID_W - NA_WIN_W)
    in_win = ((r[None, :] >= rs[:, None]) & (r[None, :] < rs[:, None] + wh)
              & (c[None, :] >= ws[:, None]) & (c[None, :] < ws[:, None] + NA_WIN_W))
    dr = np.clip(r[None, :] - r[:, None] + NA_WIN_H - 1, 0, 2 * NA_WIN_H - 2)
    dc = np.clip(c[None, :] - c[:, None] + NA_WIN_W - 1, 0, 2 * NA_WIN_W - 2)
    idx = (dr * (2 * NA_WIN_W - 1) + dc).astype(np.int32).reshape(-1)
    flat = rel_bias.reshape(B_HEADS, -1)
    dense = jnp.take(flat, jnp.asarray(idx), axis=1).reshape(B_HEADS, DEC_SEQ, DEC_SEQ)
    return jnp.where(jnp.asarray(in_win)[None], dense, -jnp.inf)


def _gla_kernel(has_s0, emit_s, *refs):
    it = iter(refs)
    q_ref, k_ref, v_ref, lgf_ref, lgb_ref = (next(it) for _ in range(5))
    s0_ref = next(it) if has_s0 else None
    o_ref = next(it)
    sf_ref = next(it) if emit_s else None
    st_ref = next(it)

    c_len = GLA_CHUNK
    n = q_ref.shape[0] // c_len
    row = lax.broadcasted_iota(jnp.int32, (c_len, c_len), 0)
    col = lax.broadcasted_iota(jnp.int32, (c_len, c_len), 1)
    scale = GLA_DK ** -0.5

    def chunk(c, lg_ref, keep, edge, accumulate):
        r = pl.ds(pl.multiple_of(c * c_len, c_len), c_len)
        tri = jnp.where(keep, 1.0, 0.0).astype(F32)
        b = jnp.dot(tri, lg_ref[r, :], precision=lax.Precision.HIGHEST, preferred_element_type=F32)
        b_edge = b[edge:edge + 1, :]
        q = q_ref[r, :] * scale
        k = k_ref[r, :]
        v = v_ref[r, :].astype(BF16)
        q_in = (q * jnp.exp(b)).astype(BF16)
        k_in = (k * jnp.exp(-b)).astype(BF16)
        k_out = (k * jnp.exp(b_edge - b)).astype(BF16)
        a = jnp.where(keep, _dot_nt(q_in, k_in), 0.0).astype(BF16)
        st = st_ref[...]
        o = _dot(a, v) + _dot_nt(q_in, st.astype(BF16))
        st_ref[...] = st * jnp.exp(b_edge) + _dot_tn(v, k_out)
        if accumulate:
            o_ref[r, :] += o
        else:
            o_ref[r, :] = o

    def run(direction):
        if has_s0:
            st_ref[...] = s0_ref[direction].T
        else:
            st_ref[...] = jnp.zeros_like(st_ref)
        if direction == 0:
            keep, edge, lg_ref = col <= row, c_len - 1, lgf_ref
            lax.fori_loop(0, n, lambda c, _: chunk(c, lg_ref, keep, edge, False), None)
        else:
            keep, edge, lg_ref = col >= row, 0, lgb_ref
            lax.fori_loop(0, n, lambda c, _: chunk(n - 1 - c, lg_ref, keep, edge, True), None)
        if emit_s:
            sf_ref[direction] = st_ref[...].T

    run(0)
    run(1)


def _gla(q, k, v, lgf, lgb, *, lat, layer_j=0, state=None):
    if lat:
        nb, t, row0 = DEC_BATCH, DEC_SEQ, N_CTX
    else:
        nb, t, row0 = BATCH, SEQ, 0
    off = row0 // t
    kspec = pl.BlockSpec((t, GLA_DK), lambda b, h: (off + b, h))
    in_specs = [kspec, kspec, pl.BlockSpec((t, GLA_DV), lambda b, h: (off + b, h)), kspec, kspec]
    args = [q, k, v, lgf, lgb]
    if lat:
        in_specs.append(pl.BlockSpec((None, None, 2, None, GLA_DK, GLA_DV),
                                     lambda b, h: (b, layer_j, 0, h, 0, 0)))
        args.append(state)
    out_specs = [pl.BlockSpec((t, GLA_DV), lambda b, h: (b, h))]
    out_shape = [jax.ShapeDtypeStruct((nb * t, GLA_VDIM), F32)]
    if not lat:
        out_specs.append(pl.BlockSpec((None, 2, None, GLA_DK, GLA_DV), lambda b, h: (b, 0, h, 0, 0)))
        out_shape.append(jax.ShapeDtypeStruct((nb, 2, GLA_HEADS, GLA_DK, GLA_DV), F32))
    return pl.pallas_call(
        functools.partial(_gla_kernel, lat, not lat),
        grid=(nb, GLA_HEADS),
        in_specs=in_specs,
        out_specs=out_specs,
        out_shape=out_shape,
        scratch_shapes=[pltpu.VMEM((GLA_DV, GLA_DK), F32)],
        compiler_params=_cparams(("parallel", "parallel")),
        name="gla_lat" if lat else "gla_ctx",
    )(*args)


def _attn_out_kernel(x_ref, mod_ref, oa_ref, ob_ref, w_ref, o_ref):
    na = oa_ref.shape[1]
    y = _dot(oa_ref[...].astype(BF16), w_ref[:na, :]) + _dot(ob_ref[...].astype(BF16), w_ref[na:, :])
    o_ref[...] = x_ref[...] + mod_ref[2:3, :] * y


def _gla_out_kernel(x_ref, mod_ref, o_pre_ref, gate_ref, gn_ref, w_ref, o_ref):
    gn = gn_ref[...]
    parts = []
    for h in range(GLA_HEADS):
        sl = slice(h * GLA_DV, (h + 1) * GLA_DV)
        gt = gate_ref[:, sl]
        parts.append((_rms(o_pre_ref[:, sl], gn) * (gt * jax.nn.sigmoid(gt))).astype(BF16))
    y = _dot(jnp.concatenate(parts, axis=-1), w_ref[...])
    o_ref[...] = x_ref[...] + mod_ref[2:3, :] * y


def _mixer_out(kernel, name, x, mod, layer, acts, consts):
    tm = TM_PROJ
    in_specs = [
        pl.BlockSpec((tm, D_MODEL), lambda i: (i, 0)),
        pl.BlockSpec((None, None, None, 3, D_MODEL), lambda i: (layer, _cond_row(i, tm), 1, 0, 0)),
    ]
    in_specs += [pl.BlockSpec((tm, a.shape[1]), lambda i: (i, 0)) for a in acts]
    in_specs += [pl.BlockSpec(c.shape, lambda i: (0, 0)) for c in consts]
    return pl.pallas_call(
        kernel,
        grid=(N_TOK // tm,),
        in_specs=in_specs,
        out_specs=pl.BlockSpec((tm, D_MODEL), lambda i: (i, 0)),
        out_shape=jax.ShapeDtypeStruct((N_TOK, D_MODEL), F32),
        compiler_params=_cparams(("parallel",)),
        name=name,
    )(x, mod, *acts, *consts)


def _final_kernel(x_ref, g_ref, o_ref):
    o_ref[...] = _rms(x_ref[...], g_ref[...])


def _final_norm(x, g):
    tm = TM_PROJ
    return pl.pallas_call(
        _final_kernel,
        grid=(N_TOK // tm,),
        in_specs=[pl.BlockSpec((tm, D_MODEL), lambda i: (i, 0)),
                  pl.BlockSpec((1, D_MODEL), lambda i: (0, 0))],
        out_specs=pl.BlockSpec((tm, D_MODEL), lambda i: (i, 0)),
        out_shape=jax.ShapeDtypeStruct((N_TOK, D_MODEL), F32),
        compiler_params=_cparams(("parallel",)),
        name="final_norm",
    )(x, g.reshape(1, D_MODEL))


def kernel(x_prompt, x_sample, cache_a_k, cache_a_v, cache_b_k, cache_b_v, state_gla, c, c_ctx,
           w_mod, b_mod, g_norm, w_ffn_gate, w_ffn_up, w_ffn_down, w_attn_in, w_attn_out,
           g_qnorm, g_knorm, na_rel_bias, w_gla_in, w_gla_gup, b_gla_gup, g_gla_norm, w_gla_out, g_final):
    x = jnp.concatenate([x_prompt.reshape(N_CTX, D_MODEL), x_sample.reshape(N_LAT, D_MODEL)], axis=0)
    cond = jnp.concatenate([c_ctx[None, :], c, jnp.zeros((COND_ROWS - 1 - DEC_BATCH, D_MODEL), F32)], axis=0)
    mod = _modulation(cond, w_mod, b_mod)

    wg_all, wu_all, wd_all = (w.astype(BF16) for w in (w_ffn_gate, w_ffn_up, w_ffn_down))
    rope_tabs = _rope_tables()
    n_attn = w_attn_in.shape[0]
    ck_a = cache_a_k.reshape(DEC_BATCH, n_attn, PAST_LEN, A_KV_HEADS * HEAD_DIM)
    cv_a = cache_a_v.reshape(DEC_BATCH, n_attn, PAST_LEN, A_KV_HEADS * HEAD_DIM)
    ck_b = cache_b_k.reshape(DEC_BATCH, n_attn, PAST_LEN, B_HEADS * HEAD_DIM)
    cv_b = cache_b_v.reshape(DEC_BATCH, n_attn, PAST_LEN, B_HEADS * HEAD_DIM)

    new_a_k, new_a_v, new_b_k, new_b_v, new_gla = [], [], [], [], []
    for l in range(DEPTH):
        x = _ffn(x, mod, l, 0, g_norm[l, 0], wg_all[l, 0], wu_all[l, 0], wd_all[l, 0])
        if l % 2 == 0:
            i = l // 2
            qa, ka, va, qb, kb, vb = _attn_in(x, mod, l, g_norm[l, 1], w_attn_in[i].astype(BF16))
            gq = jnp.tile(g_qnorm[i], 2).reshape(1, PAIR)
            gk = jnp.tile(g_knorm[i], 2).reshape(1, PAIR)
            oa_c, ka_n = _attention(qa, ka, va, lat=False, gqa=True, gq=gq, gk=gk)
            (ob_c,) = _attention(qb, kb, vb, lat=False, gqa=False)
            (oa_l,) = _attention(qa, ka, va, lat=True, gqa=True, layer_i=i, cache_k=ck_a, cache_v=cv_a,
                                 gq=gq, gk=gk, rope_tabs=rope_tabs)
            (ob_l,) = _attention(qb, kb, vb, lat=True, gqa=False, layer_i=i, cache_k=ck_b, cache_v=cv_b,
                                 bias=_na_bias(na_rel_bias[i]))
            oa = jnp.concatenate([oa_c, oa_l], axis=0)
            ob = jnp.concatenate([ob_c, ob_l], axis=0)
            x = _mixer_out(_attn_out_kernel, "attn_out", x, mod, l, [oa, ob], [w_attn_out[i].astype(BF16)])
            new_a_k.append(ka_n.reshape(BATCH, SEQ, A_KV_HEADS, HEAD_DIM))
            new_a_v.append(va[:N_CTX].reshape(BATCH, SEQ, A_KV_HEADS, HEAD_DIM))
            new_b_k.append(kb[:N_CTX].reshape(BATCH, SEQ, B_HEADS, HEAD_DIM))
            new_b_v.append(vb[:N_CTX].reshape(BATCH, SEQ, B_HEADS, HEAD_DIM))
        else:
            j = l // 2
            w_in = jnp.pad(w_gla_in[j], ((0, 0), (0, GLA_D_PAD - 2 * GLA_RANK))).astype(BF16)
            w_gup = jnp.zeros((GLA_D_PAD, 2 * GLA_KDIM), F32)
            w_gup = w_gup.at[:GLA_RANK, :GLA_KDIM].set(w_gla_gup[j, 0])
            w_gup = w_gup.at[GLA_RANK:2 * GLA_RANK, GLA_KDIM:].set(w_gla_gup[j, 1]).astype(BF16)
            b_gup = b_gla_gup[j].reshape(1, 2 * GLA_KDIM)
            q, k, v, gt, lgf, lgb = _gla_in(x, mod, l, g_norm[l, 1], w_in, w_gup, b_gup)
            o_c, s_fin = _gla(q, k, v, lgf, lgb, lat=False)
            (o_l,) = _gla(q, k, v, lgf, lgb, lat=True, layer_j=j, state=state_gla)
            o_pre = jnp.concatenate([o_c, o_l], axis=0)
            x = _mixer_out(_gla_out_kernel, "gla_out", x, mod, l, [o_pre, gt],
                           [g_gla_norm[j].reshape(1, GLA_DV), w_gla_out[j].astype(BF16)])
            new_gla.append(s_fin)
        x = _ffn(x, mod, l, 2, g_norm[l, 2], wg_all[l, 1], wu_all[l, 1], wd_all[l, 1])

    y = _final_norm(x, g_final)
    return (y[:N_CTX].reshape(BATCH, SEQ, D_MODEL), y[N_CTX:].reshape(DEC_BATCH, DEC_SEQ, D_MODEL),
            jnp.stack(new_a_k, axis=1), jnp.stack(new_a_v, axis=1),
            jnp.stack(new_b_k, axis=1), jnp.stack(new_b_v, axis=1), jnp.stack(new_gla, axis=1))
```

```python
import functools

import numpy as np
import jax
import jax.numpy as jnp
from jax import lax
from jax.experimental import pallas as pl
from jax.experimental.pallas import tpu as pltpu

D_MODEL = 1024
BATCH = 32
SEQ = 256
DEPTH = 4
DEC_BATCH = 8
DEC_SEQ = 1024
PAST_LEN = 512
GRID_W = 64
HEAD_DIM = 64
A_HEADS = 8
A_KV_HEADS = 4
B_HEADS = 8
ROPE_THETA = 10000.0
NA_WIN_H = 8
NA_WIN_W = 16
GLA_HEADS = 4
GLA_DK = D_MODEL // 2 // GLA_HEADS
GLA_DV = D_MODEL // GLA_HEADS
GLA_KDIM = GLA_HEADS * GLA_DK
GLA_VDIM = GLA_HEADS * GLA_DV
GLA_RANK = 16
GLA_GATE_NORM = 16.0
GLA_CHUNK = 64
D_FF = ((8 * D_MODEL // 3 + 127) // 128) * 128
FFN_RES = 0.5
N_MOD = 9
EPS = 1e-6

N_CTX = BATCH * SEQ
N_LAT = DEC_BATCH * DEC_SEQ
N_TOK = N_CTX + N_LAT
COND_ROWS = 16
LANES = 128
PAIR = 2 * HEAD_DIM
GLA_D_PAD = LANES
VMEM_LIMIT = 56 * 1024 * 1024

F32 = jnp.float32
BF16 = jnp.bfloat16

TM_FFN = 1024
TF_FFN = 256
TM_PROJ = 512
TQ_LAT = 256


def _cparams(sem):
    return pltpu.CompilerParams(dimension_semantics=sem, vmem_limit_bytes=VMEM_LIMIT)


def _cond_row(i, tm):
    nct = N_CTX // tm
    per_batch = DEC_SEQ // tm
    return jnp.where(i < nct, 0, 1 + (i - nct) // per_batch)


def _rms(x, g):
    ms = jnp.mean(x * x, axis=-1, keepdims=True)
    return x * lax.rsqrt(ms + EPS) * g


def _dot(a, b):
    return jnp.dot(a, b, preferred_element_type=F32)


def _dot_nt(a, b):
    return lax.dot_general(a, b, (((1,), (1,)), ((), ())), preferred_element_type=F32)


def _dot_tn(a, b):
    return lax.dot_general(a, b, (((0,), (0,)), ((), ())), preferred_element_type=F32)


def _mod_kernel(c_ref, w_ref, b_ref, o_ref):
    c = c_ref[...]
    s = c * jax.nn.sigmoid(c)
    o_ref[...] = jnp.dot(s, w_ref[...], precision=lax.Precision.HIGHEST,
                         preferred_element_type=F32) + b_ref[...]


def _modulation(cond, w_mod, b_mod):
    tn = D_MODEL
    n_mod = N_MOD * D_MODEL
    out = pl.pallas_call(
        _mod_kernel,
        grid=(DEPTH, n_mod // tn),
        in_specs=[
            pl.BlockSpec((COND_ROWS, D_MODEL), lambda l, j: (0, 0)),
            pl.BlockSpec((None, D_MODEL, tn), lambda l, j: (l, 0, j)),
            pl.BlockSpec((None, 1, tn), lambda l, j: (l, 0, j)),
        ],
        out_specs=pl.BlockSpec((None, COND_ROWS, tn), lambda l, j: (l, 0, j)),
        out_shape=jax.ShapeDtypeStruct((DEPTH, COND_ROWS, n_mod), F32),
        compiler_params=_cparams(("parallel", "parallel")),
        name="adaln_mod",
    )(cond, w_mod, b_mod.reshape(DEPTH, 1, n_mod))
    return out.reshape(DEPTH, COND_ROWS, 3, 3, D_MODEL)


def _ffn_kernel(x_ref, mod_ref, g_ref, wg_ref, wu_ref, wd_ref, o_ref, h_ref, acc_ref):
    j = pl.program_id(1)

    @pl.when(j == 0)
    def _():
        y = _rms(x_ref[...], g_ref[...])
        h_ref[...] = (y * (1.0 + mod_ref[1:2, :]) + mod_ref[0:1, :]).astype(BF16)
        acc_ref[...] = jnp.zeros_like(acc_ref)

    h = h_ref[...]
    g = _dot(h, wg_ref[...])
    u = _dot(h, wu_ref[...])
    a = (g * jax.nn.sigmoid(g) * u).astype(BF16)
    acc_ref[...] += _dot(a, wd_ref[...])

    @pl.when(j == pl.num_programs(1) - 1)
    def _():
        o_ref[...] = x_ref[...] + FFN_RES * mod_ref[2:3, :] * acc_ref[...]


def _ffn(x, mod, layer, sub, g, wg, wu, wd):
    tm, tf = TM_FFN, TF_FFN
    return pl.pallas_call(
        _ffn_kernel,
        grid=(N_TOK // tm, D_FF // tf),
        in_specs=[
            pl.BlockSpec((tm, D_MODEL), lambda i, j: (i, 0)),
            pl.BlockSpec((None, None, None, 3, D_MODEL),
                         lambda i, j: (layer, _cond_row(i, tm), sub, 0, 0)),
            pl.BlockSpec((1, D_MODEL), lambda i, j: (0, 0)),
            pl.BlockSpec((D_MODEL, tf), lambda i, j: (0, j)),
            pl.BlockSpec((D_MODEL, tf), lambda i, j: (0, j)),
            pl.BlockSpec((tf, D_MODEL), lambda i, j: (j, 0)),
        ],
        out_specs=pl.BlockSpec((tm, D_MODEL), lambda i, j: (i, 0)),
        out_shape=jax.ShapeDtypeStruct((N_TOK, D_MODEL), F32),
        scratch_shapes=[pltpu.VMEM((tm, D_MODEL), BF16), pltpu.VMEM((tm, D_MODEL), F32)],
        compiler_params=_cparams(("parallel", "arbitrary")),
        name="ffn",
    )(x, mod, g.reshape(1, D_MODEL), wg, wu, wd)


def _mixer_h(x_ref, mod_ref, g_ref):
    y = _rms(x_ref[...], g_ref[...])
    return (y * (1.0 + mod_ref[1:2, :]) + mod_ref[0:1, :]).astype(BF16)


def _attn_in_kernel(widths, x_ref, mod_ref, g_ref, w_ref, *o_refs):
    h = _mixer_h(x_ref, mod_ref, g_ref)
    off = 0
    for o_ref, wd in zip(o_refs, widths):
        o_ref[...] = _dot(h, w_ref[:, off:off + wd])
        off += wd


def _gla_in_kernel(widths, x_ref, mod_ref, g_ref, w_ref, wgup_ref, bgup_ref, *o_refs):
    h = _mixer_h(x_ref, mod_ref, g_ref)
    off = 0
    for o_ref, wd in zip(o_refs[:4], widths):
        o_ref[...] = _dot(h, w_ref[:, off:off + wd])
        off += wd
    d = _dot(h, w_ref[:, off:off + GLA_D_PAD]).astype(BF16)
    pre = _dot(d, wgup_ref[...]) + bgup_ref[...]
    lg = (jnp.minimum(pre, 0.0) - jnp.log1p(jnp.exp(-jnp.abs(pre)))) * (1.0 / GLA_GATE_NORM)
    o_refs[4][...] = lg[:, :GLA_KDIM]
    o_refs[5][...] = lg[:, GLA_KDIM:]


def _mixer_in_specs(layer, tm, n_in):
    return [
        pl.BlockSpec((tm, D_MODEL), lambda i: (i, 0)),
        pl.BlockSpec((None, None, None, 3, D_MODEL), lambda i: (layer, _cond_row(i, tm), 1, 0, 0)),
        pl.BlockSpec((1, D_MODEL), lambda i: (0, 0)),
        pl.BlockSpec((D_MODEL, n_in), lambda i: (0, 0)),
    ]


def _attn_in(x, mod, layer, g, w_in):
    tm = TM_PROJ
    widths = (A_HEADS * HEAD_DIM, A_KV_HEADS * HEAD_DIM, A_KV_HEADS * HEAD_DIM,
              B_HEADS * HEAD_DIM, B_HEADS * HEAD_DIM, B_HEADS * HEAD_DIM)
    return pl.pallas_call(
        functools.partial(_attn_in_kernel, widths),
        grid=(N_TOK // tm,),
        in_specs=_mixer_in_specs(layer, tm, sum(widths)),
        out_specs=[pl.BlockSpec((tm, wd), lambda i: (i, 0)) for wd in widths],
        out_shape=[jax.ShapeDtypeStruct((N_TOK, wd), F32) for wd in widths],
        compiler_params=_cparams(("parallel",)),
        name="attn_in",
    )(x, mod, g.reshape(1, D_MODEL), w_in)


def _gla_in(x, mod, layer, g, w_in, w_gup, b_gup):
    tm = TM_PROJ
    widths = (GLA_KDIM, GLA_KDIM, GLA_VDIM, GLA_VDIM)
    out_w = widths + (GLA_KDIM, GLA_KDIM)
    n_in = sum(widths) + GLA_D_PAD
    return pl.pallas_call(
        functools.partial(_gla_in_kernel, widths),
        grid=(N_TOK // tm,),
        in_specs=_mixer_in_specs(layer, tm, n_in) + [
            pl.BlockSpec((GLA_D_PAD, 2 * GLA_KDIM), lambda i: (0, 0)),
            pl.BlockSpec((1, 2 * GLA_KDIM), lambda i: (0, 0)),
        ],
        out_specs=[pl.BlockSpec((tm, wd), lambda i: (i, 0)) for wd in out_w],
        out_shape=[jax.ShapeDtypeStruct((N_TOK, wd), F32) for wd in out_w],
        compiler_params=_cparams(("parallel",)),
        name="gla_in",
    )(x, mod, g.reshape(1, D_MODEL), w_in, w_gup, b_gup)


def _head_masks():
    lane = lax.broadcasted_iota(jnp.int32, (1, PAIR), 1)
    return lane < HEAD_DIM


def _pair_rms(x, g2, first):
    sq = x * x
    s0 = jnp.sum(jnp.where(first, sq, 0.0), axis=-1, keepdims=True)
    s1 = jnp.sum(jnp.where(first, 0.0, sq), axis=-1, keepdims=True)
    ms = jnp.where(first, s0, s1) * (1.0 / HEAD_DIM)
    return x * lax.rsqrt(ms + EPS) * g2


def _pair_rope(x, cos, sin_signed):
    lane = lax.broadcasted_iota(jnp.int32, x.shape, 1)
    quarter = HEAD_DIM // 4
    partner = jnp.where(lane % (2 * quarter) < quarter,
                        pltpu.roll(x, PAIR - quarter, 1), pltpu.roll(x, quarter, 1))
    return x * cos + partner * sin_signed


def _dup_head(x, j, first):
    sw = pltpu.roll(x, HEAD_DIM, 1)
    if j == 0:
        return jnp.where(first, x, sw)
    return jnp.where(first, sw, x)


def _fill_na_bias(tab_ref, bias_scr, qi, tq, first):
    rows = DEC_SEQ // GRID_W
    wh = min(NA_WIN_H, rows)
    neg = -jnp.inf
    for a in range(tq // GRID_W):
        qr = qi * (tq // GRID_W) + a
        rs = jnp.clip(qr - wh // 2, 0, rows - wh)
        for m in range(rows // 2):
            kr0, kr1 = 2 * m, 2 * m + 1
            ok0 = jnp.logical_and(kr0 >= rs, kr0 < rs + wh)
            ok1 = jnp.logical_and(kr1 >= rs, kr1 < rs + wh)
            d0 = jnp.clip(kr0 - qr + NA_WIN_H - 1, 0, 2 * NA_WIN_H - 2)
            d1 = jnp.clip(kr1 - qr + NA_WIN_H - 1, 0, 2 * NA_WIN_H - 2)
            for hh in range(2):
                t0 = jnp.where(ok0, tab_ref[hh, d0], neg)
                t1 = jnp.where(ok1, tab_ref[hh, d1], neg)
                bias_scr[hh * tq + a * GRID_W:hh * tq + (a + 1) * GRID_W, m * PAIR:(m + 1) * PAIR] = (
                    jnp.where(first, t0, t1))


def _attn_kernel(cfg, *refs):
    norm, rope, gqa, ctx, bias, emit_k = cfg
    it = iter(refs)
    q_ref, k_ref, v_ref = next(it), next(it), next(it)
    kc_ref = vc_ref = b_ref = gq_ref = gk_ref = cq_ref = sq_ref = ck_ref = sk_ref = kn_ref = None
    if ctx:
        kc_ref, vc_ref = next(it), next(it)
    if bias:
        b_ref = next(it)
    if norm:
        gq_ref, gk_ref = next(it), next(it)
    if rope:
        cq_ref, sq_ref, ck_ref, sk_ref = next(it), next(it), next(it), next(it)
    o_ref = next(it)
    if emit_k:
        kn_ref = next(it)
    bias_scr = next(it) if bias else None

    first = _head_masks()
    tq = q_ref.shape[0]
    n_half = q_ref.shape[1] // PAIR
    scale = HEAD_DIM ** -0.5

    k = k_ref[...]
    if norm:
        k = _pair_rms(k, gk_ref[...], first)
    if rope:
        k = _pair_rope(k, ck_ref[...], sk_ref[...])
    if emit_k:
        kn_ref[...] = k
    v = v_ref[...]
    kc = kc_ref[...] if ctx else None
    vc = vc_ref[...] if ctx else None

    if bias:
        @pl.when(pl.program_id(2) == 0)
        def _():
            _fill_na_bias(b_ref, bias_scr, pl.program_id(1), tq, first)

    for j in range(n_half):
        q = q_ref[:, j * PAIR:(j + 1) * PAIR]
        if norm:
            q = _pair_rms(q, gq_ref[...], first)
        if rope:
            q = _pair_rope(q, cq_ref[...], sq_ref[...])
        q = q * scale
        qs = jnp.concatenate([jnp.where(first, q, 0.0), jnp.where(first, 0.0, q)], axis=0).astype(BF16)
        if gqa:
            kj, vj = _dup_head(k, j, first), _dup_head(v, j, first)
            kcj = _dup_head(kc, j, first) if ctx else None
            vcj = _dup_head(vc, j, first) if ctx else None
        else:
            kj, vj, kcj, vcj = k, v, kc, vc
        s = _dot_nt(qs, kj.astype(BF16))
        if bias:
            s = s + bias_scr[...]
        m = jnp.max(s, axis=-1, keepdims=True)
        if ctx:
            sc = _dot_nt(qs, kcj.astype(BF16))
            m = jnp.maximum(m, jnp.max(sc, axis=-1, keepdims=True))
        p = jnp.exp(s - m)
        l = jnp.sum(p, axis=-1, keepdims=True)
        o = _dot(p.astype(BF16), vj.astype(BF16))
        if ctx:
            pc = jnp.exp(sc - m)
            l = l + jnp.sum(pc, axis=-1, keepdims=True)
            o = o + _dot(pc.astype(BF16), vcj.astype(BF16))
        o = o / l
        o_ref[:, j * PAIR:(j + 1) * PAIR] = jnp.where(first, o[:tq], o[tq:])


def _attention(q, k, v, *, lat, gqa, layer_i=0, cache_k=None, cache_v=None, bias=None,
               gq=None, gk=None, rope_tabs=None):
    norm = gq is not None
    rope = rope_tabs is not None
    has_bias = bias is not None
    emit_k = norm and not lat
    n_half = 2 if gqa else 1
    n_pairs = k.shape[1] // PAIR
    if lat:
        nb, t, tq = DEC_BATCH, DEC_SEQ, TQ_LAT
        row0 = N_CTX
    else:
        nb, t, tq = BATCH, SEQ, SEQ
        row0 = 0
    nq = t // tq
    qoff, koff = row0 // tq, row0 // t
    grid = (n_pairs, nq, nb)
    in_specs = [
        pl.BlockSpec((tq, n_half * PAIR), lambda p, qi, b: (qoff + b * nq + qi, p)),
        pl.BlockSpec((t, PAIR), lambda p, qi, b: (koff + b, p)),
        pl.BlockSpec((t, PAIR), lambda p, qi, b: (koff + b, p)),
    ]
    args = [q, k, v]
    if lat:
        in_specs += [pl.BlockSpec((None, None, PAST_LEN, PAIR), lambda p, qi, b: (b, layer_i, 0, p))] * 2
        args += [cache_k, cache_v]
    if has_bias:
        in_specs.append(pl.BlockSpec((2,) + bias.shape[1:], lambda p, qi, b: (p, 0, 0, 0)))
        args.append(bias)
    if norm:
        in_specs += [pl.BlockSpec((1, PAIR), lambda p, qi, b: (0, 0))] * 2
        args += [gq, gk]
    if rope:
        cos, sin = rope_tabs
        in_specs += [pl.BlockSpec((tq, PAIR), lambda p, qi, b: (qi, 0))] * 2
        in_specs += [pl.BlockSpec((t, PAIR), lambda p, qi, b: (0, 0))] * 2
        args += [cos, sin, cos, sin]
    n_rows = nb * t
    out_specs = [pl.BlockSpec((tq, n_half * PAIR), lambda p, qi, b: (b * nq + qi, p))]
    out_shape = [jax.ShapeDtypeStruct((n_rows, q.shape[1]), F32)]
    if emit_k:
        out_specs.append(pl.BlockSpec((t, PAIR), lambda p, qi, b: (b, p)))
        out_shape.append(jax.ShapeDtypeStruct((n_rows, k.shape[1]), F32))
    cfg = (norm, rope, gqa, lat, has_bias, emit_k)
    return pl.pallas_call(
        functools.partial(_attn_kernel, cfg),
        grid=grid,
        in_specs=in_specs,
        out_specs=out_specs,
        out_shape=out_shape,
        scratch_shapes=[pltpu.VMEM((2 * tq, t), F32)] if has_bias else [],
        compiler_params=_cparams(("parallel", "parallel", "arbitrary")),
        name="attn_" + ("lat" if lat else "ctx") + ("_a" if gqa else "_b"),
    )(*args)


def _rope_tables():
    half = HEAD_DIM // 2
    freqs = ROPE_THETA ** (-np.arange(0, half, 2, dtype=np.float32) / half)
    t = np.arange(DEC_SEQ)

    def tab(pos):
        ang = pos.astype(np.float32)[:, None] * freqs[None, :]
        cos = np.concatenate([np.cos(ang), np.cos(ang)], -1)
        sin = np.concatenate([-np.sin(ang), np.sin(ang)], -1)
        return cos, sin

    cr, sr = tab(t // GRID_W)
    cc, sc = tab(t % GRID_W)
    cos = np.concatenate([cr, cc] * 2, -1).astype(np.float32)
    sin = np.concatenate([sr, sc] * 2, -1).astype(np.float32)
    return jnp.asarray(cos), jnp.asarray(sin)


def _na_bias_table(rel_bias):
    w = GRID_W
    n_rel = 2 * NA_WIN_W - 1
    left = w - NA_WIN_W
    rp = jnp.pad(rel_bias, ((0, 0), (0, 0), (left, 2 * w - n_rel - left)))
    flat = jnp.tile(rp, (1, 1, w))
    m = flat[..., :w * (2 * w - 1)].reshape(rel_bias.shape[0], rel_bias.shape[1], w, 2 * w - 1)[..., w - 1:]
    c = np.arange(w)
    ws = np.clip(c - NA_WIN_W // 2, 0, w - NA_WIN_W)
    in_col = (c[None, :] >= ws[:, None]) & (c[None, :] < ws[:, None] + NA_WIN_W)
    m = jnp.where(jnp.asarray(in_col)[None, None], m, -jnp.inf)
    return jnp.concatenate([m, m], axis=-1)


def _gla_kernel(has_s0, emit_s, *refs):
    it = iter(refs)
    q_ref, k_ref, v_ref, lgf_ref, lgb_ref = (next(it) for _ in range(5))
    s0_ref = next(it) if has_s0 else None
    o_ref = next(it)
    sf_ref = next(it) if emit_s else None
    st_ref = next(it)

    c_len = GLA_CHUNK
    n = q_ref.shape[0] // c_len
    row = lax.broadcasted_iota(jnp.int32, (c_len, c_len), 0)
    col = lax.broadcasted_iota(jnp.int32, (c_len, c_len), 1)
    scale = GLA_DK ** -0.5

    def chunk(c, lg_ref, keep, edge, accumulate):
        r = pl.ds(pl.multiple_of(c * c_len, c_len), c_len)
        tri = jnp.where(keep, 1.0, 0.0).astype(F32)
        b = jnp.dot(tri, lg_ref[r, :], precision=lax.Precision.HIGHEST, preferred_element_type=F32)
        b_edge = b[edge:edge + 1, :]
        q = q_ref[r, :] * scale
        k = k_ref[r, :]
        v = v_ref[r, :].astype(BF16)
        q_in = (q * jnp.exp(b)).astype(BF16)
        k_in = (k * jnp.exp(-b)).astype(BF16)
        k_out = (k * jnp.exp(b_edge - b)).astype(BF16)
        a = jnp.where(keep, _dot_nt(q_in, k_in), 0.0).astype(BF16)
        st = st_ref[...]
        o = _dot(a, v) + _dot_nt(q_in, st.astype(BF16))
        st_ref[...] = st * jnp.exp(b_edge) + _dot_tn(v, k_out)
        if accumulate:
            o_ref[r, :] += o
        else:
            o_ref[r, :] = o

    def run(direction):
        if has_s0:
            st_ref[...] = s0_ref[direction].T
        else:
            st_ref[...] = jnp.zeros_like(st_ref)
        if direction == 0:
            keep, edge, lg_ref = col <= row, c_len - 1, lgf_ref
            lax.fori_loop(0, n, lambda c, _: chunk(c, lg_ref, keep, edge, False), None)
        else:
            keep, edge, lg_ref = col >= row, 0, lgb_ref
            lax.fori_loop(0, n, lambda c, _: chunk(n - 1 - c, lg_ref, keep, edge, True), None)
        if emit_s:
            sf_ref[direction] = st_ref[...].T

    run(0)
    run(1)


def _gla(q, k, v, lgf, lgb, *, lat, layer_j=0, state=None):
    if lat:
        nb, t, row0 = DEC_BATCH, DEC_SEQ, N_CTX
    else:
        nb, t, row0 = BATCH, SEQ, 0
    off = row0 // t
    kspec = pl.BlockSpec((t, GLA_DK), lambda b, h: (off + b, h))
    in_specs = [kspec, kspec, pl.BlockSpec((t, GLA_DV), lambda b, h: (off + b, h)), kspec, kspec]
    args = [q, k, v, lgf, lgb]
    if lat:
        in_specs.append(pl.BlockSpec((None, None, 2, None, GLA_DK, GLA_DV),
                                     lambda b, h: (b, layer_j, 0, h, 0, 0)))
        args.append(state)
    out_specs = [pl.BlockSpec((t, GLA_DV), lambda b, h: (b, h))]
    out_shape = [jax.ShapeDtypeStruct((nb * t, GLA_VDIM), F32)]
    if not lat:
        out_specs.append(pl.BlockSpec((None, 2, None, GLA_DK, GLA_DV), lambda b, h: (b, 0, h, 0, 0)))
        out_shape.append(jax.ShapeDtypeStruct((nb, 2, GLA_HEADS, GLA_DK, GLA_DV), F32))
    return pl.pallas_call(
        functools.partial(_gla_kernel, lat, not lat),
        grid=(nb, GLA_HEADS),
        in_specs=in_specs,
        out_specs=out_specs,
        out_shape=out_shape,
        scratch_shapes=[pltpu.VMEM((GLA_DV, GLA_DK), F32)],
        compiler_params=_cparams(("parallel", "parallel")),
        name="gla_lat" if lat else "gla_ctx",
    )(*args)


def _group_pick(tm, c_ref, l_ref):
    return jnp.where(pl.program_id(0) < N_CTX // tm, c_ref[...], l_ref[...])


def _attn_out_kernel(tm, x_ref, mod_ref, oac_ref, oal_ref, obc_ref, obl_ref, w_ref, o_ref):
    oa = _group_pick(tm, oac_ref, oal_ref).astype(BF16)
    ob = _group_pick(tm, obc_ref, obl_ref).astype(BF16)
    na = oa.shape[1]
    y = _dot(oa, w_ref[:na, :]) + _dot(ob, w_ref[na:, :])
    o_ref[...] = x_ref[...] + mod_ref[2:3, :] * y


def _gla_out_kernel(tm, x_ref, mod_ref, oc_ref, ol_ref, gate_ref, gn_ref, w_ref, o_ref):
    gn = gn_ref[...]
    o_pre = _group_pick(tm, oc_ref, ol_ref)
    parts = []
    for h in range(GLA_HEADS):
        sl = slice(h * GLA_DV, (h + 1) * GLA_DV)
        gt = gate_ref[:, sl]
        parts.append((_rms(o_pre[:, sl], gn) * (gt * jax.nn.sigmoid(gt))).astype(BF16))
    y = _dot(jnp.concatenate(parts, axis=-1), w_ref[...])
    o_ref[...] = x_ref[...] + mod_ref[2:3, :] * y


def _mixer_out(kernel, name, x, mod, layer, group_acts, acts, consts):
    tm = TM_PROJ
    nct = N_CTX // tm
    in_specs = [
        pl.BlockSpec((tm, D_MODEL), lambda i: (i, 0)),
        pl.BlockSpec((None, None, None, 3, D_MODEL), lambda i: (layer, _cond_row(i, tm), 1, 0, 0)),
    ]
    flat = []
    for a_c, a_l in group_acts:
        in_specs.append(pl.BlockSpec((tm, a_c.shape[1]), lambda i: (jnp.minimum(i, nct - 1), 0)))
        in_specs.append(pl.BlockSpec((tm, a_l.shape[1]), lambda i: (jnp.maximum(i - nct, 0), 0)))
        flat += [a_c, a_l]
    in_specs += [pl.BlockSpec((tm, a.shape[1]), lambda i: (i, 0)) for a in acts]
    in_specs += [pl.BlockSpec(c.shape, lambda i: (0, 0)) for c in consts]
    return pl.pallas_call(
        functools.partial(kernel, tm),
        grid=(N_TOK // tm,),
        in_specs=in_specs,
        out_specs=pl.BlockSpec((tm, D_MODEL), lambda i: (i, 0)),
        out_shape=jax.ShapeDtypeStruct((N_TOK, D_MODEL), F32),
        compiler_params=_cparams(("parallel",)),
        name=name,
    )(x, mod, *flat, *acts, *consts)


def _final_kernel(x_ref, g_ref, o_ref):
    o_ref[...] = _rms(x_ref[...], g_ref[...])


def _final_norm(x, g, row0, n_rows):
    tm = TM_PROJ
    off = row0 // tm
    return pl.pallas_call(
        _final_kernel,
        grid=(n_rows // tm,),
        in_specs=[pl.BlockSpec((tm, D_MODEL), lambda i: (off + i, 0)),
                  pl.BlockSpec((1, D_MODEL), lambda i: (0, 0))],
        out_specs=pl.BlockSpec((tm, D_MODEL), lambda i: (i, 0)),
        out_shape=jax.ShapeDtypeStruct((n_rows, D_MODEL), F32),
        compiler_params=_cparams(("parallel",)),
        name="final_norm",
    )(x, g.reshape(1, D_MODEL))


def kernel(x_prompt, x_sample, cache_a_k, cache_a_v, cache_b_k, cache_b_v, state_gla, c, c_ctx,
           w_mod, b_mod, g_norm, w_ffn_gate, w_ffn_up, w_ffn_down, w_attn_in, w_attn_out,
           g_qnorm, g_knorm, na_rel_bias, w_gla_in, w_gla_gup, b_gla_gup, g_gla_norm, w_gla_out, g_final):
    x = jnp.concatenate([x_prompt.reshape(N_CTX, D_MODEL), x_sample.reshape(N_LAT, D_MODEL)], axis=0)
    cond = jnp.concatenate([c_ctx[None, :], c, jnp.zeros((COND_ROWS - 1 - DEC_BATCH, D_MODEL), F32)], axis=0)
    mod = _modulation(cond, w_mod, b_mod)

    wg_all, wu_all, wd_all = (w.astype(BF16) for w in (w_ffn_gate, w_ffn_up, w_ffn_down))
    rope_tabs = _rope_tables()
    n_attn = w_attn_in.shape[0]
    ck_a = cache_a_k.reshape(DEC_BATCH, n_attn, PAST_LEN, A_KV_HEADS * HEAD_DIM)
    cv_a = cache_a_v.reshape(DEC_BATCH, n_attn, PAST_LEN, A_KV_HEADS * HEAD_DIM)
    ck_b = cache_b_k.reshape(DEC_BATCH, n_attn, PAST_LEN, B_HEADS * HEAD_DIM)
    cv_b = cache_b_v.reshape(DEC_BATCH, n_attn, PAST_LEN, B_HEADS * HEAD_DIM)

    new_a_k, new_a_v, new_b_k, new_b_v, new_gla = [], [], [], [], []
    for l in range(DEPTH):
        x = _ffn(x, mod, l, 0, g_norm[l, 0], wg_all[l, 0], wu_all[l, 0], wd_all[l, 0])
        if l % 2 == 0:
            i = l // 2
            qa, ka, va, qb, kb, vb = _attn_in(x, mod, l, g_norm[l, 1], w_attn_in[i].astype(BF16))
            gq = jnp.tile(g_qnorm[i], 2).reshape(1, PAIR)
            gk = jnp.tile(g_knorm[i], 2).reshape(1, PAIR)
            oa_c, ka_n = _attention(qa, ka, va, lat=False, gqa=True, gq=gq, gk=gk)
            (ob_c,) = _attention(qb, kb, vb, lat=False, gqa=False)
            (oa_l,) = _attention(qa, ka, va, lat=True, gqa=True, layer_i=i, cache_k=ck_a, cache_v=cv_a,
                                 gq=gq, gk=gk, rope_tabs=rope_tabs)
            (ob_l,) = _attention(qb, kb, vb, lat=True, gqa=False, layer_i=i, cache_k=ck_b, cache_v=cv_b,
                                 bias=_na_bias_table(na_rel_bias[i]))
            x = _mixer_out(_attn_out_kernel, "attn_out", x, mod, l, [(oa_c, oa_l), (ob_c, ob_l)], [],
                           [w_attn_out[i].astype(BF16)])
            new_a_k.append(ka_n.reshape(BATCH, SEQ, A_KV_HEADS, HEAD_DIM))
            new_a_v.append(va[:N_CTX].reshape(BATCH, SEQ, A_KV_HEADS, HEAD_DIM))
            new_b_k.append(kb[:N_CTX].reshape(BATCH, SEQ, B_HEADS, HEAD_DIM))
            new_b_v.append(vb[:N_CTX].reshape(BATCH, SEQ, B_HEADS, HEAD_DIM))
        else:
            j = l // 2
            w_in = jnp.pad(w_gla_in[j], ((0, 0), (0, GLA_D_PAD - 2 * GLA_RANK))).astype(BF16)
            w_gup = jnp.zeros((GLA_D_PAD, 2 * GLA_KDIM), F32)
            w_gup = w_gup.at[:GLA_RANK, :GLA_KDIM].set(w_gla_gup[j, 0])
            w_gup = w_gup.at[GLA_RANK:2 * GLA_RANK, GLA_KDIM:].set(w_gla_gup[j, 1]).astype(BF16)
            b_gup = b_gla_gup[j].reshape(1, 2 * GLA_KDIM)
            q, k, v, gt, lgf, lgb = _gla_in(x, mod, l, g_norm[l, 1], w_in, w_gup, b_gup)
            o_c, s_fin = _gla(q, k, v, lgf, lgb, lat=False)
            (o_l,) = _gla(q, k, v, lgf, lgb, lat=True, layer_j=j, state=state_gla)
            x = _mixer_out(_gla_out_kernel, "gla_out", x, mod, l, [(o_c, o_l)], [gt],
                           [g_gla_norm[j].reshape(1, GLA_DV), w_gla_out[j].astype(BF16)])
            new_gla.append(s_fin)
        x = _ffn(x, mod, l, 2, g_norm[l, 2], wg_all[l, 1], wu_all[l, 1], wd_all[l, 1])

    y_ctx = _final_norm(x, g_final, 0, N_CTX)
    y_lat = _final_norm(x, g_final, N_CTX, N_LAT)
    return (y_ctx.reshape(BATCH, SEQ, D_MODEL), y_lat.reshape(DEC_BATCH, DEC_SEQ, D_MODEL),
            jnp.stack(new_a_k, axis=1), jnp.stack(new_a_v, axis=1),
            jnp.stack(new_b_k, axis=1), jnp.stack(new_b_v, axis=1), jnp.stack(new_gla, axis=1))
```

```python
import functools

import numpy as np
import jax
import jax.numpy as jnp
from jax import lax
from jax.experimental import pallas as pl
from jax.experimental.pallas import tpu as pltpu

D_MODEL = 1024
BATCH = 32
SEQ = 256
DEPTH = 4
DEC_BATCH = 8
DEC_SEQ = 1024
PAST_LEN = 512
GRID_W = 64
HEAD_DIM = 64
A_HEADS = 8
A_KV_HEADS = 4
B_HEADS = 8
ROPE_THETA = 10000.0
NA_WIN_H = 8
NA_WIN_W = 16
GLA_HEADS = 4
GLA_DK = D_MODEL // 2 // GLA_HEADS
GLA_DV = D_MODEL // GLA_HEADS
GLA_KDIM = GLA_HEADS * GLA_DK
GLA_VDIM = GLA_HEADS * GLA_DV
GLA_RANK = 16
GLA_GATE_NORM = 16.0
GLA_CHUNK = 64
D_FF = ((8 * D_MODEL // 3 + 127) // 128) * 128
FFN_RES = 0.5
N_MOD = 9
EPS = 1e-6

N_CTX = BATCH * SEQ
N_LAT = DEC_BATCH * DEC_SEQ
N_TOK = N_CTX + N_LAT
COND_ROWS = 16
LANES = 128
PAIR = 2 * HEAD_DIM
GLA_D_PAD = LANES
VMEM_LIMIT = 56 * 1024 * 1024

F32 = jnp.float32
BF16 = jnp.bfloat16

TM_FFN = 1024
TF_FFN = 256
TM_PROJ = 512
TQ_LAT = 256
GLA_UNROLL = 8


def _cparams(sem):
    return pltpu.CompilerParams(dimension_semantics=sem, vmem_limit_bytes=VMEM_LIMIT)


def _cond_row(i, tm):
    nct = N_CTX // tm
    per_batch = DEC_SEQ // tm
    return jnp.where(i < nct, 0, 1 + (i - nct) // per_batch)


def _rms(x, g):
    ms = jnp.mean(x * x, axis=-1, keepdims=True)
    return x * lax.rsqrt(ms + EPS) * g


def _dot(a, b):
    return jnp.dot(a, b, preferred_element_type=F32)


def _dot_nt(a, b):
    return lax.dot_general(a, b, (((1,), (1,)), ((), ())), preferred_element_type=F32)


def _dot_tn(a, b):
    return lax.dot_general(a, b, (((0,), (0,)), ((), ())), preferred_element_type=F32)


def _mod_kernel(c_ref, w_ref, b_ref, o_ref):
    c = c_ref[...]
    s = c * jax.nn.sigmoid(c)
    o_ref[...] = jnp.dot(s, w_ref[...], precision=lax.Precision.HIGHEST,
                         preferred_element_type=F32) + b_ref[...]


def _modulation(cond, w_mod, b_mod):
    tn = D_MODEL
    n_mod = N_MOD * D_MODEL
    out = pl.pallas_call(
        _mod_kernel,
        grid=(DEPTH, n_mod // tn),
        in_specs=[
            pl.BlockSpec((COND_ROWS, D_MODEL), lambda l, j: (0, 0)),
            pl.BlockSpec((None, D_MODEL, tn), lambda l, j: (l, 0, j)),
            pl.BlockSpec((None, 1, tn), lambda l, j: (l, 0, j)),
        ],
        out_specs=pl.BlockSpec((None, COND_ROWS, tn), lambda l, j: (l, 0, j)),
        out_shape=jax.ShapeDtypeStruct((DEPTH, COND_ROWS, n_mod), F32),
        compiler_params=_cparams(("parallel", "parallel")),
        name="adaln_mod",
    )(cond, w_mod, b_mod.reshape(DEPTH, 1, n_mod))
    return out.reshape(DEPTH, COND_ROWS, 3, 3, D_MODEL)


def _ffn_kernel(x_ref, mod_ref, g_ref, wg_ref, wu_ref, wd_ref, o_ref, h_ref, acc_ref):
    y = _rms(x_ref[...], g_ref[...])
    h_ref[...] = (y * (1.0 + mod_ref[1:2, :]) + mod_ref[0:1, :]).astype(BF16)
    acc_ref[...] = jnp.zeros_like(acc_ref)

    def ff_chunk(j, carry):
        h = h_ref[...]
        g = _dot(h, wg_ref[j])
        u = _dot(h, wu_ref[j])
        a = (g * jax.nn.sigmoid(g) * u).astype(BF16)
        acc_ref[...] += _dot(a, wd_ref[j])
        return carry

    lax.fori_loop(0, wg_ref.shape[0], ff_chunk, None)
    o_ref[...] = x_ref[...] + FFN_RES * mod_ref[2:3, :] * acc_ref[...]


def _ffn_weights(wg, wu, wd):
    lead = wg.shape[:-2]
    nf = D_FF // TF_FFN

    def cols(w):
        w = w.astype(BF16).reshape(lead + (D_MODEL, nf, TF_FFN))
        return jnp.moveaxis(w, -2, -3)

    return cols(wg), cols(wu), wd.astype(BF16).reshape(lead + (nf, TF_FFN, D_MODEL))


def _ffn(x, mod, layer, sub, g, wg, wu, wd):
    tm = TM_FFN
    resident = lambda shape: pl.BlockSpec(shape, lambda i: (0, 0, 0), pipeline_mode=pl.Buffered(1))
    return pl.pallas_call(
        _ffn_kernel,
        grid=(N_TOK // tm,),
        in_specs=[
            pl.BlockSpec((tm, D_MODEL), lambda i: (i, 0)),
            pl.BlockSpec((None, None, None, 3, D_MODEL), lambda i: (layer, _cond_row(i, tm), sub, 0, 0)),
            pl.BlockSpec((1, D_MODEL), lambda i: (0, 0)),
            resident(wg.shape), resident(wu.shape), resident(wd.shape),
        ],
        out_specs=pl.BlockSpec((tm, D_MODEL), lambda i: (i, 0)),
        out_shape=jax.ShapeDtypeStruct((N_TOK, D_MODEL), F32),
        scratch_shapes=[pltpu.VMEM((tm, D_MODEL), BF16), pltpu.VMEM((tm, D_MODEL), F32)],
        compiler_params=_cparams(("parallel",)),
        name="ffn",
    )(x, mod, g.reshape(1, D_MODEL), wg, wu, wd)


def _mixer_h(x_ref, mod_ref, g_ref):
    y = _rms(x_ref[...], g_ref[...])
    return (y * (1.0 + mod_ref[1:2, :]) + mod_ref[0:1, :]).astype(BF16)


def _attn_in_kernel(widths, x_ref, mod_ref, g_ref, w_ref, *o_refs):
    h = _mixer_h(x_ref, mod_ref, g_ref)
    off = 0
    for o_ref, wd in zip(o_refs, widths):
        o_ref[...] = _dot(h, w_ref[:, off:off + wd])
        off += wd


def _gla_in_kernel(widths, x_ref, mod_ref, g_ref, w_ref, wgup_ref, bgup_ref, *o_refs):
    h = _mixer_h(x_ref, mod_ref, g_ref)
    off = 0
    for o_ref, wd in zip(o_refs[:4], widths):
        o_ref[...] = _dot(h, w_ref[:, off:off + wd])
        off += wd
    d = _dot(h, w_ref[:, off:off + GLA_D_PAD]).astype(BF16)
    pre = _dot(d, wgup_ref[...]) + bgup_ref[...]
    lg = (jnp.minimum(pre, 0.0) - jnp.log1p(jnp.exp(-jnp.abs(pre)))) * (1.0 / GLA_GATE_NORM)
    o_refs[4][...] = lg[:, :GLA_KDIM]
    o_refs[5][...] = lg[:, GLA_KDIM:]


def _mixer_in_specs(layer, tm, n_in):
    return [
        pl.BlockSpec((tm, D_MODEL), lambda i: (i, 0)),
        pl.BlockSpec((None, None, None, 3, D_MODEL), lambda i: (layer, _cond_row(i, tm), 1, 0, 0)),
        pl.BlockSpec((1, D_MODEL), lambda i: (0, 0)),
        pl.BlockSpec((D_MODEL, n_in), lambda i: (0, 0)),
    ]


def _attn_in(x, mod, layer, g, w_in):
    tm = TM_PROJ
    widths = (A_HEADS * HEAD_DIM, A_KV_HEADS * HEAD_DIM, A_KV_HEADS * HEAD_DIM,
              B_HEADS * HEAD_DIM, B_HEADS * HEAD_DIM, B_HEADS * HEAD_DIM)
    return pl.pallas_call(
        functools.partial(_attn_in_kernel, widths),
        grid=(N_TOK // tm,),
        in_specs=_mixer_in_specs(layer, tm, sum(widths)),
        out_specs=[pl.BlockSpec((tm, wd), lambda i: (i, 0)) for wd in widths],
        out_shape=[jax.ShapeDtypeStruct((N_TOK, wd), F32) for wd in widths],
        compiler_params=_cparams(("parallel",)),
        name="attn_in",
    )(x, mod, g.reshape(1, D_MODEL), w_in)


def _gla_in(x, mod, layer, g, w_in, w_gup, b_gup):
    tm = TM_PROJ
    widths = (GLA_KDIM, GLA_KDIM, GLA_VDIM, GLA_VDIM)
    out_w = widths + (GLA_KDIM, GLA_KDIM)
    n_in = sum(widths) + GLA_D_PAD
    return pl.pallas_call(
        functools.partial(_gla_in_kernel, widths),
        grid=(N_TOK // tm,),
        in_specs=_mixer_in_specs(layer, tm, n_in) + [
            pl.BlockSpec((GLA_D_PAD, 2 * GLA_KDIM), lambda i: (0, 0)),
            pl.BlockSpec((1, 2 * GLA_KDIM), lambda i: (0, 0)),
        ],
        out_specs=[pl.BlockSpec((tm, wd), lambda i: (i, 0)) for wd in out_w],
        out_shape=[jax.ShapeDtypeStruct((N_TOK, wd), F32) for wd in out_w],
        compiler_params=_cparams(("parallel",)),
        name="gla_in",
    )(x, mod, g.reshape(1, D_MODEL), w_in, w_gup, b_gup)


def _head_masks():
    lane = lax.broadcasted_iota(jnp.int32, (1, PAIR), 1)
    return lane < HEAD_DIM


def _pair_rms(x, g2, first):
    sq = x * x
    s0 = jnp.sum(jnp.where(first, sq, 0.0), axis=-1, keepdims=True)
    s1 = jnp.sum(jnp.where(first, 0.0, sq), axis=-1, keepdims=True)
    ms = jnp.where(first, s0, s1) * (1.0 / HEAD_DIM)
    return x * lax.rsqrt(ms + EPS) * g2


def _pair_rope(x, cos, sin_signed):
    lane = lax.broadcasted_iota(jnp.int32, x.shape, 1)
    quarter = HEAD_DIM // 4
    partner = jnp.where(lane % (2 * quarter) < quarter,
                        pltpu.roll(x, PAIR - quarter, 1), pltpu.roll(x, quarter, 1))
    return x * cos + partner * sin_signed


def _dup_head(x, j, first):
    sw = pltpu.roll(x, HEAD_DIM, 1)
    if j == 0:
        return jnp.where(first, x, sw)
    return jnp.where(first, sw, x)


def _fill_na_bias(tab_ref, bias_scr, qi, tq, first):
    rows = DEC_SEQ // GRID_W
    wh = min(NA_WIN_H, rows)
    neg = -jnp.inf
    for a in range(tq // GRID_W):
        qr = qi * (tq // GRID_W) + a
        rs = jnp.clip(qr - wh // 2, 0, rows - wh)
        for m in range(rows // 2):
            kr0, kr1 = 2 * m, 2 * m + 1
            ok0 = jnp.logical_and(kr0 >= rs, kr0 < rs + wh)
            ok1 = jnp.logical_and(kr1 >= rs, kr1 < rs + wh)
            d0 = jnp.clip(kr0 - qr + NA_WIN_H - 1, 0, 2 * NA_WIN_H - 2)
            d1 = jnp.clip(kr1 - qr + NA_WIN_H - 1, 0, 2 * NA_WIN_H - 2)
            for hh in range(2):
                t0 = jnp.where(ok0, tab_ref[hh, d0], neg)
                t1 = jnp.where(ok1, tab_ref[hh, d1], neg)
                bias_scr[hh * tq + a * GRID_W:hh * tq + (a + 1) * GRID_W, m * PAIR:(m + 1) * PAIR] = (
                    jnp.where(first, t0, t1))


def _attn_kernel(cfg, *refs):
    norm, rope, gqa, ctx, bias, emit_k = cfg
    it = iter(refs)
    q_ref, k_ref, v_ref = next(it), next(it), next(it)
    kc_ref = vc_ref = b_ref = gq_ref = gk_ref = cq_ref = sq_ref = ck_ref = sk_ref = kn_ref = None
    if ctx:
        kc_ref, vc_ref = next(it), next(it)
    if bias:
        b_ref = next(it)
    if norm:
        gq_ref, gk_ref = next(it), next(it)
    if rope:
        cq_ref, sq_ref, ck_ref, sk_ref = next(it), next(it), next(it), next(it)
    o_ref = next(it)
    if emit_k:
        kn_ref = next(it)
    bias_scr = next(it) if bias else None

    first = _head_masks()
    tq = q_ref.shape[0]
    n_half = q_ref.shape[1] // PAIR
    scale = HEAD_DIM ** -0.5

    k = k_ref[...]
    if norm:
        k = _pair_rms(k, gk_ref[...], first)
    if rope:
        k = _pair_rope(k, ck_ref[...], sk_ref[...])
    if emit_k:
        kn_ref[...] = k
    v = v_ref[...]
    kc = kc_ref[...] if ctx else None
    vc = vc_ref[...] if ctx else None

    if bias:
        @pl.when(pl.program_id(2) == 0)
        def _():
            _fill_na_bias(b_ref, bias_scr, pl.program_id(1), tq, first)

    for j in range(n_half):
        q = q_ref[:, j * PAIR:(j + 1) * PAIR]
        if norm:
            q = _pair_rms(q, gq_ref[...], first)
        if rope:
            q = _pair_rope(q, cq_ref[...], sq_ref[...])
        q = q * scale
        qs = jnp.concatenate([jnp.where(first, q, 0.0), jnp.where(first, 0.0, q)], axis=0).astype(BF16)
        if gqa:
            kj, vj = _dup_head(k, j, first), _dup_head(v, j, first)
            kcj = _dup_head(kc, j, first) if ctx else None
            vcj = _dup_head(vc, j, first) if ctx else None
        else:
            kj, vj, kcj, vcj = k, v, kc, vc
        s = _dot_nt(qs, kj.astype(BF16))
        if bias:
            s = s + bias_scr[...]
        m = jnp.max(s, axis=-1, keepdims=True)
        if ctx:
            sc = _dot_nt(qs, kcj.astype(BF16))
            m = jnp.maximum(m, jnp.max(sc, axis=-1, keepdims=True))
        p = jnp.exp(s - m)
        l = jnp.sum(p, axis=-1, keepdims=True)
        o = _dot(p.astype(BF16), vj.astype(BF16))
        if ctx:
            pc = jnp.exp(sc - m)
            l = l + jnp.sum(pc, axis=-1, keepdims=True)
            o = o + _dot(pc.astype(BF16), vcj.astype(BF16))
        o = o / l
        o_ref[:, j * PAIR:(j + 1) * PAIR] = jnp.where(first, o[:tq], o[tq:])


def _attention(q, k, v, *, lat, gqa, layer_i=0, cache_k=None, cache_v=None, bias=None,
               gq=None, gk=None, rope_tabs=None):
    norm = gq is not None
    rope = rope_tabs is not None
    has_bias = bias is not None
    emit_k = norm and not lat
    n_half = 2 if gqa else 1
    n_pairs = k.shape[1] // PAIR
    if lat:
        nb, t, tq = DEC_BATCH, DEC_SEQ, TQ_LAT
        row0 = N_CTX
    else:
        nb, t, tq = BATCH, SEQ, SEQ
        row0 = 0
    nq = t // tq
    qoff, koff = row0 // tq, row0 // t
    grid = (n_pairs, nq, nb)
    in_specs = [
        pl.BlockSpec((tq, n_half * PAIR), lambda p, qi, b: (qoff + b * nq + qi, p)),
        pl.BlockSpec((t, PAIR), lambda p, qi, b: (koff + b, p)),
        pl.BlockSpec((t, PAIR), lambda p, qi, b: (koff + b, p)),
    ]
    args = [q, k, v]
    if lat:
        in_specs += [pl.BlockSpec((None, None, PAST_LEN, PAIR), lambda p, qi, b: (b, layer_i, 0, p))] * 2
        args += [cache_k, cache_v]
    if has_bias:
        in_specs.append(pl.BlockSpec((2,) + bias.shape[1:], lambda p, qi, b: (p, 0, 0, 0)))
        args.append(bias)
    if norm:
        in_specs += [pl.BlockSpec((1, PAIR), lambda p, qi, b: (0, 0))] * 2
        args += [gq, gk]
    if rope:
        cos, sin = rope_tabs
        in_specs += [pl.BlockSpec((tq, PAIR), lambda p, qi, b: (qi, 0))] * 2
        in_specs += [pl.BlockSpec((t, PAIR), lambda p, qi, b: (0, 0))] * 2
        args += [cos, sin, cos, sin]
    n_rows = nb * t
    out_specs = [pl.BlockSpec((tq, n_half * PAIR), lambda p, qi, b: (b * nq + qi, p))]
    out_shape = [jax.ShapeDtypeStruct((n_rows, q.shape[1]), F32)]
    if emit_k:
        out_specs.append(pl.BlockSpec((t, PAIR), lambda p, qi, b: (b, p)))
        out_shape.append(jax.ShapeDtypeStruct((n_rows, k.shape[1]), F32))
    cfg = (norm, rope, gqa, lat, has_bias, emit_k)
    return pl.pallas_call(
        functools.partial(_attn_kernel, cfg),
        grid=grid,
        in_specs=in_specs,
        out_specs=out_specs,
        out_shape=out_shape,
        scratch_shapes=[pltpu.VMEM((2 * tq, t), F32)] if has_bias else [],
        compiler_params=_cparams(("parallel", "parallel", "arbitrary")),
        name="attn_" + ("lat" if lat else "ctx") + ("_a" if gqa else "_b"),
    )(*args)


def _rope_tables():
    half = HEAD_DIM // 2
    freqs = ROPE_THETA ** (-np.arange(0, half, 2, dtype=np.float32) / half)
    t = np.arange(DEC_SEQ)

    def tab(pos):
        ang = pos.astype(np.float32)[:, None] * freqs[None, :]
        cos = np.concatenate([np.cos(ang), np.cos(ang)], -1)
        sin = np.concatenate([-np.sin(ang), np.sin(ang)], -1)
        return cos, sin

    cr, sr = tab(t // GRID_W)
    cc, sc = tab(t % GRID_W)
    cos = np.concatenate([cr, cc] * 2, -1).astype(np.float32)
    sin = np.concatenate([sr, sc] * 2, -1).astype(np.float32)
    return jnp.asarray(cos), jnp.asarray(sin)


def _na_bias_table(rel_bias):
    w = GRID_W
    n_rel = 2 * NA_WIN_W - 1
    left = w - NA_WIN_W
    rp = jnp.pad(rel_bias, ((0, 0), (0, 0), (left, 2 * w - n_rel - left)))
    flat = jnp.tile(rp, (1, 1, w))
    m = flat[..., :w * (2 * w - 1)].reshape(rel_bias.shape[0], rel_bias.shape[1], w, 2 * w - 1)[..., w - 1:]
    c = np.arange(w)
    ws = np.clip(c - NA_WIN_W // 2, 0, w - NA_WIN_W)
    in_col = (c[None, :] >= ws[:, None]) & (c[None, :] < ws[:, None] + NA_WIN_W)
    m = jnp.where(jnp.asarray(in_col)[None, None], m, -jnp.inf)
    return jnp.concatenate([m, m], axis=-1)


def _chunk_scan(x, reverse):
    t = x.shape[0]
    pos = lax.broadcasted_iota(jnp.int32, x.shape, 0) % GLA_CHUNK
    s = 1
    while s < GLA_CHUNK:
        if reverse:
            x = x + jnp.where(pos < GLA_CHUNK - s, pltpu.roll(x, t - s, 0), 0.0)
        else:
            x = x + jnp.where(pos >= s, pltpu.roll(x, s, 0), 0.0)
        s *= 2
    return x


def _gla_kernel(has_s0, emit_s, *refs):
    it = iter(refs)
    q_ref, k_ref, v_ref, lgf_ref, lgb_ref = (next(it) for _ in range(5))
    s0_ref = next(it) if has_s0 else None
    o_ref = next(it)
    sf_ref = next(it) if emit_s else None
    qf_scr, qb_scr, kin_scr, kout_scr, bf_scr, bb_scr, u_scr, st_scr = (next(it) for _ in range(8))

    c_len, dk = GLA_CHUNK, GLA_DK
    t = q_ref.shape[0]
    n = t // c_len
    scale = dk ** -0.5

    lgf, lgb = lgf_ref[...], lgb_ref[...]
    bf = _chunk_scan(lgf, False)
    bb = _chunk_scan(lgb, True)
    q = q_ref[...] * scale
    k = k_ref[...]
    zero = jnp.zeros((t, dk), BF16)
    qf_scr[:, :dk] = (q * jnp.exp(bf)).astype(BF16)
    qf_scr[:, dk:] = zero
    qb_scr[:, :dk] = zero
    qb_scr[:, dk:] = (q * jnp.exp(bb)).astype(BF16)
    kin_scr[:, :dk] = (k * jnp.exp(-bf)).astype(BF16)
    kin_scr[:, dk:] = (k * jnp.exp(-bb)).astype(BF16)
    bf3, bb3 = bf.reshape(n, c_len, dk), bb.reshape(n, c_len, dk)
    kout_scr[:, :dk] = (k * jnp.exp((bf3[:, c_len - 1:, :] - bf3).reshape(t, dk))).astype(BF16)
    kout_scr[:, dk:] = (k * jnp.exp((bb3[:, :1, :] - bb3).reshape(t, dk))).astype(BF16)
    bf_scr[...] = bf
    bb_scr[...] = bb

    row = lax.broadcasted_iota(jnp.int32, (c_len, c_len), 0)
    col = lax.broadcasted_iota(jnp.int32, (c_len, c_len), 1)

    def rows(c):
        return pl.ds(pl.multiple_of(c * c_len, c_len), c_len)

    def intra(c, _):
        r = rows(c)
        v = v_ref[r, :].astype(BF16)
        a2 = _dot_nt(jnp.concatenate([qf_scr[r, :], qb_scr[r, :]], axis=0), kin_scr[r, :])
        a = jnp.where(col <= row, a2[:c_len], 0.0) + jnp.where(col >= row, a2[c_len:], 0.0)
        o_ref[r, :] = _dot(a.astype(BF16), v)
        u = _dot_tn(v, kout_scr[r, :])
        u_scr[c, :, :dk] = u[:, :dk]
        u_scr[n - 1 - c, :, dk:] = u[:, dk:]
        return _

    lax.fori_loop(0, n, intra, None, unroll=GLA_UNROLL)

    if has_s0:
        st_scr[:, :dk] = s0_ref[0].T
        st_scr[:, dk:] = s0_ref[1].T
    else:
        st_scr[...] = jnp.zeros_like(st_scr)

    def inter(i, _):
        rf, rb = rows(i), rows(n - 1 - i)
        st = st_scr[...]
        res = _dot_nt(jnp.concatenate([qf_scr[rf, :], qb_scr[rb, :]], axis=0), st.astype(BF16))
        o_ref[rf, :] += res[:c_len]
        o_ref[rb, :] += res[c_len:]
        b_edge = jnp.concatenate([bf_scr[pl.ds(i * c_len + c_len - 1, 1), :],
                                  bb_scr[pl.ds((n - 1 - i) * c_len, 1), :]], axis=1)
        st_scr[...] = st * jnp.exp(b_edge) + u_scr[i]
        return _

    lax.fori_loop(0, n, inter, None, unroll=GLA_UNROLL)

    if emit_s:
        sf_ref[0] = st_scr[:, :dk].T
        sf_ref[1] = st_scr[:, dk:].T


def _gla(q, k, v, lgf, lgb, *, lat, layer_j=0, state=None):
    if lat:
        nb, t, row0 = DEC_BATCH, DEC_SEQ, N_CTX
    else:
        nb, t, row0 = BATCH, SEQ, 0
    off = row0 // t
    kspec = pl.BlockSpec((t, GLA_DK), lambda b, h: (off + b, h))
    in_specs = [kspec, kspec, pl.BlockSpec((t, GLA_DV), lambda b, h: (off + b, h)), kspec, kspec]
    args = [q, k, v, lgf, lgb]
    if lat:
        in_specs.append(pl.BlockSpec((None, None, 2, None, GLA_DK, GLA_DV),
                                     lambda b, h: (b, layer_j, 0, h, 0, 0)))
        args.append(state)
    out_specs = [pl.BlockSpec((t, GLA_DV), lambda b, h: (b, h))]
    out_shape = [jax.ShapeDtypeStruct((nb * t, GLA_VDIM), F32)]
    if not lat:
        out_specs.append(pl.BlockSpec((None, 2, None, GLA_DK, GLA_DV), lambda b, h: (b, 0, h, 0, 0)))
        out_shape.append(jax.ShapeDtypeStruct((nb, 2, GLA_HEADS, GLA_DK, GLA_DV), F32))
    return pl.pallas_call(
        functools.partial(_gla_kernel, lat, not lat),
        grid=(nb, GLA_HEADS),
        in_specs=in_specs,
        out_specs=out_specs,
        out_shape=out_shape,
        scratch_shapes=[
            pltpu.VMEM((t, 2 * GLA_DK), BF16),
            pltpu.VMEM((t, 2 * GLA_DK), BF16),
            pltpu.VMEM((t, 2 * GLA_DK), BF16),
            pltpu.VMEM((t, 2 * GLA_DK), BF16),
            pltpu.VMEM((t, GLA_DK), F32),
            pltpu.VMEM((t, GLA_DK), F32),
            pltpu.VMEM((t // GLA_CHUNK, GLA_DV, 2 * GLA_DK), F32),
            pltpu.VMEM((GLA_DV, 2 * GLA_DK), F32),
        ],
        compiler_params=_cparams(("parallel", "parallel")),
        name="gla_lat" if lat else "gla_ctx",
    )(*args)


def _group_pick(tm, c_ref, l_ref):
    return jnp.where(pl.program_id(0) < N_CTX // tm, c_ref[...], l_ref[...])


def _attn_out_kernel(tm, x_ref, mod_ref, oac_ref, oal_ref, obc_ref, obl_ref, w_ref, o_ref):
    oa = _group_pick(tm, oac_ref, oal_ref).astype(BF16)
    ob = _group_pick(tm, obc_ref, obl_ref).astype(BF16)
    na = oa.shape[1]
    y = _dot(oa, w_ref[:na, :]) + _dot(ob, w_ref[na:, :])
    o_ref[...] = x_ref[...] + mod_ref[2:3, :] * y


def _gla_out_kernel(tm, x_ref, mod_ref, oc_ref, ol_ref, gate_ref, gn_ref, w_ref, o_ref):
    gn = gn_ref[...]
    o_pre = _group_pick(tm, oc_ref, ol_ref)
    parts = []
    for h in range(GLA_HEADS):
        sl = slice(h * GLA_DV, (h + 1) * GLA_DV)
        gt = gate_ref[:, sl]
        parts.append((_rms(o_pre[:, sl], gn) * (gt * jax.nn.sigmoid(gt))).astype(BF16))
    y = _dot(jnp.concatenate(parts, axis=-1), w_ref[...])
    o_ref[...] = x_ref[...] + mod_ref[2:3, :] * y


def _mixer_out(kernel, name, x, mod, layer, group_acts, acts, consts):
    tm = TM_PROJ
    nct = N_CTX // tm
    in_specs = [
        pl.BlockSpec((tm, D_MODEL), lambda i: (i, 0)),
        pl.BlockSpec((None, None, None, 3, D_MODEL), lambda i: (layer, _cond_row(i, tm), 1, 0, 0)),
    ]
    flat = []
    for a_c, a_l in group_acts:
        in_specs.append(pl.BlockSpec((tm, a_c.shape[1]), lambda i: (jnp.minimum(i, nct - 1), 0)))
        in_specs.append(pl.BlockSpec((tm, a_l.shape[1]), lambda i: (jnp.maximum(i - nct, 0), 0)))
        flat += [a_c, a_l]
    in_specs += [pl.BlockSpec((tm, a.shape[1]), lambda i: (i, 0)) for a in acts]
    in_specs += [pl.BlockSpec(c.shape, lambda i: (0, 0)) for c in consts]
    return pl.pallas_call(
        functools.partial(kernel, tm),
        grid=(N_TOK // tm,),
        in_specs=in_specs,
        out_specs=pl.BlockSpec((tm, D_MODEL), lambda i: (i, 0)),
        out_shape=jax.ShapeDtypeStruct((N_TOK, D_MODEL), F32),
        compiler_params=_cparams(("parallel",)),
        name=name,
    )(x, mod, *flat, *acts, *consts)


def _final_kernel(x_ref, g_ref, o_ref):
    o_ref[...] = _rms(x_ref[...], g_ref[...])


def _final_norm(x, g, row0, n_rows):
    tm = TM_PROJ
    off = row0 // tm
    return pl.pallas_call(
        _final_kernel,
        grid=(n_rows // tm,),
        in_specs=[pl.BlockSpec((tm, D_MODEL), lambda i: (off + i, 0)),
                  pl.BlockSpec((1, D_MODEL), lambda i: (0, 0))],
        out_specs=pl.BlockSpec((tm, D_MODEL), lambda i: (i, 0)),
        out_shape=jax.ShapeDtypeStruct((n_rows, D_MODEL), F32),
        compiler_params=_cparams(("parallel",)),
        name="final_norm",
    )(x, g.reshape(1, D_MODEL))


def kernel(x_prompt, x_sample, cache_a_k, cache_a_v, cache_b_k, cache_b_v, state_gla, c, c_ctx,
           w_mod, b_mod, g_norm, w_ffn_gate, w_ffn_up, w_ffn_down, w_attn_in, w_attn_out,
           g_qnorm, g_knorm, na_rel_bias, w_gla_in, w_gla_gup, b_gla_gup, g_gla_norm, w_gla_out, g_final):
    x = jnp.concatenate([x_prompt.reshape(N_CTX, D_MODEL), x_sample.reshape(N_LAT, D_MODEL)], axis=0)
    cond = jnp.concatenate([c_ctx[None, :], c, jnp.zeros((COND_ROWS - 1 - DEC_BATCH, D_MODEL), F32)], axis=0)
    mod = _modulation(cond, w_mod, b_mod)

    wg_all, wu_all, wd_all = _ffn_weights(w_ffn_gate, w_ffn_up, w_ffn_down)
    rope_tabs = _rope_tables()
    n_attn = w_attn_in.shape[0]
    ck_a = cache_a_k.reshape(DEC_BATCH, n_attn, PAST_LEN, A_KV_HEADS * HEAD_DIM)
    cv_a = cache_a_v.reshape(DEC_BATCH, n_attn, PAST_LEN, A_KV_HEADS * HEAD_DIM)
    ck_b = cache_b_k.reshape(DEC_BATCH, n_attn, PAST_LEN, B_HEADS * HEAD_DIM)
    cv_b = cache_b_v.reshape(DEC_BATCH, n_attn, PAST_LEN, B_HEADS * HEAD_DIM)

    new_a_k, new_a_v, new_b_k, new_b_v, new_gla = [], [], [], [], []
    for l in range(DEPTH):
        x = _ffn(x, mod, l, 0, g_norm[l, 0], wg_all[l, 0], wu_all[l, 0], wd_all[l, 0])
        if l % 2 == 0:
            i = l // 2
            qa, ka, va, qb, kb, vb = _attn_in(x, mod, l, g_norm[l, 1], w_attn_in[i].astype(BF16))
            gq = jnp.tile(g_qnorm[i], 2).reshape(1, PAIR)
            gk = jnp.tile(g_knorm[i], 2).reshape(1, PAIR)
            oa_c, ka_n = _attention(qa, ka, va, lat=False, gqa=True, gq=gq, gk=gk)
            (ob_c,) = _attention(qb, kb, vb, lat=False, gqa=False)
            (oa_l,) = _attention(qa, ka, va, lat=True, gqa=True, layer_i=i, cache_k=ck_a, cache_v=cv_a,
                                 gq=gq, gk=gk, rope_tabs=rope_tabs)
            (ob_l,) = _attention(qb, kb, vb, lat=True, gqa=False, layer_i=i, cache_k=ck_b, cache_v=cv_b,
                                 bias=_na_bias_table(na_rel_bias[i]))
            x = _mixer_out(_attn_out_kernel, "attn_out", x, mod, l, [(oa_c, oa_l), (ob_c, ob_l)], [],
                           [w_attn_out[i].astype(BF16)])
            new_a_k.append(ka_n.reshape(BATCH, SEQ, A_KV_HEADS, HEAD_DIM))
            new_a_v.append(va[:N_CTX].reshape(BATCH, SEQ, A_KV_HEADS, HEAD_DIM))
            new_b_k.append(kb[:N_CTX].reshape(BATCH, SEQ, B_HEADS, HEAD_DIM))
            new_b_v.append(vb[:N_CTX].reshape(BATCH, SEQ, B_HEADS, HEAD_DIM))
        else:
            j = l // 2
            w_in = jnp.pad(w_gla_in[j], ((0, 0), (0, GLA_D_PAD - 2 * GLA_RANK))).astype(BF16)
            w_gup = jnp.zeros((GLA_D_PAD, 2 * GLA_KDIM), F32)
            w_gup = w_gup.at[:GLA_RANK, :GLA_KDIM].set(w_gla_gup[j, 0])
            w_gup = w_gup.at[GLA_RANK:2 * GLA_RANK, GLA_KDIM:].set(w_gla_gup[j, 1]).astype(BF16)
            b_gup = b_gla_gup[j].reshape(1, 2 * GLA_KDIM)
            q, k, v, gt, lgf, lgb = _gla_in(x, mod, l, g_norm[l, 1], w_in, w_gup, b_gup)
            o_c, s_fin = _gla(q, k, v, lgf, lgb, lat=False)
            (o_l,) = _gla(q, k, v, lgf, lgb, lat=True, layer_j=j, state=state_gla)
            x = _mixer_out(_gla_out_kernel, "gla_out", x, mod, l, [(o_c, o_l)], [gt],
                           [g_gla_norm[j].reshape(1, GLA_DV), w_gla_out[j].astype(BF16)])
            new_gla.append(s_fin)
        x = _ffn(x, mod, l, 2, g_norm[l, 2], wg_all[l, 1], wu_all[l, 1], wd_all[l, 1])

    y_ctx = _final_norm(x, g_final, 0, N_CTX)
    y_lat = _final_norm(x, g_final, N_CTX, N_LAT)
    return (y_ctx.reshape(BATCH, SEQ, D_MODEL), y_lat.reshape(DEC_BATCH, DEC_SEQ, D_MODEL),
            jnp.stack(new_a_k, axis=1), jnp.stack(new_a_v, axis=1),
            jnp.stack(new_b_k, axis=1), jnp.stack(new_b_v, axis=1), jnp.stack(new_gla, axis=1))
```

```python
import functools

import numpy as np
import jax
import jax.numpy as jnp
from jax import lax
from jax.experimental import pallas as pl
from jax.experimental.pallas import tpu as pltpu

D_MODEL = 1024
BATCH = 32
SEQ = 256
DEPTH = 4
DEC_BATCH = 8
DEC_SEQ = 1024
PAST_LEN = 512
GRID_W = 64
HEAD_DIM = 64
A_HEADS = 8
A_KV_HEADS = 4
B_HEADS = 8
ROPE_THETA = 10000.0
NA_WIN_H = 8
NA_WIN_W = 16
GLA_HEADS = 4
GLA_DK = D_MODEL // 2 // GLA_HEADS
GLA_DV = D_MODEL // GLA_HEADS
GLA_KDIM = GLA_HEADS * GLA_DK
GLA_VDIM = GLA_HEADS * GLA_DV
GLA_RANK = 16
GLA_GATE_NORM = 16.0
GLA_CHUNK = 64
D_FF = ((8 * D_MODEL // 3 + 127) // 128) * 128
FFN_RES = 0.5
N_MOD = 9
EPS = 1e-6

N_CTX = BATCH * SEQ
N_LAT = DEC_BATCH * DEC_SEQ
N_TOK = N_CTX + N_LAT
COND_ROWS = 16
LANES = 128
PAIR = 2 * HEAD_DIM
GLA_D_PAD = LANES
VMEM_LIMIT = 56 * 1024 * 1024

F32 = jnp.float32
BF16 = jnp.bfloat16

TM_FFN = 1024
TF_FFN = 256
TM_PROJ = 512
TQ_LAT_A = 128
TQ_LAT_B = 256
ATTN_SUB_ROWS = 256
GLA_UNROLL = 8


def _cparams(sem):
    return pltpu.CompilerParams(dimension_semantics=sem, vmem_limit_bytes=VMEM_LIMIT)


def _cond_row(i, tm):
    nct = N_CTX // tm
    per_batch = DEC_SEQ // tm
    return jnp.where(i < nct, 0, 1 + (i - nct) // per_batch)


def _rms(x, g):
    ms = jnp.mean(x * x, axis=-1, keepdims=True)
    return x * lax.rsqrt(ms + EPS) * g


def _dot(a, b):
    return jnp.dot(a, b, preferred_element_type=F32)


def _dot_nt(a, b):
    return lax.dot_general(a, b, (((1,), (1,)), ((), ())), preferred_element_type=F32)


def _dot_tn(a, b):
    return lax.dot_general(a, b, (((0,), (0,)), ((), ())), preferred_element_type=F32)


def _mod_kernel(c_ref, w_ref, b_ref, o_ref):
    c = c_ref[...]
    s = c * jax.nn.sigmoid(c)
    o_ref[...] = jnp.dot(s, w_ref[...], precision=lax.Precision.HIGHEST,
                         preferred_element_type=F32) + b_ref[...]


def _modulation(cond, w_mod, b_mod):
    tn = D_MODEL
    n_mod = N_MOD * D_MODEL
    out = pl.pallas_call(
        _mod_kernel,
        grid=(DEPTH, n_mod // tn),
        in_specs=[
            pl.BlockSpec((COND_ROWS, D_MODEL), lambda l, j: (0, 0)),
            pl.BlockSpec((None, D_MODEL, tn), lambda l, j: (l, 0, j)),
            pl.BlockSpec((None, 1, tn), lambda l, j: (l, 0, j)),
        ],
        out_specs=pl.BlockSpec((None, COND_ROWS, tn), lambda l, j: (l, 0, j)),
        out_shape=jax.ShapeDtypeStruct((DEPTH, COND_ROWS, n_mod), F32),
        compiler_params=_cparams(("parallel", "parallel")),
        name="adaln_mod",
    )(cond, w_mod, b_mod.reshape(DEPTH, 1, n_mod))
    return out.reshape(DEPTH, COND_ROWS, 3, 3, D_MODEL)


def _ffn_kernel(x_ref, mod_ref, g_ref, wg_ref, wu_ref, wd_ref, o_ref, h_ref, acc_ref):
    y = _rms(x_ref[...], g_ref[...])
    h_ref[...] = (y * (1.0 + mod_ref[1:2, :]) + mod_ref[0:1, :]).astype(BF16)
    acc_ref[...] = jnp.zeros_like(acc_ref)

    def ff_chunk(j, carry):
        cols = pl.ds(pl.multiple_of(j * TF_FFN, TF_FFN), TF_FFN)
        h = h_ref[...]
        g = _dot(h, wg_ref[:, cols])
        u = _dot(h, wu_ref[:, cols])
        a = (g * jax.nn.sigmoid(g) * u).astype(BF16)
        acc_ref[...] += _dot(a, wd_ref[cols, :])
        return carry

    lax.fori_loop(0, D_FF // TF_FFN, ff_chunk, None)
    o_ref[...] = x_ref[...] + FFN_RES * mod_ref[2:3, :] * acc_ref[...]


def _ffn(x, mod, layer, sub, g, wg, wu, wd):
    tm = TM_FFN
    resident = lambda shape: pl.BlockSpec(shape, lambda i: (0, 0), pipeline_mode=pl.Buffered(1))
    return pl.pallas_call(
        _ffn_kernel,
        grid=(N_TOK // tm,),
        in_specs=[
            pl.BlockSpec((tm, D_MODEL), lambda i: (i, 0)),
            pl.BlockSpec((None, None, None, 3, D_MODEL), lambda i: (layer, _cond_row(i, tm), sub, 0, 0)),
            pl.BlockSpec((1, D_MODEL), lambda i: (0, 0)),
            resident(wg.shape), resident(wu.shape), resident(wd.shape),
        ],
        out_specs=pl.BlockSpec((tm, D_MODEL), lambda i: (i, 0)),
        out_shape=jax.ShapeDtypeStruct((N_TOK, D_MODEL), F32),
        scratch_shapes=[pltpu.VMEM((tm, D_MODEL), BF16), pltpu.VMEM((tm, D_MODEL), F32)],
        compiler_params=_cparams(("parallel",)),
        name="ffn",
    )(x, mod, g.reshape(1, D_MODEL), wg, wu, wd)


def _mixer_h(x_ref, mod_ref, g_ref):
    y = _rms(x_ref[...], g_ref[...])
    return (y * (1.0 + mod_ref[1:2, :]) + mod_ref[0:1, :]).astype(BF16)


ATTN_WIDTHS = (A_HEADS * HEAD_DIM, A_KV_HEADS * HEAD_DIM, A_KV_HEADS * HEAD_DIM,
               B_HEADS * HEAD_DIM, B_HEADS * HEAD_DIM, B_HEADS * HEAD_DIM)


def _attn_in_kernel(lat, x_ref, mod_ref, g_ref, w_ref, gq_ref, gk_ref, *refs):
    if lat:
        cos_ref, sin_ref, *o_refs = refs
    else:
        o_refs = refs
    h = _mixer_h(x_ref, mod_ref, g_ref)
    first = _head_masks()
    scale = HEAD_DIM ** -0.5
    offs = np.cumsum((0,) + ATTN_WIDTHS)

    def seg(i):
        return _dot(h, w_ref[:, offs[i]:offs[i + 1]])

    def normed_pairs(y, g2):
        for p in range(y.shape[1] // PAIR):
            z = _pair_rms(y[:, p * PAIR:(p + 1) * PAIR], g2, first)
            if lat:
                z = _pair_rope(z, cos_ref[...], sin_ref[...])
            yield slice(p * PAIR, (p + 1) * PAIR), z

    for sl, z in normed_pairs(seg(0), gq_ref[...]):
        o_refs[0][:, sl] = (z * scale).astype(BF16)
    for sl, z in normed_pairs(seg(1), gk_ref[...]):
        o_refs[1][:, sl] = z.astype(BF16)
        if not lat:
            o_refs[6][:, sl] = z
    for i in (2, 3, 4, 5):
        y = seg(i)
        o_refs[i][...] = ((y * scale) if i == 3 else y).astype(BF16)
        if not lat and i != 3:
            o_refs[{2: 7, 4: 8, 5: 9}[i]][...] = y


def _gla_in_kernel(widths, x_ref, mod_ref, g_ref, w_ref, wgup_ref, bgup_ref, *o_refs):
    h = _mixer_h(x_ref, mod_ref, g_ref)
    off = 0
    for o_ref, wd in zip(o_refs[:4], widths):
        o_ref[...] = _dot(h, w_ref[:, off:off + wd])
        off += wd
    d = _dot(h, w_ref[:, off:off + GLA_D_PAD]).astype(BF16)
    pre = _dot(d, wgup_ref[...]) + bgup_ref[...]
    lg = (jnp.minimum(pre, 0.0) - jnp.log1p(jnp.exp(-jnp.abs(pre)))) * (1.0 / GLA_GATE_NORM)
    o_refs[4][...] = lg[:, :GLA_KDIM]
    o_refs[5][...] = lg[:, GLA_KDIM:]


def _mixer_in_specs(layer, tm, n_in, tile0=0):
    return [
        pl.BlockSpec((tm, D_MODEL), lambda i: (tile0 + i, 0)),
        pl.BlockSpec((None, None, None, 3, D_MODEL), lambda i: (layer, _cond_row(tile0 + i, tm), 1, 0, 0)),
        pl.BlockSpec((1, D_MODEL), lambda i: (0, 0)),
        pl.BlockSpec((D_MODEL, n_in), lambda i: (0, 0)),
    ]


def _attn_in(x, mod, layer, g, w_in, gq, gk, *, lat, rope_tabs=None):
    tm = TM_PROJ
    n_rows, tile0 = (N_LAT, N_CTX // tm) if lat else (N_CTX, 0)
    in_specs = _mixer_in_specs(layer, tm, sum(ATTN_WIDTHS), tile0)
    in_specs += [pl.BlockSpec((1, PAIR), lambda i: (0, 0))] * 2
    args = [x, mod, g.reshape(1, D_MODEL), w_in, gq, gk]
    out_w = [(wd, BF16) for wd in ATTN_WIDTHS]
    if lat:
        per_seq = DEC_SEQ // tm
        in_specs += [pl.BlockSpec((tm, PAIR), lambda i: (i % per_seq, 0))] * 2
        args += list(rope_tabs)
    else:
        out_w += [(ATTN_WIDTHS[i], F32) for i in (1, 2, 4, 5)]
    return pl.pallas_call(
        functools.partial(_attn_in_kernel, lat),
        grid=(n_rows // tm,),
        in_specs=in_specs,
        out_specs=[pl.BlockSpec((tm, wd), lambda i: (i, 0)) for wd, _ in out_w],
        out_shape=[jax.ShapeDtypeStruct((n_rows, wd), dt) for wd, dt in out_w],
        compiler_params=_cparams(("parallel",)),
        name="attn_in_lat" if lat else "attn_in_ctx",
    )(*args)


def _gla_in(x, mod, layer, g, w_in, w_gup, b_gup):
    tm = TM_PROJ
    widths = (GLA_KDIM, GLA_KDIM, GLA_VDIM, GLA_VDIM)
    out_w = widths + (GLA_KDIM, GLA_KDIM)
    n_in = sum(widths) + GLA_D_PAD
    return pl.pallas_call(
        functools.partial(_gla_in_kernel, widths),
        grid=(N_TOK // tm,),
        in_specs=_mixer_in_specs(layer, tm, n_in) + [
            pl.BlockSpec((GLA_D_PAD, 2 * GLA_KDIM), lambda i: (0, 0)),
            pl.BlockSpec((1, 2 * GLA_KDIM), lambda i: (0, 0)),
        ],
        out_specs=[pl.BlockSpec((tm, wd), lambda i: (i, 0)) for wd in out_w],
        out_shape=[jax.ShapeDtypeStruct((N_TOK, wd), F32) for wd in out_w],
        compiler_params=_cparams(("parallel",)),
        name="gla_in",
    )(x, mod, g.reshape(1, D_MODEL), w_in, w_gup, b_gup)


def _head_masks():
    lane = lax.broadcasted_iota(jnp.int32, (1, PAIR), 1)
    return lane < HEAD_DIM


def _pair_rms(x, g2, first):
    sq = x * x
    s0 = jnp.sum(jnp.where(first, sq, 0.0), axis=-1, keepdims=True)
    s1 = jnp.sum(jnp.where(first, 0.0, sq), axis=-1, keepdims=True)
    ms = jnp.where(first, s0, s1) * (1.0 / HEAD_DIM)
    return x * lax.rsqrt(ms + EPS) * g2


def _pair_rope(x, cos, sin_signed):
    lane = lax.broadcasted_iota(jnp.int32, x.shape, 1)
    quarter = HEAD_DIM // 4
    partner = jnp.where(lane % (2 * quarter) < quarter,
                        pltpu.roll(x, PAIR - quarter, 1), pltpu.roll(x, quarter, 1))
    return x * cos + partner * sin_signed


def _fill_na_bias(tab_ref, bias_scr, qi, tq, first):
    rows = DEC_SEQ // GRID_W
    wh = min(NA_WIN_H, rows)
    neg = -jnp.inf
    for a in range(tq // GRID_W):
        qr = qi * (tq // GRID_W) + a
        rs = jnp.clip(qr - wh // 2, 0, rows - wh)
        for m in range(rows // 2):
            kr0, kr1 = 2 * m, 2 * m + 1
            ok0 = jnp.logical_and(kr0 >= rs, kr0 < rs + wh)
            ok1 = jnp.logical_and(kr1 >= rs, kr1 < rs + wh)
            d0 = jnp.clip(kr0 - qr + NA_WIN_H - 1, 0, 2 * NA_WIN_H - 2)
            d1 = jnp.clip(kr1 - qr + NA_WIN_H - 1, 0, 2 * NA_WIN_H - 2)
            for hh in range(2):
                t0 = jnp.where(ok0, tab_ref[hh, d0], neg)
                t1 = jnp.where(ok1, tab_ref[hh, d1], neg)
                bias_scr[hh * tq + a * GRID_W:hh * tq + (a + 1) * GRID_W, m * PAIR:(m + 1) * PAIR] = (
                    jnp.where(first, t0, t1))


def _attn_kernel(has_ctx, has_bias, *refs):
    it = iter(refs)
    q_ref, k_ref, v_ref = next(it), next(it), next(it)
    kc_ref, vc_ref = (next(it), next(it)) if has_ctx else (None, None)
    tab_ref = next(it) if has_bias else None
    o_ref = next(it)
    bias_scr = next(it) if has_bias else None

    first = _head_masks()
    tq = q_ref.shape[0]
    n_half = q_ref.shape[1] // PAIR

    if has_bias:
        @pl.when(pl.program_id(2) == 0)
        def _():
            _fill_na_bias(tab_ref, bias_scr, pl.program_id(1), tq, first)

    zero = jnp.zeros((tq, PAIR), BF16)
    stacked = []
    for j in range(n_half):
        qj = q_ref[:, j * PAIR:(j + 1) * PAIR]
        stacked += [jnp.where(first, qj, zero), jnp.where(first, zero, qj)]
    qs = jnp.concatenate(stacked, axis=0)

    k, v = k_ref[...], v_ref[...]
    kc = kc_ref[...].astype(BF16) if has_ctx else None
    vc = vc_ref[...].astype(BF16) if has_ctx else None
    rows = qs.shape[0]
    sub = min(rows, ATTN_SUB_ROWS)

    def scores(i):
        qi = qs[i * sub:(i + 1) * sub]
        return _dot_nt(qi, k), (_dot_nt(qi, kc) if has_ctx else None)

    def finish(i, s, sc):
        if has_bias:
            s = s + bias_scr[i * sub:(i + 1) * sub, :]
        m = jnp.max(s, axis=-1, keepdims=True)
        if has_ctx:
            m = jnp.maximum(m, jnp.max(sc, axis=-1, keepdims=True))
        p = jnp.exp(s - m)
        l = jnp.sum(p, axis=-1, keepdims=True)
        o = _dot(p.astype(BF16), v)
        if has_ctx:
            pc = jnp.exp(sc - m)
            l = l + jnp.sum(pc, axis=-1, keepdims=True)
            o = o + _dot(pc.astype(BF16), vc)
        return o / l

    outs = []
    cur = scores(0)
    for i in range(rows // sub):
        nxt = scores(i + 1) if (i + 1) * sub < rows else None
        outs.append(finish(i, *cur))
        cur = nxt
    o = jnp.concatenate(outs, axis=0) if len(outs) > 1 else outs[0]
    for j in range(n_half):
        o_ref[:, j * PAIR:(j + 1) * PAIR] = jnp.where(
            first, o[2 * j * tq:(2 * j + 1) * tq], o[(2 * j + 1) * tq:(2 * j + 2) * tq]).astype(BF16)


def _attention(q, k, v, *, lat, tq, layer_i=0, cache_k=None, cache_v=None, bias_tab=None):
    has_bias = bias_tab is not None
    n_pairs = k.shape[1] // PAIR
    n_half = q.shape[1] // k.shape[1]
    nb, t = (DEC_BATCH, DEC_SEQ) if lat else (BATCH, SEQ)
    nq = t // tq
    in_specs = [
        pl.BlockSpec((tq, n_half * PAIR), lambda p, qi, b: (b * nq + qi, p)),
        pl.BlockSpec((t, PAIR), lambda p, qi, b: (b, p)),
        pl.BlockSpec((t, PAIR), lambda p, qi, b: (b, p)),
    ]
    args = [q, k, v]
    if lat:
        in_specs += [pl.BlockSpec((None, None, PAST_LEN, PAIR), lambda p, qi, b: (b, layer_i, 0, p))] * 2
        args += [cache_k, cache_v]
    if has_bias:
        in_specs.append(pl.BlockSpec((2,) + bias_tab.shape[1:], lambda p, qi, b: (p, 0, 0, 0)))
        args.append(bias_tab)
    return pl.pallas_call(
        functools.partial(_attn_kernel, lat, has_bias),
        grid=(n_pairs, nq, nb),
        in_specs=in_specs,
        out_specs=pl.BlockSpec((tq, n_half * PAIR), lambda p, qi, b: (b * nq + qi, p)),
        out_shape=jax.ShapeDtypeStruct(q.shape, BF16),
        scratch_shapes=[pltpu.VMEM((2 * tq, t), F32)] if has_bias else [],
        compiler_params=_cparams(("parallel", "parallel", "arbitrary")),
        name="attn_" + ("lat" if lat else "ctx") + ("_a" if n_half == 2 else "_b"),
    )(*args)


def _rope_tables():
    half = HEAD_DIM // 2
    freqs = ROPE_THETA ** (-np.arange(0, half, 2, dtype=np.float32) / half)
    t = np.arange(DEC_SEQ)

    def tab(pos):
        ang = pos.astype(np.float32)[:, None] * freqs[None, :]
        cos = np.concatenate([np.cos(ang), np.cos(ang)], -1)
        sin = np.concatenate([-np.sin(ang), np.sin(ang)], -1)
        return cos, sin

    cr, sr = tab(t // GRID_W)
    cc, sc = tab(t % GRID_W)
    cos = np.concatenate([cr, cc] * 2, -1).astype(np.float32)
    sin = np.concatenate([sr, sc] * 2, -1).astype(np.float32)
    return jnp.asarray(cos), jnp.asarray(sin)


def _na_bias_table(rel_bias):
    w = GRID_W
    n_rel = 2 * NA_WIN_W - 1
    left = w - NA_WIN_W
    rp = jnp.pad(rel_bias, ((0, 0), (0, 0), (left, 2 * w - n_rel - left)))
    flat = jnp.tile(rp, (1, 1, w))
    m = flat[..., :w * (2 * w - 1)].reshape(rel_bias.shape[0], rel_bias.shape[1], w, 2 * w - 1)[..., w - 1:]
    c = np.arange(w)
    ws = np.clip(c - NA_WIN_W // 2, 0, w - NA_WIN_W)
    in_col = (c[None, :] >= ws[:, None]) & (c[None, :] < ws[:, None] + NA_WIN_W)
    m = jnp.where(jnp.asarray(in_col)[None, None], m, -jnp.inf)
    return jnp.concatenate([m, m], axis=-1)


def _chunk_scan(x, reverse):
    t = x.shape[0]
    pos = lax.broadcasted_iota(jnp.int32, x.shape, 0) % GLA_CHUNK
    s = 1
    while s < GLA_CHUNK:
        if reverse:
            x = x + jnp.where(pos < GLA_CHUNK - s, pltpu.roll(x, t - s, 0), 0.0)
        else:
            x = x + jnp.where(pos >= s, pltpu.roll(x, s, 0), 0.0)
        s *= 2
    return x


def _gla_kernel(has_s0, emit_s, *refs):
    it = iter(refs)
    q_ref, k_ref, v_ref, lgf_ref, lgb_ref = (next(it) for _ in range(5))
    s0_ref = next(it) if has_s0 else None
    o_ref = next(it)
    sf_ref = next(it) if emit_s else None
    qf_scr, qb_scr, kin_scr, kout_scr, bf_scr, bb_scr, u_scr, st_scr = (next(it) for _ in range(8))

    c_len, dk = GLA_CHUNK, GLA_DK
    t = q_ref.shape[0]
    n = t // c_len
    scale = dk ** -0.5

    lgf, lgb = lgf_ref[...], lgb_ref[...]
    bf = _chunk_scan(lgf, False)
    bb = _chunk_scan(lgb, True)
    q = q_ref[...] * scale
    k = k_ref[...]
    zero = jnp.zeros((t, dk), BF16)
    qf_scr[:, :dk] = (q * jnp.exp(bf)).astype(BF16)
    qf_scr[:, dk:] = zero
    qb_scr[:, :dk] = zero
    qb_scr[:, dk:] = (q * jnp.exp(bb)).astype(BF16)
    kin_scr[:, :dk] = (k * jnp.exp(-bf)).astype(BF16)
    kin_scr[:, dk:] = (k * jnp.exp(-bb)).astype(BF16)
    bf3, bb3 = bf.reshape(n, c_len, dk), bb.reshape(n, c_len, dk)
    kout_scr[:, :dk] = (k * jnp.exp((bf3[:, c_len - 1:, :] - bf3).reshape(t, dk))).astype(BF16)
    kout_scr[:, dk:] = (k * jnp.exp((bb3[:, :1, :] - bb3).reshape(t, dk))).astype(BF16)
    bf_scr[...] = bf
    bb_scr[...] = bb

    row = lax.broadcasted_iota(jnp.int32, (c_len, c_len), 0)
    col = lax.broadcasted_iota(jnp.int32, (c_len, c_len), 1)

    def rows(c):
        return pl.ds(pl.multiple_of(c * c_len, c_len), c_len)

    def intra(c, _):
        r = rows(c)
        v = v_ref[r, :].astype(BF16)
        a2 = _dot_nt(jnp.concatenate([qf_scr[r, :], qb_scr[r, :]], axis=0), kin_scr[r, :])
        a = jnp.where(col <= row, a2[:c_len], 0.0) + jnp.where(col >= row, a2[c_len:], 0.0)
        o_ref[r, :] = _dot(a.astype(BF16), v)
        u = _dot_tn(v, kout_scr[r, :])
        u_scr[c, :, :dk] = u[:, :dk]
        u_scr[n - 1 - c, :, dk:] = u[:, dk:]
        return _

    lax.fori_loop(0, n, intra, None, unroll=GLA_UNROLL)

    if has_s0:
        st_scr[:, :dk] = s0_ref[0].T
        st_scr[:, dk:] = s0_ref[1].T
    else:
        st_scr[...] = jnp.zeros_like(st_scr)

    def inter(i, _):
        rf, rb = rows(i), rows(n - 1 - i)
        st = st_scr[...]
        res = _dot_nt(jnp.concatenate([qf_scr[rf, :], qb_scr[rb, :]], axis=0), st.astype(BF16))
        o_ref[rf, :] += res[:c_len]
        o_ref[rb, :] += res[c_len:]
        b_edge = jnp.concatenate([bf_scr[pl.ds(i * c_len + c_len - 1, 1), :],
                                  bb_scr[pl.ds((n - 1 - i) * c_len, 1), :]], axis=1)
        st_scr[...] = st * jnp.exp(b_edge) + u_scr[i]
        return _

    lax.fori_loop(0, n, inter, None, unroll=GLA_UNROLL)

    if emit_s:
        sf_ref[0] = st_scr[:, :dk].T
        sf_ref[1] = st_scr[:, dk:].T


def _gla(q, k, v, lgf, lgb, *, lat, layer_j=0, state=None):
    if lat:
        nb, t, row0 = DEC_BATCH, DEC_SEQ, N_CTX
    else:
        nb, t, row0 = BATCH, SEQ, 0
    off = row0 // t
    kspec = pl.BlockSpec((t, GLA_DK), lambda b, h: (off + b, h))
    in_specs = [kspec, kspec, pl.BlockSpec((t, GLA_DV), lambda b, h: (off + b, h)), kspec, kspec]
    args = [q, k, v, lgf, lgb]
    if lat:
        in_specs.append(pl.BlockSpec((None, None, 2, None, GLA_DK, GLA_DV),
                                     lambda b, h: (b, layer_j, 0, h, 0, 0)))
        args.append(state)
    out_specs = [pl.BlockSpec((t, GLA_DV), lambda b, h: (b, h))]
    out_shape = [jax.ShapeDtypeStruct((nb * t, GLA_VDIM), F32)]
    if not lat:
        out_specs.append(pl.BlockSpec((None, 2, None, GLA_DK, GLA_DV), lambda b, h: (b, 0, h, 0, 0)))
        out_shape.append(jax.ShapeDtypeStruct((nb, 2, GLA_HEADS, GLA_DK, GLA_DV), F32))
    return pl.pallas_call(
        functools.partial(_gla_kernel, lat, not lat),
        grid=(nb, GLA_HEADS),
        in_specs=in_specs,
        out_specs=out_specs,
        out_shape=out_shape,
        scratch_shapes=[
            pltpu.VMEM((t, 2 * GLA_DK), BF16),
            pltpu.VMEM((t, 2 * GLA_DK), BF16),
            pltpu.VMEM((t, 2 * GLA_DK), BF16),
            pltpu.VMEM((t, 2 * GLA_DK), BF16),
            pltpu.VMEM((t, GLA_DK), F32),
            pltpu.VMEM((t, GLA_DK), F32),
            pltpu.VMEM((t // GLA_CHUNK, GLA_DV, 2 * GLA_DK), F32),
            pltpu.VMEM((GLA_DV, 2 * GLA_DK), F32),
        ],
        compiler_params=_cparams(("parallel", "parallel")),
        name="gla_lat" if lat else "gla_ctx",
    )(*args)


def _group_pick(tm, c_ref, l_ref):
    return jnp.where(pl.program_id(0) < N_CTX // tm, c_ref[...], l_ref[...])


def _attn_out_kernel(tm, x_ref, mod_ref, oac_ref, oal_ref, obc_ref, obl_ref, w_ref, o_ref):
    oa = _group_pick(tm, oac_ref, oal_ref)
    ob = _group_pick(tm, obc_ref, obl_ref)
    na = oa.shape[1]
    y = _dot(oa, w_ref[:na, :]) + _dot(ob, w_ref[na:, :])
    o_ref[...] = x_ref[...] + mod_ref[2:3, :] * y


def _gla_out_kernel(tm, x_ref, mod_ref, oc_ref, ol_ref, gate_ref, gn_ref, w_ref, o_ref):
    gn = gn_ref[...]
    o_pre = _group_pick(tm, oc_ref, ol_ref)
    parts = []
    for h in range(GLA_HEADS):
        sl = slice(h * GLA_DV, (h + 1) * GLA_DV)
        gt = gate_ref[:, sl]
        parts.append((_rms(o_pre[:, sl], gn) * (gt * jax.nn.sigmoid(gt))).astype(BF16))
    y = _dot(jnp.concatenate(parts, axis=-1), w_ref[...])
    o_ref[...] = x_ref[...] + mod_ref[2:3, :] * y


def _mixer_out(kernel, name, x, mod, layer, group_acts, acts, consts):
    tm = TM_PROJ
    nct = N_CTX // tm
    in_specs = [
        pl.BlockSpec((tm, D_MODEL), lambda i: (i, 0)),
        pl.BlockSpec((None, None, None, 3, D_MODEL), lambda i: (layer, _cond_row(i, tm), 1, 0, 0)),
    ]
    flat = []
    for a_c, a_l in group_acts:
        in_specs.append(pl.BlockSpec((tm, a_c.shape[1]), lambda i: (jnp.minimum(i, nct - 1), 0)))
        in_specs.append(pl.BlockSpec((tm, a_l.shape[1]), lambda i: (jnp.maximum(i - nct, 0), 0)))
        flat += [a_c, a_l]
    in_specs += [pl.BlockSpec((tm, a.shape[1]), lambda i: (i, 0)) for a in acts]
    in_specs += [pl.BlockSpec(c.shape, lambda i: (0, 0)) for c in consts]
    return pl.pallas_call(
        functools.partial(kernel, tm),
        grid=(N_TOK // tm,),
        in_specs=in_specs,
        out_specs=pl.BlockSpec((tm, D_MODEL), lambda i: (i, 0)),
        out_shape=jax.ShapeDtypeStruct((N_TOK, D_MODEL), F32),
        compiler_params=_cparams(("parallel",)),
        name=name,
    )(x, mod, *flat, *acts, *consts)


def _final_kernel(x_ref, g_ref, o_ref):
    o_ref[...] = _rms(x_ref[...], g_ref[...])


def _final_norm(x, g, row0, n_rows):
    tm = TM_PROJ
    off = row0 // tm
    return pl.pallas_call(
        _final_kernel,
        grid=(n_rows // tm,),
        in_specs=[pl.BlockSpec((tm, D_MODEL), lambda i: (off + i, 0)),
                  pl.BlockSpec((1, D_MODEL), lambda i: (0, 0))],
        out_specs=pl.BlockSpec((tm, D_MODEL), lambda i: (i, 0)),
        out_shape=jax.ShapeDtypeStruct((n_rows, D_MODEL), F32),
        compiler_params=_cparams(("parallel",)),
        name="final_norm",
    )(x, g.reshape(1, D_MODEL))


def _permute_heads(w, axis):
    grp = A_HEADS // A_KV_HEADS
    order = [(2 * p + j) * grp + g for p in range(A_KV_HEADS // 2) for g in range(grp) for j in range(2)]
    rest = tuple(np.delete(w.shape, axis))
    w = jnp.moveaxis(w, axis, 0).reshape((A_HEADS, HEAD_DIM) + rest)
    w = w[np.asarray(order)].reshape((A_HEADS * HEAD_DIM,) + rest)
    return jnp.moveaxis(w, 0, axis)


def kernel(x_prompt, x_sample, cache_a_k, cache_a_v, cache_b_k, cache_b_v, state_gla, c, c_ctx,
           w_mod, b_mod, g_norm, w_ffn_gate, w_ffn_up, w_ffn_down, w_attn_in, w_attn_out,
           g_qnorm, g_knorm, na_rel_bias, w_gla_in, w_gla_gup, b_gla_gup, g_gla_norm, w_gla_out, g_final):
    x = jnp.concatenate([x_prompt.reshape(N_CTX, D_MODEL), x_sample.reshape(N_LAT, D_MODEL)], axis=0)
    cond = jnp.concatenate([c_ctx[None, :], c, jnp.zeros((COND_ROWS - 1 - DEC_BATCH, D_MODEL), F32)], axis=0)
    mod = _modulation(cond, w_mod, b_mod)

    wg_all, wu_all, wd_all = (w.astype(BF16) for w in (w_ffn_gate, w_ffn_up, w_ffn_down))
    rope_tabs = _rope_tables()
    n_attn = w_attn_in.shape[0]
    ck_a = cache_a_k.reshape(DEC_BATCH, n_attn, PAST_LEN, A_KV_HEADS * HEAD_DIM)
    cv_a = cache_a_v.reshape(DEC_BATCH, n_attn, PAST_LEN, A_KV_HEADS * HEAD_DIM)
    ck_b = cache_b_k.reshape(DEC_BATCH, n_attn, PAST_LEN, B_HEADS * HEAD_DIM)
    cv_b = cache_b_v.reshape(DEC_BATCH, n_attn, PAST_LEN, B_HEADS * HEAD_DIM)

    new_a_k, new_a_v, new_b_k, new_b_v, new_gla = [], [], [], [], []
    for l in range(DEPTH):
        x = _ffn(x, mod, l, 0, g_norm[l, 0], wg_all[l, 0], wu_all[l, 0], wd_all[l, 0])
        if l % 2 == 0:
            i = l // 2
            gq = jnp.tile(g_qnorm[i], 2).reshape(1, PAIR)
            gk = jnp.tile(g_knorm[i], 2).reshape(1, PAIR)
            n_qa = A_HEADS * HEAD_DIM
            w_in = w_attn_in[i].at[:, :n_qa].set(_permute_heads(w_attn_in[i][:, :n_qa], axis=1)).astype(BF16)
            w_out = w_attn_out[i].at[:n_qa].set(_permute_heads(w_attn_out[i][:n_qa], axis=0)).astype(BF16)
            qa_c, ka_c, va_c, qb_c, kb_c, vb_c, ka_n, va_f, kb_f, vb_f = _attn_in(
                x, mod, l, g_norm[l, 1], w_in, gq, gk, lat=False)
            qa_l, ka_l, va_l, qb_l, kb_l, vb_l = _attn_in(
                x, mod, l, g_norm[l, 1], w_in, gq, gk, lat=True, rope_tabs=rope_tabs)
            oa_c = _attention(qa_c, ka_c, va_c, lat=False, tq=SEQ)
            ob_c = _attention(qb_c, kb_c, vb_c, lat=False, tq=SEQ)
            oa_l = _attention(qa_l, ka_l, va_l, lat=True, tq=TQ_LAT_A, layer_i=i, cache_k=ck_a, cache_v=cv_a)
            ob_l = _attention(qb_l, kb_l, vb_l, lat=True, tq=TQ_LAT_B, layer_i=i, cache_k=ck_b, cache_v=cv_b,
                              bias_tab=_na_bias_table(na_rel_bias[i]))
            x = _mixer_out(_attn_out_kernel, "attn_out", x, mod, l, [(oa_c, oa_l), (ob_c, ob_l)], [], [w_out])
            new_a_k.append(ka_n.reshape(BATCH, SEQ, A_KV_HEADS, HEAD_DIM))
            new_a_v.append(va_f.reshape(BATCH, SEQ, A_KV_HEADS, HEAD_DIM))
            new_b_k.append(kb_f.reshape(BATCH, SEQ, B_HEADS, HEAD_DIM))
            new_b_v.append(vb_f.reshape(BATCH, SEQ, B_HEADS, HEAD_DIM))
        else:
            j = l // 2
            w_in = jnp.pad(w_gla_in[j], ((0, 0), (0, GLA_D_PAD - 2 * GLA_RANK))).astype(BF16)
            w_gup = jnp.zeros((GLA_D_PAD, 2 * GLA_KDIM), F32)
            w_gup = w_gup.at[:GLA_RANK, :GLA_KDIM].set(w_gla_gup[j, 0])
            w_gup = w_gup.at[GLA_RANK:2 * GLA_RANK, GLA_KDIM:].set(w_gla_gup[j, 1]).astype(BF16)
            b_gup = b_gla_gup[j].reshape(1, 2 * GLA_KDIM)
            q, k, v, gt, lgf, lgb = _gla_in(x, mod, l, g_norm[l, 1], w_in, w_gup, b_gup)
            o_c, s_fin = _gla(q, k, v, lgf, lgb, lat=False)
            (o_l,) = _gla(q, k, v, lgf, lgb, lat=True, layer_j=j, state=state_gla)
            x = _mixer_out(_gla_out_kernel, "gla_out", x, mod, l, [(o_c, o_l)], [gt],
                           [g_gla_norm[j].reshape(1, GLA_DV), w_gla_out[j].astype(BF16)])
            new_gla.append(s_fin)
        x = _ffn(x, mod, l, 2, g_norm[l, 2], wg_all[l, 1], wu_all[l, 1], wd_all[l, 1])

    y_ctx = _final_norm(x, g_final, 0, N_CTX)
    y_lat = _final_norm(x, g_final, N_CTX, N_LAT)
    return (y_ctx.reshape(BATCH, SEQ, D_MODEL), y_lat.reshape(DEC_BATCH, DEC_SEQ, D_MODEL),
            jnp.stack(new_a_k, axis=1), jnp.stack(new_a_v, axis=1),
            jnp.stack(new_b_k, axis=1), jnp.stack(new_b_v, axis=1), jnp.stack(new_gla, axis=1))
```

```python
import functools

import numpy as np
import jax
import jax.numpy as jnp
from jax import lax
from jax.experimental import pallas as pl
from jax.experimental.pallas import tpu as pltpu

D_MODEL = 1024
BATCH = 32
SEQ = 256
DEPTH = 4
DEC_BATCH = 8
DEC_SEQ = 1024
PAST_LEN = 512
GRID_W = 64
HEAD_DIM = 64
A_HEADS = 8
A_KV_HEADS = 4
B_HEADS = 8
ROPE_THETA = 10000.0
NA_WIN_H = 8
NA_WIN_W = 16
GLA_HEADS = 4
GLA_DK = D_MODEL // 2 // GLA_HEADS
GLA_DV = D_MODEL // GLA_HEADS
GLA_KDIM = GLA_HEADS * GLA_DK
GLA_VDIM = GLA_HEADS * GLA_DV
GLA_RANK = 16
GLA_GATE_NORM = 16.0
GLA_CHUNK = 64
D_FF = ((8 * D_MODEL // 3 + 127) // 128) * 128
FFN_RES = 0.5
N_MOD = 9
EPS = 1e-6

N_CTX = BATCH * SEQ
N_LAT = DEC_BATCH * DEC_SEQ
N_TOK = N_CTX + N_LAT
COND_ROWS = 16
LANES = 128
PAIR = 2 * HEAD_DIM
GLA_D_PAD = LANES
VMEM_LIMIT = 56 * 1024 * 1024

F32 = jnp.float32
BF16 = jnp.bfloat16

TM_FFN = 1024
TF_FFN = 256
FFN_CHUNKS = D_FF // TF_FFN
assert FFN_CHUNKS * TF_FFN == D_FF and FFN_CHUNKS % 2 == 1
TM_PROJ = 512
TQ_LAT_A = 128
TQ_LAT_B = 256
ATTN_SUB_ROWS = 256
GLA_UNROLL = 8


def _cparams(sem):
    return pltpu.CompilerParams(dimension_semantics=sem, vmem_limit_bytes=VMEM_LIMIT)


def _cond_row(i, tm):
    nct = N_CTX // tm
    per_batch = DEC_SEQ // tm
    return jnp.where(i < nct, 0, 1 + (i - nct) // per_batch)


def _rms(x, g):
    ms = jnp.mean(x * x, axis=-1, keepdims=True)
    return x * lax.rsqrt(ms + EPS) * g


def _dot(a, b):
    return jnp.dot(a, b, preferred_element_type=F32)


def _dot_nt(a, b):
    return lax.dot_general(a, b, (((1,), (1,)), ((), ())), preferred_element_type=F32)


def _dot_tn(a, b):
    return lax.dot_general(a, b, (((0,), (0,)), ((), ())), preferred_element_type=F32)


def _mod_kernel(c_ref, w_ref, b_ref, o_ref):
    c = c_ref[...]
    s = c * jax.nn.sigmoid(c)
    o_ref[...] = jnp.dot(s, w_ref[...], precision=lax.Precision.HIGHEST,
                         preferred_element_type=F32) + b_ref[...]


def _modulation(cond, w_mod, b_mod):
    tn = D_MODEL
    n_mod = N_MOD * D_MODEL
    out = pl.pallas_call(
        _mod_kernel,
        grid=(DEPTH, n_mod // tn),
        in_specs=[
            pl.BlockSpec((COND_ROWS, D_MODEL), lambda l, j: (0, 0)),
            pl.BlockSpec((None, D_MODEL, tn), lambda l, j: (l, 0, j)),
            pl.BlockSpec((None, 1, tn), lambda l, j: (l, 0, j)),
        ],
        out_specs=pl.BlockSpec((None, COND_ROWS, tn), lambda l, j: (l, 0, j)),
        out_shape=jax.ShapeDtypeStruct((DEPTH, COND_ROWS, n_mod), F32),
        compiler_params=_cparams(("parallel", "parallel")),
        name="adaln_mod",
    )(cond, w_mod, b_mod.reshape(DEPTH, 1, n_mod))
    return out.reshape(DEPTH, COND_ROWS, 3, 3, D_MODEL)


def _ffn_kernel(split_x, *refs):
    if split_x:
        xc_ref, xl_ref, *refs = refs
    else:
        x_ref, *refs = refs
    mod_ref, g_ref, wg_ref, wu_ref, wd_ref, o_ref, h_ref, acc_ref, g_scr, u_scr = refs
    tm = o_ref.shape[0]
    if split_x:
        x = jnp.where(pl.program_id(0) < N_CTX // tm, xc_ref[...], xl_ref[...])
    else:
        x = x_ref[...]
    ms = jnp.mean(x * x, axis=-1, keepdims=True)
    gain = g_ref[...] * (1.0 + mod_ref[1:2, :])
    h_ref[...] = (x * lax.rsqrt(ms + EPS) * gain + mod_ref[0:1, :]).astype(BF16)
    acc_ref[...] = jnp.zeros_like(acc_ref)

    def cols(j):
        return pl.ds(pl.multiple_of(j * TF_FFN, TF_FFN), TF_FFN)

    def gate_up(j):
        h = h_ref[...]
        return _dot(h, wg_ref[:, cols(j)]), _dot(h, wu_ref[:, cols(j)])

    def down(j, g, u):
        a = (g * jax.nn.sigmoid(g) * u).astype(BF16)
        acc_ref[...] += _dot(a, wd_ref[cols(j), :])

    g_scr[...], u_scr[...] = gate_up(0)

    def chunk_pair(jj, carry):
        j = 2 * jj
        g1, u1 = gate_up(j + 1)
        down(j, g_scr[...], u_scr[...])
        g2, u2 = gate_up(j + 2)
        down(j + 1, g1, u1)
        g_scr[...], u_scr[...] = g2, u2
        return carry

    lax.fori_loop(0, FFN_CHUNKS // 2, chunk_pair, None)
    down(FFN_CHUNKS - 1, g_scr[...], u_scr[...])
    o_ref[...] = x + (FFN_RES * mod_ref[2:3, :]) * acc_ref[...]


def _ffn(x, mod, layer, sub, g, wg, wu, wd):
    tm = TM_FFN
    nct = N_CTX // tm
    which = sub // 2
    resident = lambda w: pl.BlockSpec((None, None) + w.shape[2:], lambda i: (layer, which, 0, 0),
                                      pipeline_mode=pl.Buffered(1))
    split_x = isinstance(x, tuple)
    if split_x:
        x_specs = [pl.BlockSpec((tm, D_MODEL), lambda i: (jnp.minimum(i, nct - 1), 0)),
                   pl.BlockSpec((tm, D_MODEL), lambda i: (jnp.maximum(i - nct, 0), 0))]
        xs = list(x)
    else:
        x_specs = [pl.BlockSpec((tm, D_MODEL), lambda i: (i, 0))]
        xs = [x]
    return pl.pallas_call(
        functools.partial(_ffn_kernel, split_x),
        grid=(N_TOK // tm,),
        in_specs=x_specs + [
            pl.BlockSpec((None, None, None, 3, D_MODEL), lambda i: (layer, _cond_row(i, tm), sub, 0, 0)),
            pl.BlockSpec((1, D_MODEL), lambda i: (0, 0)),
            resident(wg), resident(wu), resident(wd),
        ],
        out_specs=pl.BlockSpec((tm, D_MODEL), lambda i: (i, 0)),
        out_shape=jax.ShapeDtypeStruct((N_TOK, D_MODEL), F32),
        scratch_shapes=[pltpu.VMEM((tm, D_MODEL), BF16), pltpu.VMEM((tm, D_MODEL), F32),
                        pltpu.VMEM((tm, TF_FFN), F32), pltpu.VMEM((tm, TF_FFN), F32)],
        compiler_params=_cparams(("parallel",)),
        name="ffn",
    )(*xs, mod, g.reshape(1, D_MODEL), wg, wu, wd)


def _mixer_h(x_ref, mod_ref, g_ref):
    y = _rms(x_ref[...], g_ref[...])
    return (y * (1.0 + mod_ref[1:2, :]) + mod_ref[0:1, :]).astype(BF16)


ATTN_WIDTHS = (A_HEADS * HEAD_DIM, A_KV_HEADS * HEAD_DIM, A_KV_HEADS * HEAD_DIM,
               B_HEADS * HEAD_DIM, B_HEADS * HEAD_DIM, B_HEADS * HEAD_DIM)


def _attn_in_kernel(lat, x_ref, mod_ref, g_ref, w_ref, gq_ref, gk_ref, *refs):
    if lat:
        cos_ref, sin_ref, *o_refs = refs
    else:
        o_refs = refs[4:]
    h = _mixer_h(x_ref, mod_ref, g_ref)
    nb = x_ref.shape[0] // SEQ
    first = _head_masks()
    scale = HEAD_DIM ** -0.5
    offs = np.cumsum((0,) + ATTN_WIDTHS)

    def seg(i):
        return _dot(h, w_ref[:, offs[i]:offs[i + 1]])

    def normed_pairs(y, g2):
        for p in range(y.shape[1] // PAIR):
            z = _pair_rms(y[:, p * PAIR:(p + 1) * PAIR], g2, first)
            if lat:
                z = _pair_rope(z, cos_ref[...], sin_ref[...])
            yield slice(p * PAIR, (p + 1) * PAIR), z

    for sl, z in normed_pairs(seg(0), gq_ref[...]):
        o_refs[0][:, sl] = (z * scale).astype(BF16)
    for sl, z in normed_pairs(seg(1), gk_ref[...]):
        o_refs[1][:, sl] = z.astype(BF16)
        if not lat:
            o_refs[6][:, :, sl] = z.reshape(nb, SEQ, PAIR)
    for i in (2, 3, 4, 5):
        y = seg(i)
        o_refs[i][...] = ((y * scale) if i == 3 else y).astype(BF16)
        if not lat and i != 3:
            o_refs[{2: 7, 4: 8, 5: 9}[i]][...] = y.reshape(nb, SEQ, y.shape[1])


def _gla_in_kernel(widths, x_ref, mod_ref, g_ref, w_ref, wgup_ref, bgup_ref, *o_refs):
    h = _mixer_h(x_ref, mod_ref, g_ref)
    off = 0
    for o_ref, wd in zip(o_refs[:4], widths):
        o_ref[...] = _dot(h, w_ref[:, off:off + wd])
        off += wd
    d = _dot(h, w_ref[:, off:off + GLA_D_PAD]).astype(BF16)
    pre = _dot(d, wgup_ref[...]) + bgup_ref[...]
    lg = (jnp.minimum(pre, 0.0) - jnp.log1p(jnp.exp(-jnp.abs(pre)))) * (1.0 / GLA_GATE_NORM)
    o_refs[4][...] = lg[:, :GLA_KDIM]
    o_refs[5][...] = lg[:, GLA_KDIM:]


def _mixer_in_specs(layer, tm, n_in, tile0=0):
    return [
        pl.BlockSpec((tm, D_MODEL), lambda i: (tile0 + i, 0)),
        pl.BlockSpec((None, None, None, 3, D_MODEL), lambda i: (layer, _cond_row(tile0 + i, tm), 1, 0, 0)),
        pl.BlockSpec((1, D_MODEL), lambda i: (0, 0)),
        pl.BlockSpec((D_MODEL, n_in), lambda i: (0, 0)),
    ]


def _attn_in(x, mod, layer, g, w_in, gq, gk, *, lat, rope_tabs=None, layer_i=0, caches=None):
    tm = TM_PROJ
    n_rows, tile0 = (N_LAT, N_CTX // tm) if lat else (N_CTX, 0)
    in_specs = _mixer_in_specs(layer, tm, sum(ATTN_WIDTHS), tile0)
    in_specs += [pl.BlockSpec((1, PAIR), lambda i: (0, 0))] * 2
    args = [x, mod, g.reshape(1, D_MODEL), w_in, gq, gk]
    out_specs = [pl.BlockSpec((tm, wd), lambda i: (i, 0)) for wd in ATTN_WIDTHS]
    out_shape = [jax.ShapeDtypeStruct((n_rows, wd), BF16) for wd in ATTN_WIDTHS]
    aliases = {}
    if lat:
        per_seq = DEC_SEQ // tm
        in_specs += [pl.BlockSpec((tm, PAIR), lambda i: (i % per_seq, 0))] * 2
        args += list(rope_tabs)
    else:
        for c in caches:
            aliases[len(args)] = len(out_specs)
            in_specs.append(pl.BlockSpec(memory_space=pl.ANY))
            args.append(c)
            out_specs.append(pl.BlockSpec((tm // SEQ, None, SEQ, c.shape[-1]), lambda i: (i, layer_i, 0, 0)))
            out_shape.append(jax.ShapeDtypeStruct(c.shape, c.dtype))
    return pl.pallas_call(
        functools.partial(_attn_in_kernel, lat),
        grid=(n_rows // tm,),
        in_specs=in_specs,
        out_specs=out_specs,
        out_shape=out_shape,
        input_output_aliases=aliases,
        compiler_params=_cparams(("parallel",)),
        name="attn_in_lat" if lat else "attn_in_ctx",
    )(*args)


def _gla_in(x, mod, layer, g, w_in, w_gup, b_gup):
    tm = TM_PROJ
    widths = (GLA_KDIM, GLA_KDIM, GLA_VDIM, GLA_VDIM)
    out_w = widths + (GLA_KDIM, GLA_KDIM)
    n_in = sum(widths) + GLA_D_PAD
    return pl.pallas_call(
        functools.partial(_gla_in_kernel, widths),
        grid=(N_TOK // tm,),
        in_specs=_mixer_in_specs(layer, tm, n_in) + [
            pl.BlockSpec((GLA_D_PAD, 2 * GLA_KDIM), lambda i: (0, 0)),
            pl.BlockSpec((1, 2 * GLA_KDIM), lambda i: (0, 0)),
        ],
        out_specs=[pl.BlockSpec((tm, wd), lambda i: (i, 0)) for wd in out_w],
        out_shape=[jax.ShapeDtypeStruct((N_TOK, wd), F32) for wd in out_w],
        compiler_params=_cparams(("parallel",)),
        name="gla_in",
    )(x, mod, g.reshape(1, D_MODEL), w_in, w_gup, b_gup)


def _head_masks():
    lane = lax.broadcasted_iota(jnp.int32, (1, PAIR), 1)
    return lane < HEAD_DIM


def _pair_rms(x, g2, first):
    sq = x * x
    s0 = jnp.sum(jnp.where(first, sq, 0.0), axis=-1, keepdims=True)
    s1 = jnp.sum(jnp.where(first, 0.0, sq), axis=-1, keepdims=True)
    ms = jnp.where(first, s0, s1) * (1.0 / HEAD_DIM)
    return x * lax.rsqrt(ms + EPS) * g2


def _pair_rope(x, cos, sin_signed):
    lane = lax.broadcasted_iota(jnp.int32, x.shape, 1)
    quarter = HEAD_DIM // 4
    partner = jnp.where(lane % (2 * quarter) < quarter,
                        pltpu.roll(x, PAIR - quarter, 1), pltpu.roll(x, quarter, 1))
    return x * cos + partner * sin_signed


def _fill_na_bias(tab_ref, bias_scr, qi, tq, first):
    rows = DEC_SEQ // GRID_W
    wh = min(NA_WIN_H, rows)
    neg = -jnp.inf
    for a in range(tq // GRID_W):
        qr = qi * (tq // GRID_W) + a
        rs = jnp.clip(qr - wh // 2, 0, rows - wh)
        for m in range(rows // 2):
            kr0, kr1 = 2 * m, 2 * m + 1
            ok0 = jnp.logical_and(kr0 >= rs, kr0 < rs + wh)
            ok1 = jnp.logical_and(kr1 >= rs, kr1 < rs + wh)
            d0 = jnp.clip(kr0 - qr + NA_WIN_H - 1, 0, 2 * NA_WIN_H - 2)
            d1 = jnp.clip(kr1 - qr + NA_WIN_H - 1, 0, 2 * NA_WIN_H - 2)
            for hh in range(2):
                t0 = jnp.where(ok0, tab_ref[hh, d0], neg)
                t1 = jnp.where(ok1, tab_ref[hh, d1], neg)
                bias_scr[hh * tq + a * GRID_W:hh * tq + (a + 1) * GRID_W, m * PAIR:(m + 1) * PAIR] = (
                    jnp.where(first, t0, t1))


def _attn_kernel(has_ctx, has_bias, *refs):
    it = iter(refs)
    q_ref, k_ref, v_ref = next(it), next(it), next(it)
    kc_ref, vc_ref = (next(it), next(it)) if has_ctx else (None, None)
    tab_ref = next(it) if has_bias else None
    o_ref = next(it)
    bias_scr = next(it) if has_bias else None

    first = _head_masks()
    tq = q_ref.shape[0]
    n_half = q_ref.shape[1] // PAIR

    if has_bias:
        @pl.when(pl.program_id(2) == 0)
        def _():
            _fill_na_bias(tab_ref, bias_scr, pl.program_id(1), tq, first)

    zero = jnp.zeros((tq, PAIR), BF16)
    stacked = []
    for j in range(n_half):
        qj = q_ref[:, j * PAIR:(j + 1) * PAIR]
        stacked += [jnp.where(first, qj, zero), jnp.where(first, zero, qj)]
    qs = jnp.concatenate(stacked, axis=0)

    k, v = k_ref[...], v_ref[...]
    kc = kc_ref[...].astype(BF16) if has_ctx else None
    vc = vc_ref[...].astype(BF16) if has_ctx else None
    rows = qs.shape[0]
    sub = min(rows, ATTN_SUB_ROWS)

    def scores(i):
        qi = qs[i * sub:(i + 1) * sub]
        return _dot_nt(qi, k), (_dot_nt(qi, kc) if has_ctx else None)

    def finish(i, s, sc):
        if has_bias:
            s = s + bias_scr[i * sub:(i + 1) * sub, :]
        m = jnp.max(s, axis=-1, keepdims=True)
        if has_ctx:
            m = jnp.maximum(m, jnp.max(sc, axis=-1, keepdims=True))
        p = jnp.exp(s - m)
        l = jnp.sum(p, axis=-1, keepdims=True)
        o = _dot(p.astype(BF16), v)
        if has_ctx:
            pc = jnp.exp(sc - m)
            l = l + jnp.sum(pc, axis=-1, keepdims=True)
            o = o + _dot(pc.astype(BF16), vc)
        return o / l

    outs = []
    cur = scores(0)
    for i in range(rows // sub):
        nxt = scores(i + 1) if (i + 1) * sub < rows else None
        outs.append(finish(i, *cur))
        cur = nxt
    o = jnp.concatenate(outs, axis=0) if len(outs) > 1 else outs[0]
    for j in range(n_half):
        o_ref[:, j * PAIR:(j + 1) * PAIR] = jnp.where(
            first, o[2 * j * tq:(2 * j + 1) * tq], o[(2 * j + 1) * tq:(2 * j + 2) * tq]).astype(BF16)


def _attention(q, k, v, *, lat, tq, layer_i=0, cache_k=None, cache_v=None, bias_tab=None):
    has_bias = bias_tab is not None
    n_pairs = k.shape[1] // PAIR
    n_half = q.shape[1] // k.shape[1]
    nb, t = (DEC_BATCH, DEC_SEQ) if lat else (BATCH, SEQ)
    nq = t // tq
    in_specs = [
        pl.BlockSpec((tq, n_half * PAIR), lambda p, qi, b: (b * nq + qi, p)),
        pl.BlockSpec((t, PAIR), lambda p, qi, b: (b, p)),
        pl.BlockSpec((t, PAIR), lambda p, qi, b: (b, p)),
    ]
    args = [q, k, v]
    if lat:
        in_specs += [pl.BlockSpec((None, None, PAST_LEN, PAIR), lambda p, qi, b: (b, layer_i, 0, p))] * 2
        args += [cache_k, cache_v]
    if has_bias:
        in_specs.append(pl.BlockSpec((2,) + bias_tab.shape[1:], lambda p, qi, b: (p, 0, 0, 0)))
        args.append(bias_tab)
    return pl.pallas_call(
        functools.partial(_attn_kernel, lat, has_bias),
        grid=(n_pairs, nq, nb),
        in_specs=in_specs,
        out_specs=pl.BlockSpec((tq, n_half * PAIR), lambda p, qi, b: (b * nq + qi, p)),
        out_shape=jax.ShapeDtypeStruct(q.shape, BF16),
        scratch_shapes=[pltpu.VMEM((2 * tq, t), F32)] if has_bias else [],
        compiler_params=_cparams(("parallel", "parallel", "arbitrary")),
        name="attn_" + ("lat" if lat else "ctx") + ("_a" if n_half == 2 else "_b"),
    )(*args)


def _rope_tables():
    half = HEAD_DIM // 2
    freqs = ROPE_THETA ** (-np.arange(0, half, 2, dtype=np.float32) / half)
    t = np.arange(DEC_SEQ)

    def tab(pos):
        ang = pos.astype(np.float32)[:, None] * freqs[None, :]
        cos = np.concatenate([np.cos(ang), np.cos(ang)], -1)
        sin = np.concatenate([-np.sin(ang), np.sin(ang)], -1)
        return cos, sin

    cr, sr = tab(t // GRID_W)
    cc, sc = tab(t % GRID_W)
    cos = np.concatenate([cr, cc] * 2, -1).astype(np.float32)
    sin = np.concatenate([sr, sc] * 2, -1).astype(np.float32)
    return jnp.asarray(cos), jnp.asarray(sin)


def _na_bias_table(rel_bias):
    w = GRID_W
    n_rel = 2 * NA_WIN_W - 1
    left = w - NA_WIN_W
    rp = jnp.pad(rel_bias, ((0, 0), (0, 0), (left, 2 * w - n_rel - left)))
    flat = jnp.tile(rp, (1, 1, w))
    m = flat[..., :w * (2 * w - 1)].reshape(rel_bias.shape[0], rel_bias.shape[1], w, 2 * w - 1)[..., w - 1:]
    c = np.arange(w)
    ws = np.clip(c - NA_WIN_W // 2, 0, w - NA_WIN_W)
    in_col = (c[None, :] >= ws[:, None]) & (c[None, :] < ws[:, None] + NA_WIN_W)
    m = jnp.where(jnp.asarray(in_col)[None, None], m, -jnp.inf)
    return jnp.concatenate([m, m], axis=-1)


def _chunk_scan(x, reverse):
    t = x.shape[0]
    pos = lax.broadcasted_iota(jnp.int32, x.shape, 0) % GLA_CHUNK
    s = 1
    while s < GLA_CHUNK:
        if reverse:
            x = x + jnp.where(pos < GLA_CHUNK - s, pltpu.roll(x, t - s, 0), 0.0)
        else:
            x = x + jnp.where(pos >= s, pltpu.roll(x, s, 0), 0.0)
        s *= 2
    return x


def _gla_kernel(has_s0, emit_s, *refs):
    it = iter(refs)
    q_ref, k_ref, v_ref, lgf_ref, lgb_ref = (next(it) for _ in range(5))
    s0_ref = next(it) if has_s0 else None
    if emit_s:
        next(it)
    o_ref = next(it)
    sf_ref = next(it) if emit_s else None
    qf_scr, qb_scr, kin_scr, kout_scr, bf_scr, bb_scr, u_scr, st_scr = (next(it) for _ in range(8))

    c_len, dk = GLA_CHUNK, GLA_DK
    t = q_ref.shape[0]
    n = t // c_len
    scale = dk ** -0.5

    lgf, lgb = lgf_ref[...], lgb_ref[...]
    bf = _chunk_scan(lgf, False)
    bb = _chunk_scan(lgb, True)
    q = q_ref[...] * scale
    k = k_ref[...]
    zero = jnp.zeros((t, dk), BF16)
    qf_scr[:, :dk] = (q * jnp.exp(bf)).astype(BF16)
    qf_scr[:, dk:] = zero
    qb_scr[:, :dk] = zero
    qb_scr[:, dk:] = (q * jnp.exp(bb)).astype(BF16)
    kin_scr[:, :dk] = (k * jnp.exp(-bf)).astype(BF16)
    kin_scr[:, dk:] = (k * jnp.exp(-bb)).astype(BF16)
    bf3, bb3 = bf.reshape(n, c_len, dk), bb.reshape(n, c_len, dk)
    kout_scr[:, :dk] = (k * jnp.exp((bf3[:, c_len - 1:, :] - bf3).reshape(t, dk))).astype(BF16)
    kout_scr[:, dk:] = (k * jnp.exp((bb3[:, :1, :] - bb3).reshape(t, dk))).astype(BF16)
    bf_scr[...] = bf
    bb_scr[...] = bb

    row = lax.broadcasted_iota(jnp.int32, (c_len, c_len), 0)
    col = lax.broadcasted_iota(jnp.int32, (c_len, c_len), 1)

    def rows(c):
        return pl.ds(pl.multiple_of(c * c_len, c_len), c_len)

    def intra(c, _):
        r = rows(c)
        v = v_ref[r, :].astype(BF16)
        a2 = _dot_nt(jnp.concatenate([qf_scr[r, :], qb_scr[r, :]], axis=0), kin_scr[r, :])
        a = jnp.where(col <= row, a2[:c_len], 0.0) + jnp.where(col >= row, a2[c_len:], 0.0)
        o_ref[r, :] = _dot(a.astype(BF16), v)
        u = _dot_tn(v, kout_scr[r, :])
        u_scr[c, :, :dk] = u[:, :dk]
        u_scr[n - 1 - c, :, dk:] = u[:, dk:]
        return _

    lax.fori_loop(0, n, intra, None, unroll=GLA_UNROLL)

    if has_s0:
        st_scr[:, :dk] = s0_ref[0].T
        st_scr[:, dk:] = s0_ref[1].T
    else:
        st_scr[...] = jnp.zeros_like(st_scr)

    def inter(i, _):
        rf, rb = rows(i), rows(n - 1 - i)
        st = st_scr[...]
        res = _dot_nt(jnp.concatenate([qf_scr[rf, :], qb_scr[rb, :]], axis=0), st.astype(BF16))
        o_ref[rf, :] += res[:c_len]
        o_ref[rb, :] += res[c_len:]
        b_edge = jnp.concatenate([bf_scr[pl.ds(i * c_len + c_len - 1, 1), :],
                                  bb_scr[pl.ds((n - 1 - i) * c_len, 1), :]], axis=1)
        st_scr[...] = st * jnp.exp(b_edge) + u_scr[i]
        return _

    lax.fori_loop(0, n, inter, None, unroll=GLA_UNROLL)

    if emit_s:
        sf_ref[0] = st_scr[:, :dk].T
        sf_ref[1] = st_scr[:, dk:].T


def _gla(q, k, v, lgf, lgb, *, lat, layer_j=0, state=None, new_state=None):
    if lat:
        nb, t, row0 = DEC_BATCH, DEC_SEQ, N_CTX
    else:
        nb, t, row0 = BATCH, SEQ, 0
    off = row0 // t
    kspec = pl.BlockSpec((t, GLA_DK), lambda b, h: (off + b, h))
    in_specs = [kspec, kspec, pl.BlockSpec((t, GLA_DV), lambda b, h: (off + b, h)), kspec, kspec]
    args = [q, k, v, lgf, lgb]
    if lat:
        in_specs.append(pl.BlockSpec((None, None, 2, None, GLA_DK, GLA_DV),
                                     lambda b, h: (b, layer_j, 0, h, 0, 0)))
        args.append(state)
    out_specs = [pl.BlockSpec((t, GLA_DV), lambda b, h: (b, h))]
    out_shape = [jax.ShapeDtypeStruct((nb * t, GLA_VDIM), F32)]
    aliases = {}
    if not lat:
        aliases[len(args)] = 1
        in_specs.append(pl.BlockSpec(memory_space=pl.ANY))
        args.append(new_state)
        out_specs.append(pl.BlockSpec((None, None, 2, None, GLA_DK, GLA_DV),
                                      lambda b, h: (b, layer_j, 0, h, 0, 0)))
        out_shape.append(jax.ShapeDtypeStruct(new_state.shape, new_state.dtype))
    return pl.pallas_call(
        functools.partial(_gla_kernel, lat, not lat),
        grid=(nb, GLA_HEADS),
        in_specs=in_specs,
        out_specs=out_specs,
        out_shape=out_shape,
        input_output_aliases=aliases,
        scratch_shapes=[
            pltpu.VMEM((t, 2 * GLA_DK), BF16),
            pltpu.VMEM((t, 2 * GLA_DK), BF16),
            pltpu.VMEM((t, 2 * GLA_DK), BF16),
            pltpu.VMEM((t, 2 * GLA_DK), BF16),
            pltpu.VMEM((t, GLA_DK), F32),
            pltpu.VMEM((t, GLA_DK), F32),
            pltpu.VMEM((t // GLA_CHUNK, GLA_DV, 2 * GLA_DK), F32),
            pltpu.VMEM((GLA_DV, 2 * GLA_DK), F32),
        ],
        compiler_params=_cparams(("parallel", "parallel")),
        name="gla_lat" if lat else "gla_ctx",
    )(*args)


def _group_pick(tm, c_ref, l_ref):
    return jnp.where(pl.program_id(0) < N_CTX // tm, c_ref[...], l_ref[...])


def _attn_out_kernel(tm, x_ref, mod_ref, oac_ref, oal_ref, obc_ref, obl_ref, w_ref, o_ref):
    oa = _group_pick(tm, oac_ref, oal_ref)
    ob = _group_pick(tm, obc_ref, obl_ref)
    na = oa.shape[1]
    y = _dot(oa, w_ref[:na, :]) + _dot(ob, w_ref[na:, :])
    o_ref[...] = x_ref[...] + mod_ref[2:3, :] * y


def _gla_out_kernel(tm, x_ref, mod_ref, oc_ref, ol_ref, gate_ref, gn_ref, w_ref, o_ref):
    gn = gn_ref[...]
    o_pre = _group_pick(tm, oc_ref, ol_ref)
    parts = []
    for h in range(GLA_HEADS):
        sl = slice(h * GLA_DV, (h + 1) * GLA_DV)
        gt = gate_ref[:, sl]
        parts.append((_rms(o_pre[:, sl], gn) * (gt * jax.nn.sigmoid(gt))).astype(BF16))
    y = _dot(jnp.concatenate(parts, axis=-1), w_ref[...])
    o_ref[...] = x_ref[...] + mod_ref[2:3, :] * y


def _mixer_out(kernel, name, x, mod, layer, group_acts, acts, consts):
    tm = TM_PROJ
    nct = N_CTX // tm
    in_specs = [
        pl.BlockSpec((tm, D_MODEL), lambda i: (i, 0)),
        pl.BlockSpec((None, None, None, 3, D_MODEL), lambda i: (layer, _cond_row(i, tm), 1, 0, 0)),
    ]
    flat = []
    for a_c, a_l in group_acts:
        in_specs.append(pl.BlockSpec((tm, a_c.shape[1]), lambda i: (jnp.minimum(i, nct - 1), 0)))
        in_specs.append(pl.BlockSpec((tm, a_l.shape[1]), lambda i: (jnp.maximum(i - nct, 0), 0)))
        flat += [a_c, a_l]
    in_specs += [pl.BlockSpec((tm, a.shape[1]), lambda i: (i, 0)) for a in acts]
    in_specs += [pl.BlockSpec(c.shape, lambda i: (0, 0)) for c in consts]
    return pl.pallas_call(
        functools.partial(kernel, tm),
        grid=(N_TOK // tm,),
        in_specs=in_specs,
        out_specs=pl.BlockSpec((tm, D_MODEL), lambda i: (i, 0)),
        out_shape=jax.ShapeDtypeStruct((N_TOK, D_MODEL), F32),
        compiler_params=_cparams(("parallel",)),
        name=name,
    )(x, mod, *flat, *acts, *consts)


def _final_kernel(x_ref, g_ref, o_ref):
    o_ref[...] = _rms(x_ref[...], g_ref[...])


def _final_norm(x, g, row0, n_rows):
    tm = TM_PROJ
    off = row0 // tm
    return pl.pallas_call(
        _final_kernel,
        grid=(n_rows // tm,),
        in_specs=[pl.BlockSpec((tm, D_MODEL), lambda i: (off + i, 0)),
                  pl.BlockSpec((1, D_MODEL), lambda i: (0, 0))],
        out_specs=pl.BlockSpec((tm, D_MODEL), lambda i: (i, 0)),
        out_shape=jax.ShapeDtypeStruct((n_rows, D_MODEL), F32),
        compiler_params=_cparams(("parallel",)),
        name="final_norm",
    )(x, g.reshape(1, D_MODEL))


def _permute_heads(w, axis):
    grp = A_HEADS // A_KV_HEADS
    order = [(2 * p + j) * grp + g for p in range(A_KV_HEADS // 2) for g in range(grp) for j in range(2)]
    rest = tuple(np.delete(w.shape, axis))
    w = jnp.moveaxis(w, axis, 0).reshape((A_HEADS, HEAD_DIM) + rest)
    w = w[np.asarray(order)].reshape((A_HEADS * HEAD_DIM,) + rest)
    return jnp.moveaxis(w, 0, axis)


def kernel(x_prompt, x_sample, cache_a_k, cache_a_v, cache_b_k, cache_b_v, state_gla, c, c_ctx,
           w_mod, b_mod, g_norm, w_ffn_gate, w_ffn_up, w_ffn_down, w_attn_in, w_attn_out,
           g_qnorm, g_knorm, na_rel_bias, w_gla_in, w_gla_gup, b_gla_gup, g_gla_norm, w_gla_out, g_final):
    x = (x_prompt.reshape(N_CTX, D_MODEL), x_sample.reshape(N_LAT, D_MODEL))
    cond = jnp.concatenate([c_ctx[None, :], c, jnp.zeros((COND_ROWS - 1 - DEC_BATCH, D_MODEL), F32)], axis=0)
    mod = _modulation(cond, w_mod, b_mod)

    wg_all, wu_all, wd_all = (w.astype(BF16) for w in (w_ffn_gate, w_ffn_up, w_ffn_down))
    rope_tabs = _rope_tables()
    n_attn = w_attn_in.shape[0]
    ck_a = cache_a_k.reshape(DEC_BATCH, n_attn, PAST_LEN, A_KV_HEADS * HEAD_DIM)
    cv_a = cache_a_v.reshape(DEC_BATCH, n_attn, PAST_LEN, A_KV_HEADS * HEAD_DIM)
    ck_b = cache_b_k.reshape(DEC_BATCH, n_attn, PAST_LEN, B_HEADS * HEAD_DIM)
    cv_b = cache_b_v.reshape(DEC_BATCH, n_attn, PAST_LEN, B_HEADS * HEAD_DIM)

    caches = [jnp.zeros((BATCH, n_attn, SEQ, ATTN_WIDTHS[s]), F32) for s in (1, 2, 4, 5)]
    new_gla = jnp.zeros((BATCH, w_gla_in.shape[0], 2, GLA_HEADS, GLA_DK, GLA_DV), F32)
    for l in range(DEPTH):
        x = _ffn(x, mod, l, 0, g_norm[l, 0], wg_all, wu_all, wd_all)
        if l % 2 == 0:
            i = l // 2
            gq = jnp.tile(g_qnorm[i], 2).reshape(1, PAIR)
            gk = jnp.tile(g_knorm[i], 2).reshape(1, PAIR)
            n_qa = A_HEADS * HEAD_DIM
            w_in = w_attn_in[i].at[:, :n_qa].set(_permute_heads(w_attn_in[i][:, :n_qa], axis=1)).astype(BF16)
            w_out = w_attn_out[i].at[:n_qa].set(_permute_heads(w_attn_out[i][:n_qa], axis=0)).astype(BF16)
            qa_c, ka_c, va_c, qb_c, kb_c, vb_c, *caches = _attn_in(
                x, mod, l, g_norm[l, 1], w_in, gq, gk, lat=False, layer_i=i, caches=caches)
            qa_l, ka_l, va_l, qb_l, kb_l, vb_l = _attn_in(
                x, mod, l, g_norm[l, 1], w_in, gq, gk, lat=True, rope_tabs=rope_tabs)
            oa_c = _attention(qa_c, ka_c, va_c, lat=False, tq=SEQ)
            ob_c = _attention(qb_c, kb_c, vb_c, lat=False, tq=SEQ)
            oa_l = _attention(qa_l, ka_l, va_l, lat=True, tq=TQ_LAT_A, layer_i=i, cache_k=ck_a, cache_v=cv_a)
            ob_l = _attention(qb_l, kb_l, vb_l, lat=True, tq=TQ_LAT_B, layer_i=i, cache_k=ck_b, cache_v=cv_b,
                              bias_tab=_na_bias_table(na_rel_bias[i]))
            x = _mixer_out(_attn_out_kernel, "attn_out", x, mod, l, [(oa_c, oa_l), (ob_c, ob_l)], [], [w_out])
        else:
            j = l // 2
            w_in = jnp.pad(w_gla_in[j], ((0, 0), (0, GLA_D_PAD - 2 * GLA_RANK))).astype(BF16)
            w_gup = jnp.zeros((GLA_D_PAD, 2 * GLA_KDIM), F32)
            w_gup = w_gup.at[:GLA_RANK, :GLA_KDIM].set(w_gla_gup[j, 0])
            w_gup = w_gup.at[GLA_RANK:2 * GLA_RANK, GLA_KDIM:].set(w_gla_gup[j, 1]).astype(BF16)
            b_gup = b_gla_gup[j].reshape(1, 2 * GLA_KDIM)
            q, k, v, gt, lgf, lgb = _gla_in(x, mod, l, g_norm[l, 1], w_in, w_gup, b_gup)
            o_c, new_gla = _gla(q, k, v, lgf, lgb, lat=False, layer_j=j, new_state=new_gla)
            (o_l,) = _gla(q, k, v, lgf, lgb, lat=True, layer_j=j, state=state_gla)
            x = _mixer_out(_gla_out_kernel, "gla_out", x, mod, l, [(o_c, o_l)], [gt],
                           [g_gla_norm[j].reshape(1, GLA_DV), w_gla_out[j].astype(BF16)])
        x = _ffn(x, mod, l, 2, g_norm[l, 2], wg_all, wu_all, wd_all)

    y_ctx = _final_norm(x, g_final, 0, N_CTX)
    y_lat = _final_norm(x, g_final, N_CTX, N_LAT)
    return (y_ctx.reshape(BATCH, SEQ, D_MODEL), y_lat.reshape(DEC_BATCH, DEC_SEQ, D_MODEL),
            caches[0].reshape(BATCH, n_attn, SEQ, A_KV_HEADS, HEAD_DIM),
            caches[1].reshape(BATCH, n_attn, SEQ, A_KV_HEADS, HEAD_DIM),
            caches[2].reshape(BATCH, n_attn, SEQ, B_HEADS, HEAD_DIM),
            caches[3].reshape(BATCH, n_attn, SEQ, B_HEADS, HEAD_DIM), new_gla)
```

```python
import functools

import numpy as np
import jax
import jax.numpy as jnp
from jax import lax
from jax.experimental import pallas as pl
from jax.experimental.pallas import tpu as pltpu

D_MODEL = 1024
BATCH = 32
SEQ = 256
DEPTH = 4
DEC_BATCH = 8
DEC_SEQ = 1024
PAST_LEN = 512
GRID_W = 64
HEAD_DIM = 64
A_HEADS = 8
A_KV_HEADS = 4
B_HEADS = 8
ROPE_THETA = 10000.0
NA_WIN_H = 8
NA_WIN_W = 16
GLA_HEADS = 4
GLA_DK = D_MODEL // 2 // GLA_HEADS
GLA_DV = D_MODEL // GLA_HEADS
GLA_KDIM = GLA_HEADS * GLA_DK
GLA_VDIM = GLA_HEADS * GLA_DV
GLA_RANK = 16
GLA_GATE_NORM = 16.0
GLA_CHUNK = 64
D_FF = ((8 * D_MODEL // 3 + 127) // 128) * 128
FFN_RES = 0.5
N_MOD = 9
EPS = 1e-6

N_CTX = BATCH * SEQ
N_LAT = DEC_BATCH * DEC_SEQ
N_TOK = N_CTX + N_LAT
COND_ROWS = 16
LANES = 128
PAIR = 2 * HEAD_DIM
GLA_D_PAD = LANES
VMEM_LIMIT = 56 * 1024 * 1024

F32 = jnp.float32
BF16 = jnp.bfloat16

TM_FFN = 1024
TF_FFN = 256
FFN_CHUNKS = D_FF // TF_FFN
FFN_ROW_BLOCK = 256
assert FFN_CHUNKS * TF_FFN == D_FF and FFN_CHUNKS % 2 == 1
TM_PROJ = 512
TM_GLA_IN = 1024
TQ_LAT_A = 256
TQ_LAT_B = 512
CTX_SEQS_A = 2
CTX_SEQS_B = 4
ATTN_SUB_ROWS = 256
GLA_UNROLL = 8


def _cparams(sem):
    return pltpu.CompilerParams(dimension_semantics=sem, vmem_limit_bytes=VMEM_LIMIT)


def _cond_row(i, tm):
    nct = N_CTX // tm
    per_batch = DEC_SEQ // tm
    return jnp.where(i < nct, 0, 1 + (i - nct) // per_batch)


def _rms(x, g):
    ms = jnp.mean(x * x, axis=-1, keepdims=True)
    return x * lax.rsqrt(ms + EPS) * g


def _dot(a, b):
    return jnp.dot(a, b, preferred_element_type=F32)


def _dot_nt(a, b):
    return lax.dot_general(a, b, (((1,), (1,)), ((), ())), preferred_element_type=F32)


def _dot_tn(a, b):
    return lax.dot_general(a, b, (((0,), (0,)), ((), ())), preferred_element_type=F32)


def _mod_kernel(c_ref, w_ref, b_ref, o_ref):
    c = c_ref[...]
    s = c * jax.nn.sigmoid(c)
    o_ref[...] = jnp.dot(s, w_ref[...], precision=lax.Precision.HIGHEST,
                         preferred_element_type=F32) + b_ref[...]


def _modulation(cond, w_mod, b_mod):
    tn = D_MODEL
    n_mod = N_MOD * D_MODEL
    out = pl.pallas_call(
        _mod_kernel,
        grid=(DEPTH, n_mod // tn),
        in_specs=[
            pl.BlockSpec((COND_ROWS, D_MODEL), lambda l, j: (0, 0)),
            pl.BlockSpec((None, D_MODEL, tn), lambda l, j: (l, 0, j)),
            pl.BlockSpec((None, 1, tn), lambda l, j: (l, 0, j)),
        ],
        out_specs=pl.BlockSpec((None, COND_ROWS, tn), lambda l, j: (l, 0, j)),
        out_shape=jax.ShapeDtypeStruct((DEPTH, COND_ROWS, n_mod), F32),
        compiler_params=_cparams(("parallel", "parallel")),
        name="adaln_mod",
    )(cond, w_mod, b_mod.reshape(DEPTH, 1, n_mod))
    return out.reshape(DEPTH, COND_ROWS, 3, 3, D_MODEL)


def _ffn_kernel(split_x, *refs):
    if split_x:
        xc_ref, xl_ref, *refs = refs
    else:
        x_ref, *refs = refs
    mod_ref, g_ref, wg_ref, wu_ref, wd_ref, o_ref, h_ref, acc_ref, g_scr, u_scr = refs
    tm = o_ref.shape[0]

    def x_rows(rs):
        if split_x:
            return jnp.where(pl.program_id(0) < N_CTX // tm, xc_ref[rs, :], xl_ref[rs, :])
        return x_ref[rs, :]

    def cols(j):
        return pl.ds(pl.multiple_of(j * TF_FFN, TF_FFN), TF_FFN)

    def gate_up(h, j):
        return _dot(h, wg_ref[:, cols(j)]), _dot(h, wu_ref[:, cols(j)])

    def down(g, u, j):
        a = (g * jax.nn.sigmoid(g) * u).astype(BF16)
        return _dot(a, wd_ref[cols(j), :])

    row_blocks = [slice(r, r + FFN_ROW_BLOCK) for r in range(0, tm, FFN_ROW_BLOCK)]
    gain = g_ref[...] * (1.0 + mod_ref[1:2, :])
    for rs in row_blocks:
        x = x_rows(rs)
        ms = jnp.mean(x * x, axis=-1, keepdims=True)
        h = (x * lax.rsqrt(ms + EPS) * gain + mod_ref[0:1, :]).astype(BF16)
        h_ref[rs, :] = h
        g_scr[rs, :], u_scr[rs, :] = gate_up(h, 0)
    acc_ref[...] = jnp.zeros_like(acc_ref)

    def chunk_pair(jj, carry):
        j = 2 * jj
        h = h_ref[...]
        g1, u1 = gate_up(h, j + 1)
        acc_ref[...] += down(g_scr[...], u_scr[...], j)
        g2, u2 = gate_up(h, j + 2)
        acc_ref[...] += down(g1, u1, j + 1)
        g_scr[...], u_scr[...] = g2, u2
        return carry

    lax.fori_loop(0, FFN_CHUNKS // 2, chunk_pair, None, unroll=True)
    res_gate = FFN_RES * mod_ref[2:3, :]
    for rs in row_blocks:
        y = acc_ref[rs, :] + down(g_scr[rs, :], u_scr[rs, :], FFN_CHUNKS - 1)
        o_ref[rs, :] = x_rows(rs) + res_gate * y


def _ffn(x, mod, layer, sub, g, wg, wu, wd):
    tm = TM_FFN
    nct = N_CTX // tm
    which = sub // 2
    resident = lambda w: pl.BlockSpec((None, None) + w.shape[2:], lambda i: (layer, which, 0, 0),
                                      pipeline_mode=pl.Buffered(1))
    split_x = isinstance(x, tuple)
    if split_x:
        x_specs = [pl.BlockSpec((tm, D_MODEL), lambda i: (jnp.minimum(i, nct - 1), 0)),
                   pl.BlockSpec((tm, D_MODEL), lambda i: (jnp.maximum(i - nct, 0), 0))]
        xs = list(x)
    else:
        x_specs = [pl.BlockSpec((tm, D_MODEL), lambda i: (i, 0))]
        xs = [x]
    return pl.pallas_call(
        functools.partial(_ffn_kernel, split_x),
        grid=(N_TOK // tm,),
        in_specs=x_specs + [
            pl.BlockSpec((None, None, None, 3, D_MODEL), lambda i: (layer, _cond_row(i, tm), sub, 0, 0)),
            pl.BlockSpec((1, D_MODEL), lambda i: (0, 0)),
            resident(wg), resident(wu), resident(wd),
        ],
        out_specs=pl.BlockSpec((tm, D_MODEL), lambda i: (i, 0)),
        out_shape=jax.ShapeDtypeStruct((N_TOK, D_MODEL), F32),
        scratch_shapes=[pltpu.VMEM((tm, D_MODEL), BF16), pltpu.VMEM((tm, D_MODEL), F32),
                        pltpu.VMEM((tm, TF_FFN), F32), pltpu.VMEM((tm, TF_FFN), F32)],
        compiler_params=_cparams(("parallel",)),
        name="ffn",
    )(*xs, mod, g.reshape(1, D_MODEL), wg, wu, wd)


def _mixer_h(x_ref, mod_ref, g_ref):
    y = _rms(x_ref[...], g_ref[...])
    return (y * (1.0 + mod_ref[1:2, :]) + mod_ref[0:1, :]).astype(BF16)


ATTN_WIDTHS = (A_HEADS * HEAD_DIM, A_KV_HEADS * HEAD_DIM, A_KV_HEADS * HEAD_DIM,
               B_HEADS * HEAD_DIM, B_HEADS * HEAD_DIM, B_HEADS * HEAD_DIM)


def _attn_in_kernel(lat, x_ref, mod_ref, g_ref, w_ref, gq_ref, gk_ref, *refs):
    if lat:
        cos_ref, sin_ref, *o_refs = refs
    else:
        o_refs = refs[4:]
    h = _mixer_h(x_ref, mod_ref, g_ref)
    nb = x_ref.shape[0] // SEQ
    first = _head_masks()
    scale = HEAD_DIM ** -0.5
    offs = np.cumsum((0,) + ATTN_WIDTHS)

    def seg(i):
        return _dot(h, w_ref[:, offs[i]:offs[i + 1]])

    def normed_pairs(y, g2):
        for p in range(y.shape[1] // PAIR):
            z = _pair_rms(y[:, p * PAIR:(p + 1) * PAIR], g2, first)
            if lat:
                z = _pair_rope(z, cos_ref[...], sin_ref[...])
            yield slice(p * PAIR, (p + 1) * PAIR), z

    for sl, z in normed_pairs(seg(0), gq_ref[...]):
        o_refs[0][:, sl] = (z * scale).astype(BF16)
    for sl, z in normed_pairs(seg(1), gk_ref[...]):
        o_refs[1][:, sl] = z.astype(BF16)
        if not lat:
            o_refs[6][:, :, sl] = z.reshape(nb, SEQ, PAIR)
    for i in (2, 3, 4, 5):
        y = seg(i)
        o_refs[i][...] = ((y * scale) if i == 3 else y).astype(BF16)
        if not lat and i != 3:
            o_refs[{2: 7, 4: 8, 5: 9}[i]][...] = y.reshape(nb, SEQ, y.shape[1])


def _gla_in_kernel(widths, x_ref, mod_ref, g_ref, w_ref, wgup_ref, bgup_ref, *o_refs):
    h = _mixer_h(x_ref, mod_ref, g_ref)
    off = sum(widths)
    d = _dot(h, w_ref[:, off:off + GLA_D_PAD]).astype(BF16)
    pre = _dot(d, wgup_ref[...]) + bgup_ref[...]
    lg = (jnp.minimum(pre, 0.0) - jnp.log1p(jnp.exp(-jnp.abs(pre)))) * (1.0 / GLA_GATE_NORM)
    o_refs[4][...] = lg[:, :GLA_KDIM]
    o_refs[5][...] = lg[:, GLA_KDIM:]
    off = 0
    for o_ref, wd in zip(o_refs[:4], widths):
        o_ref[...] = _dot(h, w_ref[:, off:off + wd]).astype(o_ref.dtype)
        off += wd


def _mixer_in_specs(layer, tm, n_in, tile0=0):
    return [
        pl.BlockSpec((tm, D_MODEL), lambda i: (tile0 + i, 0)),
        pl.BlockSpec((None, None, None, 3, D_MODEL), lambda i: (layer, _cond_row(tile0 + i, tm), 1, 0, 0)),
        pl.BlockSpec((1, D_MODEL), lambda i: (0, 0)),
        pl.BlockSpec((D_MODEL, n_in), lambda i: (0, 0)),
    ]


def _attn_in(x, mod, layer, g, w_in, gq, gk, *, lat, rope_tabs=None, layer_i=0, caches=None):
    tm = TM_PROJ
    n_rows, tile0 = (N_LAT, N_CTX // tm) if lat else (N_CTX, 0)
    in_specs = _mixer_in_specs(layer, tm, sum(ATTN_WIDTHS), tile0)
    in_specs += [pl.BlockSpec((1, PAIR), lambda i: (0, 0))] * 2
    args = [x, mod, g.reshape(1, D_MODEL), w_in, gq, gk]
    out_specs = [pl.BlockSpec((tm, wd), lambda i: (i, 0)) for wd in ATTN_WIDTHS]
    out_shape = [jax.ShapeDtypeStruct((n_rows, wd), BF16) for wd in ATTN_WIDTHS]
    aliases = {}
    if lat:
        per_seq = DEC_SEQ // tm
        in_specs += [pl.BlockSpec((tm, PAIR), lambda i: (i % per_seq, 0))] * 2
        args += list(rope_tabs)
    else:
        for c in caches:
            aliases[len(args)] = len(out_specs)
            in_specs.append(pl.BlockSpec(memory_space=pl.ANY))
            args.append(c)
            out_specs.append(pl.BlockSpec((tm // SEQ, None, SEQ, c.shape[-1]), lambda i: (i, layer_i, 0, 0)))
            out_shape.append(jax.ShapeDtypeStruct(c.shape, c.dtype))
    return pl.pallas_call(
        functools.partial(_attn_in_kernel, lat),
        grid=(n_rows // tm,),
        in_specs=in_specs,
        out_specs=out_specs,
        out_shape=out_shape,
        input_output_aliases=aliases,
        compiler_params=_cparams(("parallel",)),
        name="attn_in_lat" if lat else "attn_in_ctx",
    )(*args)


def _gla_in(x, mod, layer, g, w_in, w_gup, b_gup):
    tm = TM_GLA_IN
    widths = (GLA_KDIM, GLA_KDIM, GLA_VDIM, GLA_VDIM)
    out_w = widths + (GLA_KDIM, GLA_KDIM)
    out_dt = (F32, F32, BF16, F32, F32, F32)
    n_in = sum(widths) + GLA_D_PAD
    return pl.pallas_call(
        functools.partial(_gla_in_kernel, widths),
        grid=(N_TOK // tm,),
        in_specs=_mixer_in_specs(layer, tm, n_in) + [
            pl.BlockSpec((GLA_D_PAD, 2 * GLA_KDIM), lambda i: (0, 0)),
            pl.BlockSpec((1, 2 * GLA_KDIM), lambda i: (0, 0)),
        ],
        out_specs=[pl.BlockSpec((tm, wd), lambda i: (i, 0)) for wd in out_w],
        out_shape=[jax.ShapeDtypeStruct((N_TOK, wd), dt) for wd, dt in zip(out_w, out_dt)],
        compiler_params=_cparams(("parallel",)),
        name="gla_in",
    )(x, mod, g.reshape(1, D_MODEL), w_in, w_gup, b_gup)


def _head_masks():
    lane = lax.broadcasted_iota(jnp.int32, (1, PAIR), 1)
    return lane < HEAD_DIM


def _pair_rms(x, g2, first):
    sq = x * x
    s0 = jnp.sum(jnp.where(first, sq, 0.0), axis=-1, keepdims=True)
    s1 = jnp.sum(jnp.where(first, 0.0, sq), axis=-1, keepdims=True)
    ms = jnp.where(first, s0, s1) * (1.0 / HEAD_DIM)
    return x * lax.rsqrt(ms + EPS) * g2


def _pair_rope(x, cos, sin_signed):
    lane = lax.broadcasted_iota(jnp.int32, x.shape, 1)
    quarter = HEAD_DIM // 4
    partner = jnp.where(lane % (2 * quarter) < quarter,
                        pltpu.roll(x, PAIR - quarter, 1), pltpu.roll(x, quarter, 1))
    return x * cos + partner * sin_signed


def _fill_na_bias(tab_ref, bias_scr, qi, tq, first):
    rows = DEC_SEQ // GRID_W
    wh = min(NA_WIN_H, rows)
    neg = -jnp.inf
    for a in range(tq // GRID_W):
        qr = qi * (tq // GRID_W) + a
        rs = jnp.clip(qr - wh // 2, 0, rows - wh)
        for m in range(rows // 2):
            kr0, kr1 = 2 * m, 2 * m + 1
            ok0 = jnp.logical_and(kr0 >= rs, kr0 < rs + wh)
            ok1 = jnp.logical_and(kr1 >= rs, kr1 < rs + wh)
            d0 = jnp.clip(kr0 - qr + NA_WIN_H - 1, 0, 2 * NA_WIN_H - 2)
            d1 = jnp.clip(kr1 - qr + NA_WIN_H - 1, 0, 2 * NA_WIN_H - 2)
            for hh in range(2):
                t0 = jnp.where(ok0, tab_ref[hh, d0], neg)
                t1 = jnp.where(ok1, tab_ref[hh, d1], neg)
                bias_scr[hh * tq + a * GRID_W:hh * tq + (a + 1) * GRID_W, m * PAIR:(m + 1) * PAIR] = (
                    jnp.where(first, t0, t1))


def _attn_kernel(has_ctx, has_bias, n_seq, *refs):
    it = iter(refs)
    q_ref, k_ref, v_ref = next(it), next(it), next(it)
    kc_ref, vc_ref = (next(it), next(it)) if has_ctx else (None, None)
    tab_ref = next(it) if has_bias else None
    o_ref = next(it)
    bias_scr = next(it) if has_bias else None

    first = _head_masks()
    tq = q_ref.shape[0] // n_seq
    tk = k_ref.shape[0] // n_seq
    n_half = q_ref.shape[1] // PAIR
    rows = 2 * n_half * tq
    sub = min(rows, ATTN_SUB_ROWS)

    if has_bias:
        @pl.when(pl.program_id(2) == 0)
        def _():
            _fill_na_bias(tab_ref, bias_scr, pl.program_id(1), tq, first)

    zero = jnp.zeros((tq, PAIR), BF16)
    qs, ks, vs = [], [], []
    for sq in range(n_seq):
        stacked = []
        for j in range(n_half):
            qj = q_ref[sq * tq:(sq + 1) * tq, j * PAIR:(j + 1) * PAIR]
            stacked += [jnp.where(first, qj, zero), jnp.where(first, zero, qj)]
        qs.append(jnp.concatenate(stacked, axis=0))
        ks.append(k_ref[sq * tk:(sq + 1) * tk, :])
        vs.append(v_ref[sq * tk:(sq + 1) * tk, :])
    kc = kc_ref[...].astype(BF16) if has_ctx else None
    vc = vc_ref[...].astype(BF16) if has_ctx else None

    def scores(sq, i):
        qi = qs[sq][i * sub:(i + 1) * sub]
        return _dot_nt(qi, ks[sq]), (_dot_nt(qi, kc) if has_ctx else None)

    def finish(sq, i, s, sc):
        if has_bias:
            s = s + bias_scr[i * sub:(i + 1) * sub, :]
        m = jnp.max(s, axis=-1, keepdims=True)
        if has_ctx:
            m = jnp.maximum(m, jnp.max(sc, axis=-1, keepdims=True))
        p = jnp.exp(s - m)
        l = jnp.sum(p, axis=-1, keepdims=True)
        o = _dot(p.astype(BF16), vs[sq])
        if has_ctx:
            pc = jnp.exp(sc - m)
            l = l + jnp.sum(pc, axis=-1, keepdims=True)
            o = o + _dot(pc.astype(BF16), vc)
        return o / l

    groups = [(sq, i) for sq in range(n_seq) for i in range(rows // sub)]
    outs = [[] for _ in range(n_seq)]
    cur = scores(*groups[0])
    for n, (sq, i) in enumerate(groups):
        nxt = scores(*groups[n + 1]) if n + 1 < len(groups) else None
        outs[sq].append(finish(sq, i, *cur))
        cur = nxt
    for sq in range(n_seq):
        o = jnp.concatenate(outs[sq], axis=0) if len(outs[sq]) > 1 else outs[sq][0]
        for j in range(n_half):
            o_ref[sq * tq:(sq + 1) * tq, j * PAIR:(j + 1) * PAIR] = jnp.where(
                first, o[2 * j * tq:(2 * j + 1) * tq], o[(2 * j + 1) * tq:(2 * j + 2) * tq]).astype(BF16)


def _attention(q, k, v, *, lat, tq, n_seq=1, layer_i=0, cache_k=None, cache_v=None, bias_tab=None):
    has_bias = bias_tab is not None
    n_pairs = k.shape[1] // PAIR
    n_half = q.shape[1] // k.shape[1]
    nb, t = (DEC_BATCH, DEC_SEQ) if lat else (BATCH, SEQ)
    nq = t // tq
    assert n_seq == 1 or (nq == 1 and not lat)
    in_specs = [
        pl.BlockSpec((n_seq * tq, n_half * PAIR), lambda p, qi, b: (b * nq + qi, p)),
        pl.BlockSpec((n_seq * t, PAIR), lambda p, qi, b: (b, p)),
        pl.BlockSpec((n_seq * t, PAIR), lambda p, qi, b: (b, p)),
    ]
    args = [q, k, v]
    if lat:
        in_specs += [pl.BlockSpec((None, None, PAST_LEN, PAIR), lambda p, qi, b: (b, layer_i, 0, p))] * 2
        args += [cache_k, cache_v]
    if has_bias:
        in_specs.append(pl.BlockSpec((2,) + bias_tab.shape[1:], lambda p, qi, b: (p, 0, 0, 0)))
        args.append(bias_tab)
    return pl.pallas_call(
        functools.partial(_attn_kernel, lat, has_bias, n_seq),
        grid=(n_pairs, nq, nb // n_seq),
        in_specs=in_specs,
        out_specs=pl.BlockSpec((n_seq * tq, n_half * PAIR), lambda p, qi, b: (b * nq + qi, p)),
        out_shape=jax.ShapeDtypeStruct(q.shape, BF16),
        scratch_shapes=[pltpu.VMEM((2 * tq, t), F32)] if has_bias else [],
        compiler_params=_cparams(("parallel", "parallel", "arbitrary")),
        name="attn_" + ("lat" if lat else "ctx") + ("_a" if n_half == 2 else "_b"),
    )(*args)


def _rope_tables():
    half = HEAD_DIM // 2
    freqs = ROPE_THETA ** (-np.arange(0, half, 2, dtype=np.float32) / half)
    t = np.arange(DEC_SEQ)

    def tab(pos):
        ang = pos.astype(np.float32)[:, None] * freqs[None, :]
        cos = np.concatenate([np.cos(ang), np.cos(ang)], -1)
        sin = np.concatenate([-np.sin(ang), np.sin(ang)], -1)
        return cos, sin

    cr, sr = tab(t // GRID_W)
    cc, sc = tab(t % GRID_W)
    cos = np.concatenate([cr, cc] * 2, -1).astype(np.float32)
    sin = np.concatenate([sr, sc] * 2, -1).astype(np.float32)
    return jnp.asarray(cos), jnp.asarray(sin)


def _na_bias_table(rel_bias):
    w = GRID_W
    n_rel = 2 * NA_WIN_W - 1
    left = w - NA_WIN_W
    rp = jnp.pad(rel_bias, ((0, 0), (0, 0), (left, 2 * w - n_rel - left)))
    flat = jnp.tile(rp, (1, 1, w))
    m = flat[..., :w * (2 * w - 1)].reshape(rel_bias.shape[0], rel_bias.shape[1], w, 2 * w - 1)[..., w - 1:]
    c = np.arange(w)
    ws = np.clip(c - NA_WIN_W // 2, 0, w - NA_WIN_W)
    in_col = (c[None, :] >= ws[:, None]) & (c[None, :] < ws[:, None] + NA_WIN_W)
    m = jnp.where(jnp.asarray(in_col)[None, None], m, -jnp.inf)
    return jnp.concatenate([m, m], axis=-1)


def _chunk_scan(x, reverse):
    t = x.shape[0]
    pos = lax.broadcasted_iota(jnp.int32, x.shape, 0) % GLA_CHUNK
    s = 1
    while s < GLA_CHUNK:
        if reverse:
            x = x + jnp.where(pos < GLA_CHUNK - s, pltpu.roll(x, t - s, 0), 0.0)
        else:
            x = x + jnp.where(pos >= s, pltpu.roll(x, s, 0), 0.0)
        s *= 2
    return x


def _gla_kernel(has_s0, emit_s, *refs):
    it = iter(refs)
    q_ref, k_ref, v_ref, lgf_ref, lgb_ref = (next(it) for _ in range(5))
    s0_ref = next(it) if has_s0 else None
    if emit_s:
        next(it)
    o_ref = next(it)
    sf_ref = next(it) if emit_s else None
    qf_scr, qb_scr, kin_scr, kout_scr, bf_scr, bb_scr, u_scr, st_scr = (next(it) for _ in range(8))

    c_len, dk = GLA_CHUNK, GLA_DK
    t = q_ref.shape[0]
    n = t // c_len
    scale = dk ** -0.5

    lgf, lgb = lgf_ref[...], lgb_ref[...]
    bf = _chunk_scan(lgf, False)
    bb = _chunk_scan(lgb, True)
    q = q_ref[...] * scale
    k = k_ref[...]
    zero = jnp.zeros((t, dk), BF16)
    qf_scr[:, :dk] = (q * jnp.exp(bf)).astype(BF16)
    qf_scr[:, dk:] = zero
    qb_scr[:, :dk] = zero
    qb_scr[:, dk:] = (q * jnp.exp(bb)).astype(BF16)
    kin_scr[:, :dk] = (k * jnp.exp(-bf)).astype(BF16)
    kin_scr[:, dk:] = (k * jnp.exp(-bb)).astype(BF16)
    bf3, bb3 = bf.reshape(n, c_len, dk), bb.reshape(n, c_len, dk)
    kout_scr[:, :dk] = (k * jnp.exp((bf3[:, c_len - 1:, :] - bf3).reshape(t, dk))).astype(BF16)
    kout_scr[:, dk:] = (k * jnp.exp((bb3[:, :1, :] - bb3).reshape(t, dk))).astype(BF16)
    bf_scr[...] = bf
    bb_scr[...] = bb

    row = lax.broadcasted_iota(jnp.int32, (c_len, c_len), 0)
    col = lax.broadcasted_iota(jnp.int32, (c_len, c_len), 1)

    def rows(c):
        return pl.ds(pl.multiple_of(c * c_len, c_len), c_len)

    def intra(c, _):
        r = rows(c)
        v = v_ref[r, :]
        a2 = _dot_nt(jnp.concatenate([qf_scr[r, :], qb_scr[r, :]], axis=0), kin_scr[r, :])
        a = jnp.where(col <= row, a2[:c_len], 0.0) + jnp.where(col >= row, a2[c_len:], 0.0)
        o_ref[r, :] = _dot(a.astype(BF16), v)
        u = _dot_tn(v, kout_scr[r, :])
        u_scr[c, :, :dk] = u[:, :dk]
        u_scr[n - 1 - c, :, dk:] = u[:, dk:]
        return _

    lax.fori_loop(0, n, intra, None, unroll=GLA_UNROLL)

    if has_s0:
        st_scr[:, :dk] = s0_ref[0].T
        st_scr[:, dk:] = s0_ref[1].T
    else:
        st_scr[...] = jnp.zeros_like(st_scr)

    def inter(i, _):
        rf, rb = rows(i), rows(n - 1 - i)
        st = st_scr[...]
        res = _dot_nt(jnp.concatenate([qf_scr[rf, :], qb_scr[rb, :]], axis=0), st.astype(BF16))
        o_ref[rf, :] += res[:c_len]
        o_ref[rb, :] += res[c_len:]
        b_edge = jnp.concatenate([bf_scr[pl.ds(i * c_len + c_len - 1, 1), :],
                                  bb_scr[pl.ds((n - 1 - i) * c_len, 1), :]], axis=1)
        st_scr[...] = st * jnp.exp(b_edge) + u_scr[i]
        return _

    lax.fori_loop(0, n, inter, None, unroll=GLA_UNROLL)

    if emit_s:
        sf_ref[0] = st_scr[:, :dk].T
        sf_ref[1] = st_scr[:, dk:].T


def _gla(q, k, v, lgf, lgb, *, lat, layer_j=0, state=None, new_state=None):
    if lat:
        nb, t, row0 = DEC_BATCH, DEC_SEQ, N_CTX
    else:
        nb, t, row0 = BATCH, SEQ, 0
    off = row0 // t
    kspec = pl.BlockSpec((t, GLA_DK), lambda b, h: (off + b, h))
    in_specs = [kspec, kspec, pl.BlockSpec((t, GLA_DV), lambda b, h: (off + b, h)), kspec, kspec]
    args = [q, k, v, lgf, lgb]
    if lat:
        in_specs.append(pl.BlockSpec((None, None, 2, None, GLA_DK, GLA_DV),
                                     lambda b, h: (b, layer_j, 0, h, 0, 0)))
        args.append(state)
    out_specs = [pl.BlockSpec((t, GLA_DV), lambda b, h: (b, h))]
    out_shape = [jax.ShapeDtypeStruct((nb * t, GLA_VDIM), F32)]
    aliases = {}
    if not lat:
        aliases[len(args)] = 1
        in_specs.append(pl.BlockSpec(memory_space=pl.ANY))
        args.append(new_state)
        out_specs.append(pl.BlockSpec((None, None, 2, None, GLA_DK, GLA_DV),
                                      lambda b, h: (b, layer_j, 0, h, 0, 0)))
        out_shape.append(jax.ShapeDtypeStruct(new_state.shape, new_state.dtype))
    return pl.pallas_call(
        functools.partial(_gla_kernel, lat, not lat),
        grid=(nb, GLA_HEADS),
        in_specs=in_specs,
        out_specs=out_specs,
        out_shape=out_shape,
        input_output_aliases=aliases,
        scratch_shapes=[
            pltpu.VMEM((t, 2 * GLA_DK), BF16),
            pltpu.VMEM((t, 2 * GLA_DK), BF16),
            pltpu.VMEM((t, 2 * GLA_DK), BF16),
            pltpu.VMEM((t, 2 * GLA_DK), BF16),
            pltpu.VMEM((t, GLA_DK), F32),
            pltpu.VMEM((t, GLA_DK), F32),
            pltpu.VMEM((t // GLA_CHUNK, GLA_DV, 2 * GLA_DK), F32),
            pltpu.VMEM((GLA_DV, 2 * GLA_DK), F32),
        ],
        compiler_params=_cparams(("parallel", "parallel")),
        name="gla_lat" if lat else "gla_ctx",
    )(*args)


def _group_pick(tm, c_ref, l_ref):
    return jnp.where(pl.program_id(0) < N_CTX // tm, c_ref[...], l_ref[...])


def _attn_out_kernel(tm, x_ref, mod_ref, oac_ref, oal_ref, obc_ref, obl_ref, w_ref, o_ref):
    oa = _group_pick(tm, oac_ref, oal_ref)
    ob = _group_pick(tm, obc_ref, obl_ref)
    na = oa.shape[1]
    y = _dot(oa, w_ref[:na, :]) + _dot(ob, w_ref[na:, :])
    o_ref[...] = x_ref[...] + mod_ref[2:3, :] * y


def _gla_out_kernel(tm, x_ref, mod_ref, oc_ref, ol_ref, gate_ref, gn_ref, w_ref, o_ref):
    gn = gn_ref[...]
    o_pre = _group_pick(tm, oc_ref, ol_ref)
    parts = []
    for h in range(GLA_HEADS):
        sl = slice(h * GLA_DV, (h + 1) * GLA_DV)
        gt = gate_ref[:, sl]
        parts.append((_rms(o_pre[:, sl], gn) * (gt * jax.nn.sigmoid(gt))).astype(BF16))
    y = _dot(jnp.concatenate(parts, axis=-1), w_ref[...])
    o_ref[...] = x_ref[...] + mod_ref[2:3, :] * y


def _mixer_out(kernel, name, x, mod, layer, group_acts, acts, consts):
    tm = TM_PROJ
    nct = N_CTX // tm
    in_specs = [
        pl.BlockSpec((tm, D_MODEL), lambda i: (i, 0)),
        pl.BlockSpec((None, None, None, 3, D_MODEL), lambda i: (layer, _cond_row(i, tm), 1, 0, 0)),
    ]
    flat = []
    for a_c, a_l in group_acts:
        in_specs.append(pl.BlockSpec((tm, a_c.shape[1]), lambda i: (jnp.minimum(i, nct - 1), 0)))
        in_specs.append(pl.BlockSpec((tm, a_l.shape[1]), lambda i: (jnp.maximum(i - nct, 0), 0)))
        flat += [a_c, a_l]
    in_specs += [pl.BlockSpec((tm, a.shape[1]), lambda i: (i, 0)) for a in acts]
    in_specs += [pl.BlockSpec(c.shape, lambda i: (0, 0)) for c in consts]
    return pl.pallas_call(
        functools.partial(kernel, tm),
        grid=(N_TOK // tm,),
        in_specs=in_specs,
        out_specs=pl.BlockSpec((tm, D_MODEL), lambda i: (i, 0)),
        out_shape=jax.ShapeDtypeStruct((N_TOK, D_MODEL), F32),
        compiler_params=_cparams(("parallel",)),
        name=name,
    )(x, mod, *flat, *acts, *consts)


def _final_kernel(x_ref, g_ref, o_ref):
    o_ref[...] = _rms(x_ref[...], g_ref[...])


def _final_norm(x, g, row0, n_rows):
    tm = TM_PROJ
    off = row0 // tm
    return pl.pallas_call(
        _final_kernel,
        grid=(n_rows // tm,),
        in_specs=[pl.BlockSpec((tm, D_MODEL), lambda i: (off + i, 0)),
                  pl.BlockSpec((1, D_MODEL), lambda i: (0, 0))],
        out_specs=pl.BlockSpec((tm, D_MODEL), lambda i: (i, 0)),
        out_shape=jax.ShapeDtypeStruct((n_rows, D_MODEL), F32),
        compiler_params=_cparams(("parallel",)),
        name="final_norm",
    )(x, g.reshape(1, D_MODEL))


def _permute_heads(w, axis):
    grp = A_HEADS // A_KV_HEADS
    order = [(2 * p + j) * grp + g for p in range(A_KV_HEADS // 2) for g in range(grp) for j in range(2)]
    rest = tuple(np.delete(w.shape, axis))
    w = jnp.moveaxis(w, axis, 0).reshape((A_HEADS, HEAD_DIM) + rest)
    w = w[np.asarray(order)].reshape((A_HEADS * HEAD_DIM,) + rest)
    return jnp.moveaxis(w, 0, axis)


def kernel(x_prompt, x_sample, cache_a_k, cache_a_v, cache_b_k, cache_b_v, state_gla, c, c_ctx,
           w_mod, b_mod, g_norm, w_ffn_gate, w_ffn_up, w_ffn_down, w_attn_in, w_attn_out,
           g_qnorm, g_knorm, na_rel_bias, w_gla_in, w_gla_gup, b_gla_gup, g_gla_norm, w_gla_out, g_final):
    x = (x_prompt.reshape(N_CTX, D_MODEL), x_sample.reshape(N_LAT, D_MODEL))
    cond = jnp.concatenate([c_ctx[None, :], c, jnp.zeros((COND_ROWS - 1 - DEC_BATCH, D_MODEL), F32)], axis=0)
    mod = _modulation(cond, w_mod, b_mod)

    wg_all, wu_all, wd_all = (w.astype(BF16) for w in (w_ffn_gate, w_ffn_up, w_ffn_down))
    rope_tabs = _rope_tables()
    n_attn = w_attn_in.shape[0]
    ck_a = cache_a_k.reshape(DEC_BATCH, n_attn, PAST_LEN, A_KV_HEADS * HEAD_DIM)
    cv_a = cache_a_v.reshape(DEC_BATCH, n_attn, PAST_LEN, A_KV_HEADS * HEAD_DIM)
    ck_b = cache_b_k.reshape(DEC_BATCH, n_attn, PAST_LEN, B_HEADS * HEAD_DIM)
    cv_b = cache_b_v.reshape(DEC_BATCH, n_attn, PAST_LEN, B_HEADS * HEAD_DIM)

    caches = [jnp.zeros((BATCH, n_attn, SEQ, ATTN_WIDTHS[s]), F32) for s in (1, 2, 4, 5)]
    new_gla = jnp.zeros((BATCH, w_gla_in.shape[0], 2, GLA_HEADS, GLA_DK, GLA_DV), F32)
    for l in range(DEPTH):
        x = _ffn(x, mod, l, 0, g_norm[l, 0], wg_all, wu_all, wd_all)
        if l % 2 == 0:
            i = l // 2
            gq = jnp.tile(g_qnorm[i], 2).reshape(1, PAIR)
            gk = jnp.tile(g_knorm[i], 2).reshape(1, PAIR)
            n_qa = A_HEADS * HEAD_DIM
            w_in = w_attn_in[i].at[:, :n_qa].set(_permute_heads(w_attn_in[i][:, :n_qa], axis=1)).astype(BF16)
            w_out = w_attn_out[i].at[:n_qa].set(_permute_heads(w_attn_out[i][:n_qa], axis=0)).astype(BF16)
            qa_c, ka_c, va_c, qb_c, kb_c, vb_c, *caches = _attn_in(
                x, mod, l, g_norm[l, 1], w_in, gq, gk, lat=False, layer_i=i, caches=caches)
            qa_l, ka_l, va_l, qb_l, kb_l, vb_l = _attn_in(
                x, mod, l, g_norm[l, 1], w_in, gq, gk, lat=True, rope_tabs=rope_tabs)
            oa_c = _attention(qa_c, ka_c, va_c, lat=False, tq=SEQ, n_seq=CTX_SEQS_A)
            ob_c = _attention(qb_c, kb_c, vb_c, lat=False, tq=SEQ, n_seq=CTX_SEQS_B)
            oa_l = _attention(qa_l, ka_l, va_l, lat=True, tq=TQ_LAT_A, layer_i=i, cache_k=ck_a, cache_v=cv_a)
            ob_l = _attention(qb_l, kb_l, vb_l, lat=True, tq=TQ_LAT_B, layer_i=i, cache_k=ck_b, cache_v=cv_b,
                              bias_tab=_na_bias_table(na_rel_bias[i]))
            x = _mixer_out(_attn_out_kernel, "attn_out", x, mod, l, [(oa_c, oa_l), (ob_c, ob_l)], [], [w_out])
        else:
            j = l // 2
            w_in = jnp.pad(w_gla_in[j], ((0, 0), (0, GLA_D_PAD - 2 * GLA_RANK))).astype(BF16)
            w_gup = jnp.zeros((GLA_D_PAD, 2 * GLA_KDIM), F32)
            w_gup = w_gup.at[:GLA_RANK, :GLA_KDIM].set(w_gla_gup[j, 0])
            w_gup = w_gup.at[GLA_RANK:2 * GLA_RANK, GLA_KDIM:].set(w_gla_gup[j, 1]).astype(BF16)
            b_gup = b_gla_gup[j].reshape(1, 2 * GLA_KDIM)
            q, k, v, gt, lgf, lgb = _gla_in(x, mod, l, g_norm[l, 1], w_in, w_gup, b_gup)
            o_c, new_gla = _gla(q, k, v, lgf, lgb, lat=False, layer_j=j, new_state=new_gla)
            (o_l,) = _gla(q, k, v, lgf, lgb, lat=True, layer_j=j, state=state_gla)
            x = _mixer_out(_gla_out_kernel, "gla_out", x, mod, l, [(o_c, o_l)], [gt],
                           [g_gla_norm[j].reshape(1, GLA_DV), w_gla_out[j].astype(BF16)])
        x = _ffn(x, mod, l, 2, g_norm[l, 2], wg_all, wu_all, wd_all)

    y_ctx = _final_norm(x, g_final, 0, N_CTX)
    y_lat = _final_norm(x, g_final, N_CTX, N_LAT)
    return (y_ctx.reshape(BATCH, SEQ, D_MODEL), y_lat.reshape(DEC_BATCH, DEC_SEQ, D_MODEL),
            caches[0].reshape(BATCH, n_attn, SEQ, A_KV_HEADS, HEAD_DIM),
            caches[1].reshape(BATCH, n_attn, SEQ, A_KV_HEADS, HEAD_DIM),
            caches[2].reshape(BATCH, n_attn, SEQ, B_HEADS, HEAD_DIM),
            caches[3].reshape(BATCH, n_attn, SEQ, B_HEADS, HEAD_DIM), new_gla)
```

```python
import functools

import numpy as np
import jax
import jax.numpy as jnp
from jax import lax
from jax.experimental import pallas as pl
from jax.experimental.pallas import tpu as pltpu

D_MODEL = 1024
BATCH = 32
SEQ = 256
DEPTH = 4
DEC_BATCH = 8
DEC_SEQ = 1024
PAST_LEN = 512
GRID_W = 64
HEAD_DIM = 64
A_HEADS = 8
A_KV_HEADS = 4
B_HEADS = 8
ROPE_THETA = 10000.0
NA_WIN_H = 8
NA_WIN_W = 16
GLA_HEADS = 4
GLA_DK = D_MODEL // 2 // GLA_HEADS
GLA_DV = D_MODEL // GLA_HEADS
GLA_KDIM = GLA_HEADS * GLA_DK
GLA_VDIM = GLA_HEADS * GLA_DV
GLA_RANK = 16
GLA_GATE_NORM = 16.0
GLA_CHUNK = 64
D_FF = ((8 * D_MODEL // 3 + 127) // 128) * 128
FFN_RES = 0.5
N_MOD = 9
EPS = 1e-6

N_CTX = BATCH * SEQ
N_LAT = DEC_BATCH * DEC_SEQ
N_TOK = N_CTX + N_LAT
COND_ROWS = 16
LANES = 128
PAIR = 2 * HEAD_DIM
GLA_D_PAD = LANES
VMEM_LIMIT = 56 * 1024 * 1024

F32 = jnp.float32
BF16 = jnp.bfloat16

TM_FFN = 1024
TF_FFN = 256
FFN_CHUNKS = D_FF // TF_FFN
FFN_ROW_BLOCK = 256
assert FFN_CHUNKS * TF_FFN == D_FF and FFN_CHUNKS % 2 == 1
TM_PROJ = 512
TM_GLA_IN = 1024
TQ_LAT_A = 256
TQ_LAT_B = 512
CTX_SEQS_A = 2
CTX_SEQS_B = 4
ATTN_SUB_ROWS = 256
GLA_UNROLL = 8


def _cparams(sem):
    return pltpu.CompilerParams(dimension_semantics=sem, vmem_limit_bytes=VMEM_LIMIT)


def _cond_row(i, tm):
    nct = N_CTX // tm
    per_batch = DEC_SEQ // tm
    return jnp.where(i < nct, 0, 1 + (i - nct) // per_batch)


def _rms(x, g):
    ms = jnp.mean(x * x, axis=-1, keepdims=True)
    return x * lax.rsqrt(ms + EPS) * g


def _dot(a, b):
    return jnp.dot(a, b, preferred_element_type=F32)


def _dot_nt(a, b):
    return lax.dot_general(a, b, (((1,), (1,)), ((), ())), preferred_element_type=F32)


def _dot_tn(a, b):
    return lax.dot_general(a, b, (((0,), (0,)), ((), ())), preferred_element_type=F32)


def _split_bf16(x):
    hi = x.astype(BF16)
    return hi, (x - hi.astype(F32)).astype(BF16)


def _mod_kernel(c_ref, w_ref, b_ref, o_ref):
    c = c_ref[...]
    s_hi, s_lo = _split_bf16(c * jax.nn.sigmoid(c))
    w_hi, w_lo = _split_bf16(w_ref[...])
    o_ref[...] = _dot(s_hi, w_hi) + (_dot(s_lo, w_hi) + _dot(s_hi, w_lo)) + b_ref[...]


def _modulation(cond, w_mod, b_mod):
    tn = D_MODEL
    n_mod = N_MOD * D_MODEL
    out = pl.pallas_call(
        _mod_kernel,
        grid=(DEPTH, n_mod // tn),
        in_specs=[
            pl.BlockSpec((COND_ROWS, D_MODEL), lambda l, j: (0, 0)),
            pl.BlockSpec((None, D_MODEL, tn), lambda l, j: (l, 0, j)),
            pl.BlockSpec((None, 1, tn), lambda l, j: (l, 0, j)),
        ],
        out_specs=pl.BlockSpec((None, COND_ROWS, tn), lambda l, j: (l, 0, j)),
        out_shape=jax.ShapeDtypeStruct((DEPTH, COND_ROWS, n_mod), F32),
        compiler_params=_cparams(("parallel", "parallel")),
        name="adaln_mod",
    )(cond, w_mod, b_mod.reshape(DEPTH, 1, n_mod))
    return out.reshape(DEPTH, COND_ROWS, 3, 3, D_MODEL)


def _ffn_kernel(split_x, *refs):
    if split_x:
        xc_ref, xl_ref, *refs = refs
    else:
        x_ref, *refs = refs
    mod_ref, g_ref, wg_ref, wu_ref, wd_ref, o_ref, h_ref, acc_ref, g_scr, u_scr = refs
    tm = o_ref.shape[0]

    def x_rows(rs):
        if split_x:
            return jnp.where(pl.program_id(0) < N_CTX // tm, xc_ref[rs, :], xl_ref[rs, :])
        return x_ref[rs, :]

    def cols(j):
        return pl.ds(pl.multiple_of(j * TF_FFN, TF_FFN), TF_FFN)

    def gate_up(h, j):
        return _dot(h, wg_ref[:, cols(j)]), _dot(h, wu_ref[:, cols(j)])

    def down(g, u, j):
        a = (g * jax.nn.sigmoid(g) * u).astype(BF16)
        return _dot(a, wd_ref[cols(j), :])

    row_blocks = [slice(r, r + FFN_ROW_BLOCK) for r in range(0, tm, FFN_ROW_BLOCK)]
    gain = g_ref[...] * (1.0 + mod_ref[1:2, :])
    for rs in row_blocks:
        x = x_rows(rs)
        ms = jnp.mean(x * x, axis=-1, keepdims=True)
        h = (x * lax.rsqrt(ms + EPS) * gain + mod_ref[0:1, :]).astype(BF16)
        h_ref[rs, :] = h
        g_scr[rs, :], u_scr[rs, :] = gate_up(h, 0)
    acc_ref[...] = jnp.zeros_like(acc_ref)

    def chunk_pair(jj, carry):
        j = 2 * jj
        h = h_ref[...]
        g1, u1 = gate_up(h, j + 1)
        acc_ref[...] += down(g_scr[...], u_scr[...], j)
        g2, u2 = gate_up(h, j + 2)
        acc_ref[...] += down(g1, u1, j + 1)
        g_scr[...], u_scr[...] = g2, u2
        return carry

    lax.fori_loop(0, FFN_CHUNKS // 2, chunk_pair, None, unroll=True)
    res_gate = FFN_RES * mod_ref[2:3, :]
    for rs in row_blocks:
        y = acc_ref[rs, :] + down(g_scr[rs, :], u_scr[rs, :], FFN_CHUNKS - 1)
        o_ref[rs, :] = x_rows(rs) + res_gate * y


def _ffn(x, mod, layer, sub, g, wg, wu, wd):
    tm = TM_FFN
    nct = N_CTX // tm
    which = sub // 2
    resident = lambda w: pl.BlockSpec((None, None) + w.shape[2:], lambda i: (layer, which, 0, 0),
                                      pipeline_mode=pl.Buffered(1))
    split_x = isinstance(x, tuple)
    if split_x:
        x_specs = [pl.BlockSpec((tm, D_MODEL), lambda i: (jnp.minimum(i, nct - 1), 0)),
                   pl.BlockSpec((tm, D_MODEL), lambda i: (jnp.maximum(i - nct, 0), 0))]
        xs = list(x)
    else:
        x_specs = [pl.BlockSpec((tm, D_MODEL), lambda i: (i, 0))]
        xs = [x]
    return pl.pallas_call(
        functools.partial(_ffn_kernel, split_x),
        grid=(N_TOK // tm,),
        in_specs=x_specs + [
            pl.BlockSpec((None, None, None, 3, D_MODEL), lambda i: (layer, _cond_row(i, tm), sub, 0, 0)),
            pl.BlockSpec((1, D_MODEL), lambda i: (0, 0)),
            resident(wg), resident(wu), resident(wd),
        ],
        out_specs=pl.BlockSpec((tm, D_MODEL), lambda i: (i, 0)),
        out_shape=jax.ShapeDtypeStruct((N_TOK, D_MODEL), F32),
        scratch_shapes=[pltpu.VMEM((tm, D_MODEL), BF16), pltpu.VMEM((tm, D_MODEL), F32),
                        pltpu.VMEM((tm, TF_FFN), F32), pltpu.VMEM((tm, TF_FFN), F32)],
        compiler_params=_cparams(("parallel",)),
        name="ffn",
    )(*xs, mod, g.reshape(1, D_MODEL), wg, wu, wd)


def _mixer_h(x_ref, mod_ref, g_ref):
    y = _rms(x_ref[...], g_ref[...])
    return (y * (1.0 + mod_ref[1:2, :]) + mod_ref[0:1, :]).astype(BF16)


ATTN_WIDTHS = (A_HEADS * HEAD_DIM, A_KV_HEADS * HEAD_DIM, A_KV_HEADS * HEAD_DIM,
               B_HEADS * HEAD_DIM, B_HEADS * HEAD_DIM, B_HEADS * HEAD_DIM)


def _attn_in_kernel(lat, x_ref, mod_ref, g_ref, w_ref, gq_ref, gk_ref, *refs):
    if lat:
        cos_ref, sin_ref, *o_refs = refs
    else:
        o_refs = refs[4:]
    h = _mixer_h(x_ref, mod_ref, g_ref)
    nb = x_ref.shape[0] // SEQ
    first = _head_masks()
    scale = HEAD_DIM ** -0.5
    offs = np.cumsum((0,) + ATTN_WIDTHS)

    def seg(i):
        return _dot(h, w_ref[:, offs[i]:offs[i + 1]])

    def normed_pairs(y, g2):
        for p in range(y.shape[1] // PAIR):
            z = _pair_rms(y[:, p * PAIR:(p + 1) * PAIR], g2, first)
            if lat:
                z = _pair_rope(z, cos_ref[...], sin_ref[...])
            yield slice(p * PAIR, (p + 1) * PAIR), z

    for sl, z in normed_pairs(seg(0), gq_ref[...]):
        o_refs[0][:, sl] = (z * scale).astype(BF16)
    for sl, z in normed_pairs(seg(1), gk_ref[...]):
        o_refs[1][:, sl] = z.astype(BF16)
        if not lat:
            o_refs[6][:, :, sl] = z.reshape(nb, SEQ, PAIR)
    for i in (2, 3, 4, 5):
        y = seg(i)
        o_refs[i][...] = ((y * scale) if i == 3 else y).astype(BF16)
        if not lat and i != 3:
            o_refs[{2: 7, 4: 8, 5: 9}[i]][...] = y.reshape(nb, SEQ, y.shape[1])


def _gla_in_kernel(widths, x_ref, mod_ref, g_ref, w_ref, wgup_ref, bgup_ref, *o_refs):
    h = _mixer_h(x_ref, mod_ref, g_ref)
    off = sum(widths)
    d = _dot(h, w_ref[:, off:off + GLA_D_PAD]).astype(BF16)
    pre = _dot(d, wgup_ref[...]) + bgup_ref[...]
    lg = (jnp.minimum(pre, 0.0) - jnp.log1p(jnp.exp(-jnp.abs(pre)))) * (1.0 / GLA_GATE_NORM)
    o_refs[4][...] = lg[:, :GLA_KDIM]
    o_refs[5][...] = lg[:, GLA_KDIM:]
    off = 0
    for o_ref, wd in zip(o_refs[:4], widths):
        o_ref[...] = _dot(h, w_ref[:, off:off + wd]).astype(o_ref.dtype)
        off += wd


def _mixer_in_specs(layer, tm, n_in, tile0=0):
    return [
        pl.BlockSpec((tm, D_MODEL), lambda i: (tile0 + i, 0)),
        pl.BlockSpec((None, None, None, 3, D_MODEL), lambda i: (layer, _cond_row(tile0 + i, tm), 1, 0, 0)),
        pl.BlockSpec((1, D_MODEL), lambda i: (0, 0)),
        pl.BlockSpec((D_MODEL, n_in), lambda i: (0, 0)),
    ]


def _attn_in(x, mod, layer, g, w_in, gq, gk, *, lat, rope_tabs=None, layer_i=0, caches=None):
    tm = TM_PROJ
    n_rows, tile0 = (N_LAT, N_CTX // tm) if lat else (N_CTX, 0)
    in_specs = _mixer_in_specs(layer, tm, sum(ATTN_WIDTHS), tile0)
    in_specs += [pl.BlockSpec((1, PAIR), lambda i: (0, 0))] * 2
    args = [x, mod, g.reshape(1, D_MODEL), w_in, gq, gk]
    out_specs = [pl.BlockSpec((tm, wd), lambda i: (i, 0)) for wd in ATTN_WIDTHS]
    out_shape = [jax.ShapeDtypeStruct((n_rows, wd), BF16) for wd in ATTN_WIDTHS]
    aliases = {}
    if lat:
        per_seq = DEC_SEQ // tm
        in_specs += [pl.BlockSpec((tm, PAIR), lambda i: (i % per_seq, 0))] * 2
        args += list(rope_tabs)
    else:
        for c in caches:
            aliases[len(args)] = len(out_specs)
            in_specs.append(pl.BlockSpec(memory_space=pl.ANY))
            args.append(c)
            out_specs.append(pl.BlockSpec((tm // SEQ, None, SEQ, c.shape[-1]), lambda i: (i, layer_i, 0, 0)))
            out_shape.append(jax.ShapeDtypeStruct(c.shape, c.dtype))
    return pl.pallas_call(
        functools.partial(_attn_in_kernel, lat),
        grid=(n_rows // tm,),
        in_specs=in_specs,
        out_specs=out_specs,
        out_shape=out_shape,
        input_output_aliases=aliases,
        compiler_params=_cparams(("parallel",)),
        name="attn_in_lat" if lat else "attn_in_ctx",
    )(*args)


def _gla_in(x, mod, layer, g, w_in, w_gup, b_gup):
    tm = TM_GLA_IN
    widths = (GLA_KDIM, GLA_KDIM, GLA_VDIM, GLA_VDIM)
    out_w = widths + (GLA_KDIM, GLA_KDIM)
    out_dt = (F32, F32, BF16, F32, F32, F32)
    n_in = sum(widths) + GLA_D_PAD
    return pl.pallas_call(
        functools.partial(_gla_in_kernel, widths),
        grid=(N_TOK // tm,),
        in_specs=_mixer_in_specs(layer, tm, n_in) + [
            pl.BlockSpec((GLA_D_PAD, 2 * GLA_KDIM), lambda i: (0, 0)),
            pl.BlockSpec((1, 2 * GLA_KDIM), lambda i: (0, 0)),
        ],
        out_specs=[pl.BlockSpec((tm, wd), lambda i: (i, 0)) for wd in out_w],
        out_shape=[jax.ShapeDtypeStruct((N_TOK, wd), dt) for wd, dt in zip(out_w, out_dt)],
        compiler_params=_cparams(("parallel",)),
        name="gla_in",
    )(x, mod, g.reshape(1, D_MODEL), w_in, w_gup, b_gup)


def _head_masks():
    lane = lax.broadcasted_iota(jnp.int32, (1, PAIR), 1)
    return lane < HEAD_DIM


def _pair_rms(x, g2, first):
    sq = x * x
    s0 = jnp.sum(jnp.where(first, sq, 0.0), axis=-1, keepdims=True)
    s1 = jnp.sum(jnp.where(first, 0.0, sq), axis=-1, keepdims=True)
    ms = jnp.where(first, s0, s1) * (1.0 / HEAD_DIM)
    return x * lax.rsqrt(ms + EPS) * g2


def _pair_rope(x, cos, sin_signed):
    lane = lax.broadcasted_iota(jnp.int32, x.shape, 1)
    quarter = HEAD_DIM // 4
    partner = jnp.where(lane % (2 * quarter) < quarter,
                        pltpu.roll(x, PAIR - quarter, 1), pltpu.roll(x, quarter, 1))
    return x * cos + partner * sin_signed


def _na_key_rows(tq):
    rows = DEC_SEQ // GRID_W
    wh = min(NA_WIN_H, rows)
    r = tq // GRID_W
    start = lambda qr: max(0, min(qr - wh // 2, rows - wh))
    lo = [start(qi * r) for qi in range(rows // r)]
    hi = [start(qi * r + r - 1) + wh for qi in range(rows // r)]
    span = max(h - l for l, h in zip(lo, hi))
    span += span % 2
    return span, [min(l, rows - span) for l in lo]


def _fill_na_bias(tab_ref, bias_scr, qi, tq, first, low, span):
    rows = DEC_SEQ // GRID_W
    wh = min(NA_WIN_H, rows)
    neg = -jnp.inf
    for a in range(tq // GRID_W):
        qr = qi * (tq // GRID_W) + a
        rs = jnp.clip(qr - wh // 2, 0, rows - wh)
        for m in range(span // 2):
            kr0, kr1 = low + 2 * m, low + 2 * m + 1
            ok0 = jnp.logical_and(kr0 >= rs, kr0 < rs + wh)
            ok1 = jnp.logical_and(kr1 >= rs, kr1 < rs + wh)
            d0 = jnp.clip(kr0 - qr + NA_WIN_H - 1, 0, 2 * NA_WIN_H - 2)
            d1 = jnp.clip(kr1 - qr + NA_WIN_H - 1, 0, 2 * NA_WIN_H - 2)
            for hh in range(2):
                t0 = jnp.where(ok0, tab_ref[hh, d0], neg)
                t1 = jnp.where(ok1, tab_ref[hh, d1], neg)
                bias_scr[hh * tq + a * GRID_W:hh * tq + (a + 1) * GRID_W, m * PAIR:(m + 1) * PAIR] = (
                    jnp.where(first, t0, t1))


def _attn_kernel(has_ctx, has_bias, n_seq, *refs):
    it = iter(refs)
    q_ref, k_ref, v_ref = next(it), next(it), next(it)
    kc_ref, vc_ref = (next(it), next(it)) if has_ctx else (None, None)
    tab_ref = next(it) if has_bias else None
    o_ref = next(it)
    bias_scr = next(it) if has_bias else None

    first = _head_masks()
    tq = q_ref.shape[0] // n_seq
    tk = k_ref.shape[0] // n_seq
    n_half = q_ref.shape[1] // PAIR
    rows = 2 * n_half * tq
    sub = min(rows, ATTN_SUB_ROWS)

    if has_bias:
        span, lows = _na_key_rows(tq)
        qi = pl.program_id(1)
        low = sum(jnp.where(qi == n, l, 0) for n, l in enumerate(lows))
        key_rows = pl.ds(pl.multiple_of(low * GRID_W, GRID_W), span * GRID_W)

        @pl.when(pl.program_id(2) == 0)
        def _():
            _fill_na_bias(tab_ref, bias_scr, qi, tq, first, low, span)

    zero = jnp.zeros((tq, PAIR), BF16)
    qs, ks, vs = [], [], []
    for sq in range(n_seq):
        stacked = []
        for j in range(n_half):
            qj = q_ref[sq * tq:(sq + 1) * tq, j * PAIR:(j + 1) * PAIR]
            stacked += [jnp.where(first, qj, zero), jnp.where(first, zero, qj)]
        qs.append(jnp.concatenate(stacked, axis=0))
        if has_bias:
            ks.append(k_ref[key_rows, :])
            vs.append(v_ref[key_rows, :])
        else:
            ks.append(k_ref[sq * tk:(sq + 1) * tk, :])
            vs.append(v_ref[sq * tk:(sq + 1) * tk, :])
    kc = kc_ref[...].astype(BF16) if has_ctx else None
    vc = vc_ref[...].astype(BF16) if has_ctx else None

    def scores(sq, i):
        qi = qs[sq][i * sub:(i + 1) * sub]
        return _dot_nt(qi, ks[sq]), (_dot_nt(qi, kc) if has_ctx else None)

    def finish(sq, i, s, sc):
        if has_bias:
            s = s + bias_scr[i * sub:(i + 1) * sub, :]
        m = jnp.max(s, axis=-1, keepdims=True)
        if has_ctx:
            m = jnp.maximum(m, jnp.max(sc, axis=-1, keepdims=True))
        p = jnp.exp(s - m)
        l = jnp.sum(p, axis=-1, keepdims=True)
        o = _dot(p.astype(BF16), vs[sq])
        if has_ctx:
            pc = jnp.exp(sc - m)
            l = l + jnp.sum(pc, axis=-1, keepdims=True)
            o = o + _dot(pc.astype(BF16), vc)
        return o / l

    groups = [(sq, i) for sq in range(n_seq) for i in range(rows // sub)]
    outs = [[] for _ in range(n_seq)]
    cur = scores(*groups[0])
    for n, (sq, i) in enumerate(groups):
        nxt = scores(*groups[n + 1]) if n + 1 < len(groups) else None
        outs[sq].append(finish(sq, i, *cur))
        cur = nxt
    for sq in range(n_seq):
        o = jnp.concatenate(outs[sq], axis=0) if len(outs[sq]) > 1 else outs[sq][0]
        for j in range(n_half):
            o_ref[sq * tq:(sq + 1) * tq, j * PAIR:(j + 1) * PAIR] = jnp.where(
                first, o[2 * j * tq:(2 * j + 1) * tq], o[(2 * j + 1) * tq:(2 * j + 2) * tq]).astype(BF16)


def _attention(q, k, v, *, lat, tq, n_seq=1, layer_i=0, cache_k=None, cache_v=None, bias_tab=None):
    has_bias = bias_tab is not None
    n_pairs = k.shape[1] // PAIR
    n_half = q.shape[1] // k.shape[1]
    nb, t = (DEC_BATCH, DEC_SEQ) if lat else (BATCH, SEQ)
    nq = t // tq
    assert n_seq == 1 or (nq == 1 and not lat)
    n_keys = _na_key_rows(tq)[0] * GRID_W if has_bias else t
    in_specs = [
        pl.BlockSpec((n_seq * tq, n_half * PAIR), lambda p, qi, b: (b * nq + qi, p)),
        pl.BlockSpec((n_seq * t, PAIR), lambda p, qi, b: (b, p)),
        pl.BlockSpec((n_seq * t, PAIR), lambda p, qi, b: (b, p)),
    ]
    args = [q, k, v]
    if lat:
        in_specs += [pl.BlockSpec((None, None, PAST_LEN, PAIR), lambda p, qi, b: (b, layer_i, 0, p))] * 2
        args += [cache_k, cache_v]
    if has_bias:
        in_specs.append(pl.BlockSpec((2,) + bias_tab.shape[1:], lambda p, qi, b: (p, 0, 0, 0)))
        args.append(bias_tab)
    return pl.pallas_call(
        functools.partial(_attn_kernel, lat, has_bias, n_seq),
        grid=(n_pairs, nq, nb // n_seq),
        in_specs=in_specs,
        out_specs=pl.BlockSpec((n_seq * tq, n_half * PAIR), lambda p, qi, b: (b * nq + qi, p)),
        out_shape=jax.ShapeDtypeStruct(q.shape, BF16),
        scratch_shapes=[pltpu.VMEM((2 * tq, n_keys), F32)] if has_bias else [],
        compiler_params=_cparams(("parallel", "parallel", "arbitrary")),
        name="attn_" + ("lat" if lat else "ctx") + ("_a" if n_half == 2 else "_b"),
    )(*args)


def _rope_tables():
    half = HEAD_DIM // 2
    freqs = ROPE_THETA ** (-np.arange(0, half, 2, dtype=np.float32) / half)
    t = np.arange(DEC_SEQ)

    def tab(pos):
        ang = pos.astype(np.float32)[:, None] * freqs[None, :]
        cos = np.concatenate([np.cos(ang), np.cos(ang)], -1)
        sin = np.concatenate([-np.sin(ang), np.sin(ang)], -1)
        return cos, sin

    cr, sr = tab(t // GRID_W)
    cc, sc = tab(t % GRID_W)
    cos = np.concatenate([cr, cc] * 2, -1).astype(np.float32)
    sin = np.concatenate([sr, sc] * 2, -1).astype(np.float32)
    return jnp.asarray(cos), jnp.asarray(sin)


def _na_bias_table(rel_bias):
    w = GRID_W
    n_rel = 2 * NA_WIN_W - 1
    left = w - NA_WIN_W
    rp = jnp.pad(rel_bias, ((0, 0), (0, 0), (left, 2 * w - n_rel - left)))
    flat = jnp.tile(rp, (1, 1, w))
    m = flat[..., :w * (2 * w - 1)].reshape(rel_bias.shape[0], rel_bias.shape[1], w, 2 * w - 1)[..., w - 1:]
    c = np.arange(w)
    ws = np.clip(c - NA_WIN_W // 2, 0, w - NA_WIN_W)
    in_col = (c[None, :] >= ws[:, None]) & (c[None, :] < ws[:, None] + NA_WIN_W)
    m = jnp.where(jnp.asarray(in_col)[None, None], m, -jnp.inf)
    return jnp.concatenate([m, m], axis=-1)


def _chunk_scan(x, reverse):
    t = x.shape[0]
    pos = lax.broadcasted_iota(jnp.int32, x.shape, 0) % GLA_CHUNK
    s = 1
    while s < GLA_CHUNK:
        if reverse:
            x = x + jnp.where(pos < GLA_CHUNK - s, pltpu.roll(x, t - s, 0), 0.0)
        else:
            x = x + jnp.where(pos >= s, pltpu.roll(x, s, 0), 0.0)
        s *= 2
    return x


def _gla_kernel(has_s0, emit_s, heads, *refs):
    it = iter(refs)
    q_ref, k_ref, v_ref, lgf_ref, lgb_ref = (next(it) for _ in range(5))
    s0_ref = next(it) if has_s0 else None
    if emit_s:
        next(it)
    o_ref = next(it)
    sf_ref = next(it) if emit_s else None
    qf_scr, qb_scr, kin_scr, kout_scr, bf_scr, bb_scr, a_scr, u_scr, st_scr = (next(it) for _ in range(9))

    c_len, dk, dv = GLA_CHUNK, GLA_DK, GLA_DV
    t = q_ref.shape[0]
    n = t // c_len
    scale = dk ** -0.5
    row = lax.broadcasted_iota(jnp.int32, (c_len, c_len), 0)
    col = lax.broadcasted_iota(jnp.int32, (c_len, c_len), 1)
    zero = jnp.zeros((t, dk), BF16)

    def rows(c):
        return pl.ds(pl.multiple_of(c * c_len, c_len), c_len)

    for hd in range(heads):
        kl = slice(hd * dk, (hd + 1) * dk)
        vl = slice(hd * dv, (hd + 1) * dv)

        lgf, lgb = lgf_ref[:, kl], lgb_ref[:, kl]
        bf = _chunk_scan(lgf, False)
        bb = _chunk_scan(lgb, True)
        q = q_ref[:, kl] * scale
        k = k_ref[:, kl]
        qf_scr[:, :dk] = (q * jnp.exp(bf)).astype(BF16)
        qf_scr[:, dk:] = zero
        qb_scr[:, :dk] = zero
        qb_scr[:, dk:] = (q * jnp.exp(bb)).astype(BF16)
        kin_scr[:, :dk] = (k * jnp.exp(-bf)).astype(BF16)
        kin_scr[:, dk:] = (k * jnp.exp(-bb)).astype(BF16)
        bf3, bb3 = bf.reshape(n, c_len, dk), bb.reshape(n, c_len, dk)
        kout_scr[:, :dk] = (k * jnp.exp((bf3[:, c_len - 1:, :] - bf3).reshape(t, dk))).astype(BF16)
        kout_scr[:, dk:] = (k * jnp.exp((bb3[:, :1, :] - bb3).reshape(t, dk))).astype(BF16)
        bf_scr[...] = bf
        bb_scr[...] = bb

        def score_pass(c, _):
            r = rows(c)
            a2 = _dot_nt(jnp.concatenate([qf_scr[r, :], qb_scr[r, :]], axis=0), kin_scr[r, :])
            a = jnp.where(col <= row, a2[:c_len], 0.0) + jnp.where(col >= row, a2[c_len:], 0.0)
            a_scr[r, :] = a.astype(BF16)
            u = _dot_tn(v_ref[r, vl], kout_scr[r, :])
            u_scr[c, :, :dk] = u[:, :dk]
            u_scr[n - 1 - c, :, dk:] = u[:, dk:]
            return _

        def value_pass(c, _):
            r = rows(c)
            o_ref[r, vl] = _dot(a_scr[r, :], v_ref[r, vl])
            return _

        lax.fori_loop(0, n, score_pass, None, unroll=GLA_UNROLL)
        lax.fori_loop(0, n, value_pass, None, unroll=GLA_UNROLL)

        if has_s0:
            st_scr[:, :dk] = s0_ref[0, hd].T
            st_scr[:, dk:] = s0_ref[1, hd].T
        else:
            st_scr[...] = jnp.zeros_like(st_scr)

        def sweep(i, _):
            rf, rb = rows(i), rows(n - 1 - i)
            st = st_scr[...]
            res = _dot_nt(jnp.concatenate([qf_scr[rf, :], qb_scr[rb, :]], axis=0), st.astype(BF16))
            o_ref[rf, vl] += res[:c_len]
            o_ref[rb, vl] += res[c_len:]
            b_edge = jnp.concatenate([bf_scr[pl.ds(i * c_len + c_len - 1, 1), :],
                                      bb_scr[pl.ds((n - 1 - i) * c_len, 1), :]], axis=1)
            st_scr[...] = st * jnp.exp(b_edge) + u_scr[i]
            return _

        lax.fori_loop(0, n, sweep, None, unroll=GLA_UNROLL)

        if emit_s:
            sf_ref[0, hd] = st_scr[:, :dk].T
            sf_ref[1, hd] = st_scr[:, dk:].T


def _gla(q, k, v, lgf, lgb, *, lat, heads, layer_j=0, state=None, new_state=None):
    if lat:
        nb, t, row0 = DEC_BATCH, DEC_SEQ, N_CTX
    else:
        nb, t, row0 = BATCH, SEQ, 0
    off = row0 // t
    kspec = pl.BlockSpec((t, heads * GLA_DK), lambda b, h: (off + b, h))
    in_specs = [kspec, kspec, pl.BlockSpec((t, heads * GLA_DV), lambda b, h: (off + b, h)), kspec, kspec]
    args = [q, k, v, lgf, lgb]
    state_spec = pl.BlockSpec((None, None, 2, heads, GLA_DK, GLA_DV), lambda b, h: (b, layer_j, 0, h, 0, 0))
    if lat:
        in_specs.append(state_spec)
        args.append(state)
    out_specs = [pl.BlockSpec((t, heads * GLA_DV), lambda b, h: (b, h))]
    out_shape = [jax.ShapeDtypeStruct((nb * t, GLA_VDIM), F32)]
    aliases = {}
    if not lat:
        aliases[len(args)] = 1
        in_specs.append(pl.BlockSpec(memory_space=pl.ANY))
        args.append(new_state)
        out_specs.append(state_spec)
        out_shape.append(jax.ShapeDtypeStruct(new_state.shape, new_state.dtype))
    return pl.pallas_call(
        functools.partial(_gla_kernel, lat, not lat, heads),
        grid=(nb, GLA_HEADS // heads),
        in_specs=in_specs,
        out_specs=out_specs,
        out_shape=out_shape,
        input_output_aliases=aliases,
        scratch_shapes=[
            pltpu.VMEM((t, 2 * GLA_DK), BF16),
            pltpu.VMEM((t, 2 * GLA_DK), BF16),
            pltpu.VMEM((t, 2 * GLA_DK), BF16),
            pltpu.VMEM((t, 2 * GLA_DK), BF16),
            pltpu.VMEM((t, GLA_DK), F32),
            pltpu.VMEM((t, GLA_DK), F32),
            pltpu.VMEM((t, GLA_CHUNK), BF16),
            pltpu.VMEM((t // GLA_CHUNK, GLA_DV, 2 * GLA_DK), F32),
            pltpu.VMEM((GLA_DV, 2 * GLA_DK), F32),
        ],
        compiler_params=_cparams(("parallel", "parallel")),
        name="gla_lat" if lat else "gla_ctx",
    )(*args)


def _group_pick(tm, c_ref, l_ref):
    return jnp.where(pl.program_id(0) < N_CTX // tm, c_ref[...], l_ref[...])


def _attn_out_kernel(tm, x_ref, mod_ref, oac_ref, oal_ref, obc_ref, obl_ref, w_ref, o_ref):
    oa = _group_pick(tm, oac_ref, oal_ref)
    ob = _group_pick(tm, obc_ref, obl_ref)
    na = oa.shape[1]
    y = _dot(oa, w_ref[:na, :]) + _dot(ob, w_ref[na:, :])
    o_ref[...] = x_ref[...] + mod_ref[2:3, :] * y


def _gla_out_kernel(tm, x_ref, mod_ref, oc_ref, ol_ref, gate_ref, gn_ref, w_ref, o_ref):
    gn = gn_ref[...]
    o_pre = _group_pick(tm, oc_ref, ol_ref)
    parts = []
    for h in range(GLA_HEADS):
        sl = slice(h * GLA_DV, (h + 1) * GLA_DV)
        gt = gate_ref[:, sl]
        parts.append((_rms(o_pre[:, sl], gn) * (gt * jax.nn.sigmoid(gt))).astype(BF16))
    y = _dot(jnp.concatenate(parts, axis=-1), w_ref[...])
    o_ref[...] = x_ref[...] + mod_ref[2:3, :] * y


def _mixer_out(kernel, name, x, mod, layer, group_acts, acts, consts):
    tm = TM_PROJ
    nct = N_CTX // tm
    in_specs = [
        pl.BlockSpec((tm, D_MODEL), lambda i: (i, 0)),
        pl.BlockSpec((None, None, None, 3, D_MODEL), lambda i: (layer, _cond_row(i, tm), 1, 0, 0)),
    ]
    flat = []
    for a_c, a_l in group_acts:
        in_specs.append(pl.BlockSpec((tm, a_c.shape[1]), lambda i: (jnp.minimum(i, nct - 1), 0)))
        in_specs.append(pl.BlockSpec((tm, a_l.shape[1]), lambda i: (jnp.maximum(i - nct, 0), 0)))
        flat += [a_c, a_l]
    in_specs += [pl.BlockSpec((tm, a.shape[1]), lambda i: (i, 0)) for a in acts]
    in_specs += [pl.BlockSpec(c.shape, lambda i: (0, 0)) for c in consts]
    return pl.pallas_call(
        functools.partial(kernel, tm),
        grid=(N_TOK // tm,),
        in_specs=in_specs,
        out_specs=pl.BlockSpec((tm, D_MODEL), lambda i: (i, 0)),
        out_shape=jax.ShapeDtypeStruct((N_TOK, D_MODEL), F32),
        compiler_params=_cparams(("parallel",)),
        name=name,
    )(x, mod, *flat, *acts, *consts)


def _final_kernel(x_ref, g_ref, o_ref):
    o_ref[...] = _rms(x_ref[...], g_ref[...])


def _final_norm(x, g, row0, n_rows):
    tm = TM_PROJ
    off = row0 // tm
    return pl.pallas_call(
        _final_kernel,
        grid=(n_rows // tm,),
        in_specs=[pl.BlockSpec((tm, D_MODEL), lambda i: (off + i, 0)),
                  pl.BlockSpec((1, D_MODEL), lambda i: (0, 0))],
        out_specs=pl.BlockSpec((tm, D_MODEL), lambda i: (i, 0)),
        out_shape=jax.ShapeDtypeStruct((n_rows, D_MODEL), F32),
        compiler_params=_cparams(("parallel",)),
        name="final_norm",
    )(x, g.reshape(1, D_MODEL))


def _permute_heads(w, axis):
    grp = A_HEADS // A_KV_HEADS
    order = [(2 * p + j) * grp + g for p in range(A_KV_HEADS // 2) for g in range(grp) for j in range(2)]
    rest = tuple(np.delete(w.shape, axis))
    w = jnp.moveaxis(w, axis, 0).reshape((A_HEADS, HEAD_DIM) + rest)
    w = w[np.asarray(order)].reshape((A_HEADS * HEAD_DIM,) + rest)
    return jnp.moveaxis(w, 0, axis)


def kernel(x_prompt, x_sample, cache_a_k, cache_a_v, cache_b_k, cache_b_v, state_gla, c, c_ctx,
           w_mod, b_mod, g_norm, w_ffn_gate, w_ffn_up, w_ffn_down, w_attn_in, w_attn_out,
           g_qnorm, g_knorm, na_rel_bias, w_gla_in, w_gla_gup, b_gla_gup, g_gla_norm, w_gla_out, g_final):
    x = (x_prompt.reshape(N_CTX, D_MODEL), x_sample.reshape(N_LAT, D_MODEL))
    cond = jnp.concatenate([c_ctx[None, :], c, jnp.zeros((COND_ROWS - 1 - DEC_BATCH, D_MODEL), F32)], axis=0)
    mod = _modulation(cond, w_mod, b_mod)

    wg_all, wu_all, wd_all = (w.astype(BF16) for w in (w_ffn_gate, w_ffn_up, w_ffn_down))
    rope_tabs = _rope_tables()
    n_attn = w_attn_in.shape[0]
    ck_a = cache_a_k.reshape(DEC_BATCH, n_attn, PAST_LEN, A_KV_HEADS * HEAD_DIM)
    cv_a = cache_a_v.reshape(DEC_BATCH, n_attn, PAST_LEN, A_KV_HEADS * HEAD_DIM)
    ck_b = cache_b_k.reshape(DEC_BATCH, n_attn, PAST_LEN, B_HEADS * HEAD_DIM)
    cv_b = cache_b_v.reshape(DEC_BATCH, n_attn, PAST_LEN, B_HEADS * HEAD_DIM)

    caches = [jnp.zeros((BATCH, n_attn, SEQ, ATTN_WIDTHS[s]), F32) for s in (1, 2, 4, 5)]
    new_gla = jnp.zeros((BATCH, w_gla_in.shape[0], 2, GLA_HEADS, GLA_DK, GLA_DV), F32)
    for l in range(DEPTH):
        x = _ffn(x, mod, l, 0, g_norm[l, 0], wg_all, wu_all, wd_all)
        if l % 2 == 0:
            i = l // 2
            gq = jnp.tile(g_qnorm[i], 2).reshape(1, PAIR)
            gk = jnp.tile(g_knorm[i], 2).reshape(1, PAIR)
            n_qa = A_HEADS * HEAD_DIM
            w_in = w_attn_in[i].at[:, :n_qa].set(_permute_heads(w_attn_in[i][:, :n_qa], axis=1)).astype(BF16)
            w_out = w_attn_out[i].at[:n_qa].set(_permute_heads(w_attn_out[i][:n_qa], axis=0)).astype(BF16)
            qa_c, ka_c, va_c, qb_c, kb_c, vb_c, *caches = _attn_in(
                x, mod, l, g_norm[l, 1], w_in, gq, gk, lat=False, layer_i=i, caches=caches)
            qa_l, ka_l, va_l, qb_l, kb_l, vb_l = _attn_in(
                x, mod, l, g_norm[l, 1], w_in, gq, gk, lat=True, rope_tabs=rope_tabs)
            oa_c = _attention(qa_c, ka_c, va_c, lat=False, tq=SEQ, n_seq=CTX_SEQS_A)
            ob_c = _attention(qb_c, kb_c, vb_c, lat=False, tq=SEQ, n_seq=CTX_SEQS_B)
            oa_l = _attention(qa_l, ka_l, va_l, lat=True, tq=TQ_LAT_A, layer_i=i, cache_k=ck_a, cache_v=cv_a)
            ob_l = _attention(qb_l, kb_l, vb_l, lat=True, tq=TQ_LAT_B, layer_i=i, cache_k=ck_b, cache_v=cv_b,
                              bias_tab=_na_bias_table(na_rel_bias[i]))
            x = _mixer_out(_attn_out_kernel, "attn_out", x, mod, l, [(oa_c, oa_l), (ob_c, ob_l)], [], [w_out])
        else:
            j = l // 2
            w_in = jnp.pad(w_gla_in[j], ((0, 0), (0, GLA_D_PAD - 2 * GLA_RANK))).astype(BF16)
            w_gup = jnp.zeros((GLA_D_PAD, 2 * GLA_KDIM), F32)
            w_gup = w_gup.at[:GLA_RANK, :GLA_KDIM].set(w_gla_gup[j, 0])
            w_gup = w_gup.at[GLA_RANK:2 * GLA_RANK, GLA_KDIM:].set(w_gla_gup[j, 1]).astype(BF16)
            b_gup = b_gla_gup[j].reshape(1, 2 * GLA_KDIM)
            q, k, v, gt, lgf, lgb = _gla_in(x, mod, l, g_norm[l, 1], w_in, w_gup, b_gup)
            o_c, new_gla = _gla(q, k, v, lgf, lgb, lat=False, heads=1, layer_j=j, new_state=new_gla)
            (o_l,) = _gla(q, k, v, lgf, lgb, lat=True, heads=1, layer_j=j, state=state_gla)
            x = _mixer_out(_gla_out_kernel, "gla_out", x, mod, l, [(o_c, o_l)], [gt],
                           [g_gla_norm[j].reshape(1, GLA_DV), w_gla_out[j].astype(BF16)])
        x = _ffn(x, mod, l, 2, g_norm[l, 2], wg_all, wu_all, wd_all)

    y_ctx = _final_norm(x, g_final, 0, N_CTX)
    y_lat = _final_norm(x, g_final, N_CTX, N_LAT)
    return (y_ctx.reshape(BATCH, SEQ, D_MODEL), y_lat.reshape(DEC_BATCH, DEC_SEQ, D_MODEL),
            caches[0].reshape(BATCH, n_attn, SEQ, A_KV_HEADS, HEAD_DIM),
            caches[1].reshape(BATCH, n_attn, SEQ, A_KV_HEADS, HEAD_DIM),
            caches[2].reshape(BATCH, n_attn, SEQ, B_HEADS, HEAD_DIM),
            caches[3].reshape(BATCH, n_attn, SEQ, B_HEADS, HEAD_DIM), new_gla)
```

```python
import functools

import numpy as np
import jax
import jax.numpy as jnp
from jax import lax
from jax.experimental import pallas as pl
from jax.experimental.pallas import tpu as pltpu

D_MODEL = 1024
BATCH = 32
SEQ = 256
DEPTH = 4
DEC_BATCH = 8
DEC_SEQ = 1024
PAST_LEN = 512
GRID_W = 64
HEAD_DIM = 64
A_HEADS = 8
A_KV_HEADS = 4
B_HEADS = 8
ROPE_THETA = 10000.0
NA_WIN_H = 8
NA_WIN_W = 16
GLA_HEADS = 4
GLA_DK = D_MODEL // 2 // GLA_HEADS
GLA_DV = D_MODEL // GLA_HEADS
GLA_KDIM = GLA_HEADS * GLA_DK
GLA_VDIM = GLA_HEADS * GLA_DV
GLA_RANK = 16
GLA_GATE_NORM = 16.0
GLA_CHUNK = 64
D_FF = ((8 * D_MODEL // 3 + 127) // 128) * 128
FFN_RES = 0.5
N_MOD = 9
EPS = 1e-6

N_CTX = BATCH * SEQ
N_LAT = DEC_BATCH * DEC_SEQ
N_TOK = N_CTX + N_LAT
COND_ROWS = 16
LANES = 128
PAIR = 2 * HEAD_DIM
GLA_D_PAD = LANES
VMEM_LIMIT = 56 * 1024 * 1024

F32 = jnp.float32
BF16 = jnp.bfloat16

TM_FFN = 1024
TF_FFN = 256
FFN_CHUNKS = D_FF // TF_FFN
FFN_ROW_BLOCK = 256
assert FFN_CHUNKS * TF_FFN == D_FF and FFN_CHUNKS % 2 == 1
TM_PROJ = 512
TM_GLA_IN = 1024
TQ_LAT_A = 256
TQ_LAT_B = 512
CTX_SEQS_A = 2
CTX_SEQS_B = 4
ATTN_SUB_ROWS = 256
GLA_UNROLL = 8


def _cparams(sem):
    return pltpu.CompilerParams(dimension_semantics=sem, vmem_limit_bytes=VMEM_LIMIT)


def _cond_row(i, tm):
    nct = N_CTX // tm
    per_batch = DEC_SEQ // tm
    return jnp.where(i < nct, 0, 1 + (i - nct) // per_batch)


def _rms(x, g):
    ms = jnp.mean(x * x, axis=-1, keepdims=True)
    return x * lax.rsqrt(ms + EPS) * g


def _dot(a, b):
    return jnp.dot(a, b, preferred_element_type=F32)


def _dot_nt(a, b):
    return lax.dot_general(a, b, (((1,), (1,)), ((), ())), preferred_element_type=F32)


def _dot_tn(a, b):
    return lax.dot_general(a, b, (((0,), (0,)), ((), ())), preferred_element_type=F32)


def _split_bf16(x):
    hi = x.astype(BF16)
    return hi, (x - hi.astype(F32)).astype(BF16)


def _mod_kernel(c_ref, w_ref, b_ref, o_ref):
    c = c_ref[...]
    s_hi, s_lo = _split_bf16(c * jax.nn.sigmoid(c))
    w_hi, w_lo = _split_bf16(w_ref[...])
    o_ref[...] = _dot(s_hi, w_hi) + (_dot(s_lo, w_hi) + _dot(s_hi, w_lo)) + b_ref[...]


def _modulation(cond, w_mod, b_mod):
    tn = D_MODEL
    n_mod = N_MOD * D_MODEL
    out = pl.pallas_call(
        _mod_kernel,
        grid=(DEPTH, n_mod // tn),
        in_specs=[
            pl.BlockSpec((COND_ROWS, D_MODEL), lambda l, j: (0, 0)),
            pl.BlockSpec((None, D_MODEL, tn), lambda l, j: (l, 0, j)),
            pl.BlockSpec((None, 1, tn), lambda l, j: (l, 0, j)),
        ],
        out_specs=pl.BlockSpec((None, COND_ROWS, tn), lambda l, j: (l, 0, j)),
        out_shape=jax.ShapeDtypeStruct((DEPTH, COND_ROWS, n_mod), F32),
        compiler_params=_cparams(("parallel", "parallel")),
        name="adaln_mod",
    )(cond, w_mod, b_mod.reshape(DEPTH, 1, n_mod))
    return out.reshape(DEPTH, COND_ROWS, 3, 3, D_MODEL)


def _ffn_kernel(split_x, *refs):
    if split_x:
        xc_ref, xl_ref, *refs = refs
    else:
        x_ref, *refs = refs
    mod_ref, g_ref, wg_ref, wu_ref, wd_ref, o_ref, h_ref, acc_ref, g_scr, u_scr = refs
    tm = o_ref.shape[0]

    def x_rows(rs):
        if split_x:
            return jnp.where(pl.program_id(0) < N_CTX // tm, xc_ref[rs, :], xl_ref[rs, :])
        return x_ref[rs, :]

    def cols(j):
        return pl.ds(pl.multiple_of(j * TF_FFN, TF_FFN), TF_FFN)

    def gate_up(h, j):
        return _dot(h, wg_ref[:, cols(j)]), _dot(h, wu_ref[:, cols(j)])

    def down(g, u, j):
        a = (g * jax.nn.sigmoid(g) * u).astype(BF16)
        return _dot(a, wd_ref[cols(j), :])

    row_blocks = [slice(r, r + FFN_ROW_BLOCK) for r in range(0, tm, FFN_ROW_BLOCK)]
    gain = g_ref[...] * (1.0 + mod_ref[1:2, :])
    for rs in row_blocks:
        x = x_rows(rs)
        ms = jnp.mean(x * x, axis=-1, keepdims=True)
        h = (x * lax.rsqrt(ms + EPS) * gain + mod_ref[0:1, :]).astype(BF16)
        h_ref[rs, :] = h
        g_scr[rs, :], u_scr[rs, :] = gate_up(h, 0)
    acc_ref[...] = jnp.zeros_like(acc_ref)

    def chunk_pair(jj, carry):
        j = 2 * jj
        h = h_ref[...]
        g1, u1 = gate_up(h, j + 1)
        acc_ref[...] += down(g_scr[...], u_scr[...], j)
        g2, u2 = gate_up(h, j + 2)
        acc_ref[...] += down(g1, u1, j + 1)
        g_scr[...], u_scr[...] = g2, u2
        return carry

    lax.fori_loop(0, FFN_CHUNKS // 2, chunk_pair, None, unroll=True)
    res_gate = FFN_RES * mod_ref[2:3, :]
    for rs in row_blocks:
        y = acc_ref[rs, :] + down(g_scr[rs, :], u_scr[rs, :], FFN_CHUNKS - 1)
        o_ref[rs, :] = x_rows(rs) + res_gate * y


def _ffn(x, mod, layer, sub, g, wg, wu, wd):
    tm = TM_FFN
    nct = N_CTX // tm
    which = sub // 2
    resident = lambda w: pl.BlockSpec((None, None) + w.shape[2:], lambda i: (layer, which, 0, 0),
                                      pipeline_mode=pl.Buffered(1))
    split_x = isinstance(x, tuple)
    if split_x:
        x_specs = [pl.BlockSpec((tm, D_MODEL), lambda i: (jnp.minimum(i, nct - 1), 0)),
                   pl.BlockSpec((tm, D_MODEL), lambda i: (jnp.maximum(i - nct, 0), 0))]
        xs = list(x)
    else:
        x_specs = [pl.BlockSpec((tm, D_MODEL), lambda i: (i, 0))]
        xs = [x]
    return pl.pallas_call(
        functools.partial(_ffn_kernel, split_x),
        grid=(N_TOK // tm,),
        in_specs=x_specs + [
            pl.BlockSpec((None, None, None, 3, D_MODEL), lambda i: (layer, _cond_row(i, tm), sub, 0, 0)),
            pl.BlockSpec((1, D_MODEL), lambda i: (0, 0)),
            resident(wg), resident(wu), resident(wd),
        ],
        out_specs=pl.BlockSpec((tm, D_MODEL), lambda i: (i, 0)),
        out_shape=jax.ShapeDtypeStruct((N_TOK, D_MODEL), F32),
        scratch_shapes=[pltpu.VMEM((tm, D_MODEL), BF16), pltpu.VMEM((tm, D_MODEL), F32),
                        pltpu.VMEM((tm, TF_FFN), F32), pltpu.VMEM((tm, TF_FFN), F32)],
        compiler_params=_cparams(("parallel",)),
        name="ffn",
    )(*xs, mod, g.reshape(1, D_MODEL), wg, wu, wd)


def _mixer_h(x_ref, mod_ref, g_ref):
    y = _rms(x_ref[...], g_ref[...])
    return (y * (1.0 + mod_ref[1:2, :]) + mod_ref[0:1, :]).astype(BF16)


ATTN_WIDTHS = (A_HEADS * HEAD_DIM, A_KV_HEADS * HEAD_DIM, A_KV_HEADS * HEAD_DIM,
               B_HEADS * HEAD_DIM, B_HEADS * HEAD_DIM, B_HEADS * HEAD_DIM)


def _attn_in_kernel(lat, x_ref, mod_ref, g_ref, w_ref, gq_ref, gk_ref, *refs):
    if lat:
        cos_ref, sin_ref, *o_refs = refs
    else:
        o_refs = refs[4:]
    h = _mixer_h(x_ref, mod_ref, g_ref)
    nb = x_ref.shape[0] // SEQ
    first = _head_masks()
    scale = HEAD_DIM ** -0.5
    offs = np.cumsum((0,) + ATTN_WIDTHS)

    def seg(i):
        return _dot(h, w_ref[:, offs[i]:offs[i + 1]])

    def normed_pairs(y, g2):
        for p in range(y.shape[1] // PAIR):
            z = _pair_rms(y[:, p * PAIR:(p + 1) * PAIR], g2, first)
            if lat:
                z = _pair_rope(z, cos_ref[...], sin_ref[...])
            yield slice(p * PAIR, (p + 1) * PAIR), z

    for sl, z in normed_pairs(seg(0), gq_ref[...]):
        o_refs[0][:, sl] = (z * scale).astype(BF16)
    for sl, z in normed_pairs(seg(1), gk_ref[...]):
        o_refs[1][:, sl] = z.astype(BF16)
        if not lat:
            o_refs[6][:, :, sl] = z.reshape(nb, SEQ, PAIR)
    for i in (2, 3, 4, 5):
        y = seg(i)
        o_refs[i][...] = ((y * scale) if i == 3 else y).astype(BF16)
        if not lat and i != 3:
            o_refs[{2: 7, 4: 8, 5: 9}[i]][...] = y.reshape(nb, SEQ, y.shape[1])


def _gla_in_kernel(widths, x_ref, mod_ref, g_ref, w_ref, wgup_ref, bgup_ref, *o_refs):
    h = _mixer_h(x_ref, mod_ref, g_ref)
    off = sum(widths)
    d = _dot(h, w_ref[:, off:off + GLA_D_PAD]).astype(BF16)
    pre = _dot(d, wgup_ref[...]) + bgup_ref[...]
    lg = (jnp.minimum(pre, 0.0) - jnp.log1p(jnp.exp(-jnp.abs(pre)))) * (1.0 / GLA_GATE_NORM)
    o_refs[4][...] = lg[:, :GLA_KDIM]
    o_refs[5][...] = lg[:, GLA_KDIM:]
    off = 0
    for o_ref, wd in zip(o_refs[:4], widths):
        o_ref[...] = _dot(h, w_ref[:, off:off + wd]).astype(o_ref.dtype)
        off += wd


def _mixer_in_specs(layer, tm, n_in, tile0=0):
    return [
        pl.BlockSpec((tm, D_MODEL), lambda i: (tile0 + i, 0)),
        pl.BlockSpec((None, None, None, 3, D_MODEL), lambda i: (layer, _cond_row(tile0 + i, tm), 1, 0, 0)),
        pl.BlockSpec((1, D_MODEL), lambda i: (0, 0)),
        pl.BlockSpec((D_MODEL, n_in), lambda i: (0, 0)),
    ]


def _attn_in(x, mod, layer, g, w_in, gq, gk, *, lat, rope_tabs=None, layer_i=0, caches=None):
    tm = TM_PROJ
    n_rows, tile0 = (N_LAT, N_CTX // tm) if lat else (N_CTX, 0)
    in_specs = _mixer_in_specs(layer, tm, sum(ATTN_WIDTHS), tile0)
    in_specs += [pl.BlockSpec((1, PAIR), lambda i: (0, 0))] * 2
    args = [x, mod, g.reshape(1, D_MODEL), w_in, gq, gk]
    out_specs = [pl.BlockSpec((tm, wd), lambda i: (i, 0)) for wd in ATTN_WIDTHS]
    out_shape = [jax.ShapeDtypeStruct((n_rows, wd), BF16) for wd in ATTN_WIDTHS]
    aliases = {}
    if lat:
        per_seq = DEC_SEQ // tm
        in_specs += [pl.BlockSpec((tm, PAIR), lambda i: (i % per_seq, 0))] * 2
        args += list(rope_tabs)
    else:
        for c in caches:
            aliases[len(args)] = len(out_specs)
            in_specs.append(pl.BlockSpec(memory_space=pl.ANY))
            args.append(c)
            out_specs.append(pl.BlockSpec((tm // SEQ, None, SEQ, c.shape[-1]), lambda i: (i, layer_i, 0, 0)))
            out_shape.append(jax.ShapeDtypeStruct(c.shape, c.dtype))
    return pl.pallas_call(
        functools.partial(_attn_in_kernel, lat),
        grid=(n_rows // tm,),
        in_specs=in_specs,
        out_specs=out_specs,
        out_shape=out_shape,
        input_output_aliases=aliases,
        compiler_params=_cparams(("parallel",)),
        name="attn_in_lat" if lat else "attn_in_ctx",
    )(*args)


def _gla_in(x, mod, layer, g, w_in, w_gup, b_gup):
    tm = TM_GLA_IN
    widths = (GLA_KDIM, GLA_KDIM, GLA_VDIM, GLA_VDIM)
    out_w = widths + (GLA_KDIM, GLA_KDIM)
    out_dt = (BF16, BF16, BF16, BF16, F32, F32)
    n_in = sum(widths) + GLA_D_PAD
    return pl.pallas_call(
        functools.partial(_gla_in_kernel, widths),
        grid=(N_TOK // tm,),
        in_specs=_mixer_in_specs(layer, tm, n_in) + [
            pl.BlockSpec((GLA_D_PAD, 2 * GLA_KDIM), lambda i: (0, 0)),
            pl.BlockSpec((1, 2 * GLA_KDIM), lambda i: (0, 0)),
        ],
        out_specs=[pl.BlockSpec((tm, wd), lambda i: (i, 0)) for wd in out_w],
        out_shape=[jax.ShapeDtypeStruct((N_TOK, wd), dt) for wd, dt in zip(out_w, out_dt)],
        compiler_params=_cparams(("parallel",)),
        name="gla_in",
    )(x, mod, g.reshape(1, D_MODEL), w_in, w_gup, b_gup)


def _head_masks():
    lane = lax.broadcasted_iota(jnp.int32, (1, PAIR), 1)
    return lane < HEAD_DIM


def _pair_rms(x, g2, first):
    sq = x * x
    s0 = jnp.sum(jnp.where(first, sq, 0.0), axis=-1, keepdims=True)
    s1 = jnp.sum(jnp.where(first, 0.0, sq), axis=-1, keepdims=True)
    ms = jnp.where(first, s0, s1) * (1.0 / HEAD_DIM)
    return x * lax.rsqrt(ms + EPS) * g2


def _pair_rope(x, cos, sin_signed):
    lane = lax.broadcasted_iota(jnp.int32, x.shape, 1)
    quarter = HEAD_DIM // 4
    partner = jnp.where(lane % (2 * quarter) < quarter,
                        pltpu.roll(x, PAIR - quarter, 1), pltpu.roll(x, quarter, 1))
    return x * cos + partner * sin_signed


def _na_key_rows(tq):
    rows = DEC_SEQ // GRID_W
    wh = min(NA_WIN_H, rows)
    r = tq // GRID_W
    start = lambda qr: max(0, min(qr - wh // 2, rows - wh))
    lo = [start(qi * r) for qi in range(rows // r)]
    hi = [start(qi * r + r - 1) + wh for qi in range(rows // r)]
    span = max(h - l for l, h in zip(lo, hi))
    span += span % 2
    return span, [min(l, rows - span) for l in lo]


def _fill_na_bias(tab_ref, bias_scr, qi, tq, first, low, span):
    rows = DEC_SEQ // GRID_W
    wh = min(NA_WIN_H, rows)
    neg = -jnp.inf
    for a in range(tq // GRID_W):
        qr = qi * (tq // GRID_W) + a
        rs = jnp.clip(qr - wh // 2, 0, rows - wh)
        for m in range(span // 2):
            kr0, kr1 = low + 2 * m, low + 2 * m + 1
            ok0 = jnp.logical_and(kr0 >= rs, kr0 < rs + wh)
            ok1 = jnp.logical_and(kr1 >= rs, kr1 < rs + wh)
            d0 = jnp.clip(kr0 - qr + NA_WIN_H - 1, 0, 2 * NA_WIN_H - 2)
            d1 = jnp.clip(kr1 - qr + NA_WIN_H - 1, 0, 2 * NA_WIN_H - 2)
            for hh in range(2):
                t0 = jnp.where(ok0, tab_ref[hh, d0], neg)
                t1 = jnp.where(ok1, tab_ref[hh, d1], neg)
                bias_scr[hh * tq + a * GRID_W:hh * tq + (a + 1) * GRID_W, m * PAIR:(m + 1) * PAIR] = (
                    jnp.where(first, t0, t1))


def _attn_kernel(has_ctx, has_bias, n_seq, *refs):
    it = iter(refs)
    q_ref, k_ref, v_ref = next(it), next(it), next(it)
    kc_ref, vc_ref = (next(it), next(it)) if has_ctx else (None, None)
    tab_ref = next(it) if has_bias else None
    o_ref = next(it)
    bias_scr = next(it) if has_bias else None

    first = _head_masks()
    tq = q_ref.shape[0] // n_seq
    tk = k_ref.shape[0] // n_seq
    n_half = q_ref.shape[1] // PAIR
    rows = 2 * n_half * tq
    sub = min(rows, ATTN_SUB_ROWS)

    if has_bias:
        span, lows = _na_key_rows(tq)
        qi = pl.program_id(1)
        low = sum(jnp.where(qi == n, l, 0) for n, l in enumerate(lows))
        key_rows = pl.ds(pl.multiple_of(low * GRID_W, GRID_W), span * GRID_W)

        @pl.when(pl.program_id(2) == 0)
        def _():
            _fill_na_bias(tab_ref, bias_scr, qi, tq, first, low, span)

    zero = jnp.zeros((tq, PAIR), BF16)
    qs, ks, vs = [], [], []
    for sq in range(n_seq):
        stacked = []
        for j in range(n_half):
            qj = q_ref[sq * tq:(sq + 1) * tq, j * PAIR:(j + 1) * PAIR]
            stacked += [jnp.where(first, qj, zero), jnp.where(first, zero, qj)]
        qs.append(jnp.concatenate(stacked, axis=0))
        if has_bias:
            ks.append(k_ref[key_rows, :])
            vs.append(v_ref[key_rows, :])
        else:
            ks.append(k_ref[sq * tk:(sq + 1) * tk, :])
            vs.append(v_ref[sq * tk:(sq + 1) * tk, :])
    kc = kc_ref[...].astype(BF16) if has_ctx else None
    vc = vc_ref[...].astype(BF16) if has_ctx else None

    def scores(sq, i):
        qi = qs[sq][i * sub:(i + 1) * sub]
        return _dot_nt(qi, ks[sq]), (_dot_nt(qi, kc) if has_ctx else None)

    def finish(sq, i, s, sc):
        if has_bias:
            s = s + bias_scr[i * sub:(i + 1) * sub, :]
        m = jnp.max(s, axis=-1, keepdims=True)
        if has_ctx:
            m = jnp.maximum(m, jnp.max(sc, axis=-1, keepdims=True))
        p = jnp.exp(s - m)
        l = jnp.sum(p, axis=-1, keepdims=True)
        o = _dot(p.astype(BF16), vs[sq])
        if has_ctx:
            pc = jnp.exp(sc - m)
            l = l + jnp.sum(pc, axis=-1, keepdims=True)
            o = o + _dot(pc.astype(BF16), vc)
        return o / l

    groups = [(sq, i) for sq in range(n_seq) for i in range(rows // sub)]
    outs = [[] for _ in range(n_seq)]
    cur = scores(*groups[0])
    for n, (sq, i) in enumerate(groups):
        nxt = scores(*groups[n + 1]) if n + 1 < len(groups) else None
        outs[sq].append(finish(sq, i, *cur))
        cur = nxt
    for sq in range(n_seq):
        o = jnp.concatenate(outs[sq], axis=0) if len(outs[sq]) > 1 else outs[sq][0]
        for j in range(n_half):
            o_ref[sq * tq:(sq + 1) * tq, j * PAIR:(j + 1) * PAIR] = jnp.where(
                first, o[2 * j * tq:(2 * j + 1) * tq], o[(2 * j + 1) * tq:(2 * j + 2) * tq]).astype(BF16)


def _attention(q, k, v, *, lat, tq, n_seq=1, layer_i=0, cache_k=None, cache_v=None, bias_tab=None):
    has_bias = bias_tab is not None
    n_pairs = k.shape[1] // PAIR
    n_half = q.shape[1] // k.shape[1]
    nb, t = (DEC_BATCH, DEC_SEQ) if lat else (BATCH, SEQ)
    nq = t // tq
    assert n_seq == 1 or (nq == 1 and not lat)
    n_keys = _na_key_rows(tq)[0] * GRID_W if has_bias else t
    in_specs = [
        pl.BlockSpec((n_seq * tq, n_half * PAIR), lambda p, qi, b: (b * nq + qi, p)),
        pl.BlockSpec((n_seq * t, PAIR), lambda p, qi, b: (b, p)),
        pl.BlockSpec((n_seq * t, PAIR), lambda p, qi, b: (b, p)),
    ]
    args = [q, k, v]
    if lat:
        in_specs += [pl.BlockSpec((None, None, PAST_LEN, PAIR), lambda p, qi, b: (b, layer_i, 0, p))] * 2
        args += [cache_k, cache_v]
    if has_bias:
        in_specs.append(pl.BlockSpec((2,) + bias_tab.shape[1:], lambda p, qi, b: (p, 0, 0, 0)))
        args.append(bias_tab)
    return pl.pallas_call(
        functools.partial(_attn_kernel, lat, has_bias, n_seq),
        grid=(n_pairs, nq, nb // n_seq),
        in_specs=in_specs,
        out_specs=pl.BlockSpec((n_seq * tq, n_half * PAIR), lambda p, qi, b: (b * nq + qi, p)),
        out_shape=jax.ShapeDtypeStruct(q.shape, BF16),
        scratch_shapes=[pltpu.VMEM((2 * tq, n_keys), F32)] if has_bias else [],
        compiler_params=_cparams(("parallel", "parallel", "arbitrary")),
        name="attn_" + ("lat" if lat else "ctx") + ("_a" if n_half == 2 else "_b"),
    )(*args)


def _rope_tables():
    half = HEAD_DIM // 2
    freqs = ROPE_THETA ** (-np.arange(0, half, 2, dtype=np.float32) / half)
    t = np.arange(DEC_SEQ)

    def tab(pos):
        ang = pos.astype(np.float32)[:, None] * freqs[None, :]
        cos = np.concatenate([np.cos(ang), np.cos(ang)], -1)
        sin = np.concatenate([-np.sin(ang), np.sin(ang)], -1)
        return cos, sin

    cr, sr = tab(t // GRID_W)
    cc, sc = tab(t % GRID_W)
    cos = np.concatenate([cr, cc] * 2, -1).astype(np.float32)
    sin = np.concatenate([sr, sc] * 2, -1).astype(np.float32)
    return jnp.asarray(cos), jnp.asarray(sin)


def _na_bias_table(rel_bias):
    w = GRID_W
    n_rel = 2 * NA_WIN_W - 1
    left = w - NA_WIN_W
    rp = jnp.pad(rel_bias, ((0, 0), (0, 0), (left, 2 * w - n_rel - left)))
    flat = jnp.tile(rp, (1, 1, w))
    m = flat[..., :w * (2 * w - 1)].reshape(rel_bias.shape[0], rel_bias.shape[1], w, 2 * w - 1)[..., w - 1:]
    c = np.arange(w)
    ws = np.clip(c - NA_WIN_W // 2, 0, w - NA_WIN_W)
    in_col = (c[None, :] >= ws[:, None]) & (c[None, :] < ws[:, None] + NA_WIN_W)
    m = jnp.where(jnp.asarray(in_col)[None, None], m, -jnp.inf)
    return jnp.concatenate([m, m], axis=-1)


def _chunk_scan(x, reverse):
    t = x.shape[0]
    pos = lax.broadcasted_iota(jnp.int32, x.shape, 0) % GLA_CHUNK
    s = 1
    while s < GLA_CHUNK:
        if reverse:
            x = x + jnp.where(pos < GLA_CHUNK - s, pltpu.roll(x, t - s, 0), 0.0)
        else:
            x = x + jnp.where(pos >= s, pltpu.roll(x, s, 0), 0.0)
        s *= 2
    return x


def _gla_kernel(has_s0, emit_s, heads, *refs):
    it = iter(refs)
    q_ref, k_ref, v_ref, lgf_ref, lgb_ref = (next(it) for _ in range(5))
    s0_ref = next(it) if has_s0 else None
    if emit_s:
        next(it)
    o_ref = next(it)
    sf_ref = next(it) if emit_s else None
    qf_scr, qb_scr, kin_scr, kout_scr, bf_scr, bb_scr, a_scr, u_scr, st_scr = (next(it) for _ in range(9))

    c_len, dk, dv = GLA_CHUNK, GLA_DK, GLA_DV
    t = q_ref.shape[0]
    n = t // c_len
    scale = dk ** -0.5
    row = lax.broadcasted_iota(jnp.int32, (c_len, c_len), 0)
    col = lax.broadcasted_iota(jnp.int32, (c_len, c_len), 1)
    zero = jnp.zeros((t, dk), BF16)

    def rows(c):
        return pl.ds(pl.multiple_of(c * c_len, c_len), c_len)

    def one_head(hd, kl, vl):
        lgf, lgb = lgf_ref[:, kl], lgb_ref[:, kl]
        bf = _chunk_scan(lgf, False)
        bb = _chunk_scan(lgb, True)
        q = q_ref[:, kl].astype(F32) * scale
        k = k_ref[:, kl].astype(F32)
        qf_scr[:, :dk] = (q * jnp.exp(bf)).astype(BF16)
        qf_scr[:, dk:] = zero
        qb_scr[:, :dk] = zero
        qb_scr[:, dk:] = (q * jnp.exp(bb)).astype(BF16)
        kin_scr[:, :dk] = (k * jnp.exp(-bf)).astype(BF16)
        kin_scr[:, dk:] = (k * jnp.exp(-bb)).astype(BF16)
        bf3, bb3 = bf.reshape(n, c_len, dk), bb.reshape(n, c_len, dk)
        kout_scr[:, :dk] = (k * jnp.exp((bf3[:, c_len - 1:, :] - bf3).reshape(t, dk))).astype(BF16)
        kout_scr[:, dk:] = (k * jnp.exp((bb3[:, :1, :] - bb3).reshape(t, dk))).astype(BF16)
        bf_scr[...] = bf
        bb_scr[...] = bb

        def score_pass(c, _):
            r = rows(c)
            a2 = _dot_nt(jnp.concatenate([qf_scr[r, :], qb_scr[r, :]], axis=0), kin_scr[r, :])
            a = jnp.where(col <= row, a2[:c_len], 0.0) + jnp.where(col >= row, a2[c_len:], 0.0)
            a_scr[r, :] = a.astype(BF16)
            u = _dot_tn(v_ref[r, vl], kout_scr[r, :])
            u_scr[c, :, :dk] = u[:, :dk]
            u_scr[n - 1 - c, :, dk:] = u[:, dk:]
            return _

        def value_pass(c, _):
            r = rows(c)
            o_ref[r, vl] = _dot(a_scr[r, :], v_ref[r, vl])
            return _

        lax.fori_loop(0, n, score_pass, None, unroll=GLA_UNROLL)
        lax.fori_loop(0, n, value_pass, None, unroll=GLA_UNROLL)

        if has_s0:
            st_scr[:, :dk] = s0_ref[0, hd].T
            st_scr[:, dk:] = s0_ref[1, hd].T
        else:
            st_scr[...] = jnp.zeros_like(st_scr)

        def sweep(i, _):
            rf, rb = rows(i), rows(n - 1 - i)
            st = st_scr[...]
            res = _dot_nt(jnp.concatenate([qf_scr[rf, :], qb_scr[rb, :]], axis=0), st.astype(BF16))
            o_ref[rf, vl] += res[:c_len]
            o_ref[rb, vl] += res[c_len:]
            b_edge = jnp.concatenate([bf_scr[pl.ds(i * c_len + c_len - 1, 1), :],
                                      bb_scr[pl.ds((n - 1 - i) * c_len, 1), :]], axis=1)
            st_scr[...] = st * jnp.exp(b_edge) + u_scr[i]
            return _

        lax.fori_loop(0, n, sweep, None, unroll=GLA_UNROLL)

        if emit_s:
            sf_ref[0, hd] = st_scr[:, :dk].T
            sf_ref[1, hd] = st_scr[:, dk:].T

    if heads == 1:
        one_head(0, slice(0, dk), slice(0, dv))
    else:
        def head_step(hd, carry):
            one_head(hd, pl.ds(pl.multiple_of(hd * dk, dk), dk), pl.ds(pl.multiple_of(hd * dv, dv), dv))
            return carry

        lax.fori_loop(0, heads, head_step, None)


def _gla(q, k, v, lgf, lgb, *, lat, heads, layer_j=0, state=None, new_state=None):
    if lat:
        nb, t, row0 = DEC_BATCH, DEC_SEQ, N_CTX
    else:
        nb, t, row0 = BATCH, SEQ, 0
    off = row0 // t
    kspec = pl.BlockSpec((t, heads * GLA_DK), lambda b, h: (off + b, h))
    in_specs = [kspec, kspec, pl.BlockSpec((t, heads * GLA_DV), lambda b, h: (off + b, h)), kspec, kspec]
    args = [q, k, v, lgf, lgb]
    state_spec = pl.BlockSpec((None, None, 2, heads, GLA_DK, GLA_DV), lambda b, h: (b, layer_j, 0, h, 0, 0))
    if lat:
        in_specs.append(state_spec)
        args.append(state)
    out_specs = [pl.BlockSpec((t, heads * GLA_DV), lambda b, h: (b, h))]
    out_shape = [jax.ShapeDtypeStruct((nb * t, GLA_VDIM), F32)]
    aliases = {}
    if not lat:
        aliases[len(args)] = 1
        in_specs.append(pl.BlockSpec(memory_space=pl.ANY))
        args.append(new_state)
        out_specs.append(state_spec)
        out_shape.append(jax.ShapeDtypeStruct(new_state.shape, new_state.dtype))
    return pl.pallas_call(
        functools.partial(_gla_kernel, lat, not lat, heads),
        grid=(nb, GLA_HEADS // heads),
        in_specs=in_specs,
        out_specs=out_specs,
        out_shape=out_shape,
        input_output_aliases=aliases,
        scratch_shapes=[
            pltpu.VMEM((t, 2 * GLA_DK), BF16),
            pltpu.VMEM((t, 2 * GLA_DK), BF16),
            pltpu.VMEM((t, 2 * GLA_DK), BF16),
            pltpu.VMEM((t, 2 * GLA_DK), BF16),
            pltpu.VMEM((t, GLA_DK), F32),
            pltpu.VMEM((t, GLA_DK), F32),
            pltpu.VMEM((t, GLA_CHUNK), BF16),
            pltpu.VMEM((t // GLA_CHUNK, GLA_DV, 2 * GLA_DK), F32),
            pltpu.VMEM((GLA_DV, 2 * GLA_DK), F32),
        ],
        compiler_params=_cparams(("parallel", "parallel")),
        name="gla_lat" if lat else "gla_ctx",
    )(*args)


def _group_pick(tm, c_ref, l_ref):
    return jnp.where(pl.program_id(0) < N_CTX // tm, c_ref[...], l_ref[...])


def _attn_out_kernel(tm, x_ref, mod_ref, oac_ref, oal_ref, obc_ref, obl_ref, w_ref, o_ref):
    oa = _group_pick(tm, oac_ref, oal_ref)
    ob = _group_pick(tm, obc_ref, obl_ref)
    na = oa.shape[1]
    y = _dot(oa, w_ref[:na, :]) + _dot(ob, w_ref[na:, :])
    o_ref[...] = x_ref[...] + mod_ref[2:3, :] * y


def _gla_out_kernel(tm, x_ref, mod_ref, oc_ref, ol_ref, gate_ref, gn_ref, w_ref, o_ref):
    gn = gn_ref[...]
    o_pre = _group_pick(tm, oc_ref, ol_ref)
    parts = []
    for h in range(GLA_HEADS):
        sl = slice(h * GLA_DV, (h + 1) * GLA_DV)
        gt = gate_ref[:, sl].astype(F32)
        parts.append((_rms(o_pre[:, sl], gn) * (gt * jax.nn.sigmoid(gt))).astype(BF16))
    y = _dot(jnp.concatenate(parts, axis=-1), w_ref[...])
    o_ref[...] = x_ref[...] + mod_ref[2:3, :] * y


def _mixer_out(kernel, name, x, mod, layer, group_acts, acts, consts):
    tm = TM_PROJ
    nct = N_CTX // tm
    in_specs = [
        pl.BlockSpec((tm, D_MODEL), lambda i: (i, 0)),
        pl.BlockSpec((None, None, None, 3, D_MODEL), lambda i: (layer, _cond_row(i, tm), 1, 0, 0)),
    ]
    flat = []
    for a_c, a_l in group_acts:
        in_specs.append(pl.BlockSpec((tm, a_c.shape[1]), lambda i: (jnp.minimum(i, nct - 1), 0)))
        in_specs.append(pl.BlockSpec((tm, a_l.shape[1]), lambda i: (jnp.maximum(i - nct, 0), 0)))
        flat += [a_c, a_l]
    in_specs += [pl.BlockSpec((tm, a.shape[1]), lambda i: (i, 0)) for a in acts]
    in_specs += [pl.BlockSpec(c.shape, lambda i: (0, 0)) for c in consts]
    return pl.pallas_call(
        functools.partial(kernel, tm),
        grid=(N_TOK // tm,),
        in_specs=in_specs,
        out_specs=pl.BlockSpec((tm, D_MODEL), lambda i: (i, 0)),
        out_shape=jax.ShapeDtypeStruct((N_TOK, D_MODEL), F32),
        compiler_params=_cparams(("parallel",)),
        name=name,
    )(x, mod, *flat, *acts, *consts)


def _final_kernel(x_ref, g_ref, o_ref):
    o_ref[...] = _rms(x_ref[...], g_ref[...])


def _final_norm(x, g, row0, n_rows):
    tm = TM_PROJ
    off = row0 // tm
    return pl.pallas_call(
        _final_kernel,
        grid=(n_rows // tm,),
        in_specs=[pl.BlockSpec((tm, D_MODEL), lambda i: (off + i, 0)),
                  pl.BlockSpec((1, D_MODEL), lambda i: (0, 0))],
        out_specs=pl.BlockSpec((tm, D_MODEL), lambda i: (i, 0)),
        out_shape=jax.ShapeDtypeStruct((n_rows, D_MODEL), F32),
        compiler_params=_cparams(("parallel",)),
        name="final_norm",
    )(x, g.reshape(1, D_MODEL))


def _permute_heads(w, axis):
    grp = A_HEADS // A_KV_HEADS
    order = [(2 * p + j) * grp + g for p in range(A_KV_HEADS // 2) for g in range(grp) for j in range(2)]
    rest = tuple(np.delete(w.shape, axis))
    w = jnp.moveaxis(w, axis, 0).reshape((A_HEADS, HEAD_DIM) + rest)
    w = w[np.asarray(order)].reshape((A_HEADS * HEAD_DIM,) + rest)
    return jnp.moveaxis(w, 0, axis)


def kernel(x_prompt, x_sample, cache_a_k, cache_a_v, cache_b_k, cache_b_v, state_gla, c, c_ctx,
           w_mod, b_mod, g_norm, w_ffn_gate, w_ffn_up, w_ffn_down, w_attn_in, w_attn_out,
           g_qnorm, g_knorm, na_rel_bias, w_gla_in, w_gla_gup, b_gla_gup, g_gla_norm, w_gla_out, g_final):
    x = (x_prompt.reshape(N_CTX, D_MODEL), x_sample.reshape(N_LAT, D_MODEL))
    cond = jnp.concatenate([c_ctx[None, :], c, jnp.zeros((COND_ROWS - 1 - DEC_BATCH, D_MODEL), F32)], axis=0)
    mod = _modulation(cond, w_mod, b_mod)

    wg_all, wu_all, wd_all = (w.astype(BF16) for w in (w_ffn_gate, w_ffn_up, w_ffn_down))
    rope_tabs = _rope_tables()
    n_attn = w_attn_in.shape[0]
    ck_a = cache_a_k.reshape(DEC_BATCH, n_attn, PAST_LEN, A_KV_HEADS * HEAD_DIM)
    cv_a = cache_a_v.reshape(DEC_BATCH, n_attn, PAST_LEN, A_KV_HEADS * HEAD_DIM)
    ck_b = cache_b_k.reshape(DEC_BATCH, n_attn, PAST_LEN, B_HEADS * HEAD_DIM)
    cv_b = cache_b_v.reshape(DEC_BATCH, n_attn, PAST_LEN, B_HEADS * HEAD_DIM)

    caches = [jnp.zeros((BATCH, n_attn, SEQ, ATTN_WIDTHS[s]), F32) for s in (1, 2, 4, 5)]
    new_gla = jnp.zeros((BATCH, w_gla_in.shape[0], 2, GLA_HEADS, GLA_DK, GLA_DV), F32)
    for l in range(DEPTH):
        x = _ffn(x, mod, l, 0, g_norm[l, 0], wg_all, wu_all, wd_all)
        if l % 2 == 0:
            i = l // 2
            gq = jnp.tile(g_qnorm[i], 2).reshape(1, PAIR)
            gk = jnp.tile(g_knorm[i], 2).reshape(1, PAIR)
            n_qa = A_HEADS * HEAD_DIM
            w_in = w_attn_in[i].at[:, :n_qa].set(_permute_heads(w_attn_in[i][:, :n_qa], axis=1)).astype(BF16)
            w_out = w_attn_out[i].at[:n_qa].set(_permute_heads(w_attn_out[i][:n_qa], axis=0)).astype(BF16)
            qa_c, ka_c, va_c, qb_c, kb_c, vb_c, *caches = _attn_in(
                x, mod, l, g_norm[l, 1], w_in, gq, gk, lat=False, layer_i=i, caches=caches)
            qa_l, ka_l, va_l, qb_l, kb_l, vb_l = _attn_in(
                x, mod, l, g_norm[l, 1], w_in, gq, gk, lat=True, rope_tabs=rope_tabs)
            oa_c = _attention(qa_c, ka_c, va_c, lat=False, tq=SEQ, n_seq=CTX_SEQS_A)
            ob_c = _attention(qb_c, kb_c, vb_c, lat=False, tq=SEQ, n_seq=CTX_SEQS_B)
            oa_l = _attention(qa_l, ka_l, va_l, lat=True, tq=TQ_LAT_A, layer_i=i, cache_k=ck_a, cache_v=cv_a)
            ob_l = _attention(qb_l, kb_l, vb_l, lat=True, tq=TQ_LAT_B, layer_i=i, cache_k=ck_b, cache_v=cv_b,
                              bias_tab=_na_bias_table(na_rel_bias[i]))
            x = _mixer_out(_attn_out_kernel, "attn_out", x, mod, l, [(oa_c, oa_l), (ob_c, ob_l)], [], [w_out])
        else:
            j = l // 2
            w_in = jnp.pad(w_gla_in[j], ((0, 0), (0, GLA_D_PAD - 2 * GLA_RANK))).astype(BF16)
            w_gup = jnp.zeros((GLA_D_PAD, 2 * GLA_KDIM), F32)
            w_gup = w_gup.at[:GLA_RANK, :GLA_KDIM].set(w_gla_gup[j, 0])
            w_gup = w_gup.at[GLA_RANK:2 * GLA_RANK, GLA_KDIM:].set(w_gla_gup[j, 1]).astype(BF16)
            b_gup = b_gla_gup[j].reshape(1, 2 * GLA_KDIM)
            q, k, v, gt, lgf, lgb = _gla_in(x, mod, l, g_norm[l, 1], w_in, w_gup, b_gup)
            o_c, new_gla = _gla(q, k, v, lgf, lgb, lat=False, heads=GLA_HEADS, layer_j=j, new_state=new_gla)
            (o_l,) = _gla(q, k, v, lgf, lgb, lat=True, heads=1, layer_j=j, state=state_gla)
            x = _mixer_out(_gla_out_kernel, "gla_out", x, mod, l, [(o_c, o_l)], [gt],
                           [g_gla_norm[j].reshape(1, GLA_DV), w_gla_out[j].astype(BF16)])
        x = _ffn(x, mod, l, 2, g_norm[l, 2], wg_all, wu_all, wd_all)

    y_ctx = _final_norm(x, g_final, 0, N_CTX)
    y_lat = _final_norm(x, g_final, N_CTX, N_LAT)
    return (y_ctx.reshape(BATCH, SEQ, D_MODEL), y_lat.reshape(DEC_BATCH, DEC_SEQ, D_MODEL),
            caches[0].reshape(BATCH, n_attn, SEQ, A_KV_HEADS, HEAD_DIM),
            caches[1].reshape(BATCH, n_attn, SEQ, A_KV_HEADS, HEAD_DIM),
            caches[2].reshape(BATCH, n_attn, SEQ, B_HEADS, HEAD_DIM),
            caches[3].reshape(BATCH, n_attn, SEQ, B_HEADS, HEAD_DIM), new_gla)
```

```python
import functools

import numpy as np
import jax
import jax.numpy as jnp
from jax import lax
from jax.experimental import pallas as pl
from jax.experimental.pallas import tpu as pltpu

D_MODEL = 1024
BATCH = 32
SEQ = 256
DEPTH = 4
DEC_BATCH = 8
DEC_SEQ = 1024
PAST_LEN = 512
GRID_W = 64
HEAD_DIM = 64
A_HEADS = 8
A_KV_HEADS = 4
B_HEADS = 8
ROPE_THETA = 10000.0
NA_WIN_H = 8
NA_WIN_W = 16
GLA_HEADS = 4
GLA_DK = D_MODEL // 2 // GLA_HEADS
GLA_DV = D_MODEL // GLA_HEADS
GLA_KDIM = GLA_HEADS * GLA_DK
GLA_VDIM = GLA_HEADS * GLA_DV
GLA_RANK = 16
GLA_GATE_NORM = 16.0
GLA_CHUNK = 64
D_FF = ((8 * D_MODEL // 3 + 127) // 128) * 128
FFN_RES = 0.5
N_MOD = 9
EPS = 1e-6

N_CTX = BATCH * SEQ
N_LAT = DEC_BATCH * DEC_SEQ
N_TOK = N_CTX + N_LAT
COND_ROWS = 16
LANES = 128
PAIR = 2 * HEAD_DIM
GLA_D_PAD = LANES
VMEM_LIMIT = 56 * 1024 * 1024

F32 = jnp.float32
BF16 = jnp.bfloat16

TM_FFN = 1024
TF_FFN = 256
FFN_CHUNKS = D_FF // TF_FFN
FFN_ROW_BLOCK = 256
assert FFN_CHUNKS * TF_FFN == D_FF and FFN_CHUNKS % 2 == 1
TM_PROJ = 512
TM_GLA_IN = 1024
TQ_LAT_A = 256
TQ_LAT_B = 512
CTX_SEQS_A = 2
CTX_SEQS_B = 4
ATTN_SUB_ROWS = 512
GLA_UNROLL = 8


def _cparams(sem):
    return pltpu.CompilerParams(dimension_semantics=sem, vmem_limit_bytes=VMEM_LIMIT)


def _cond_row(i, tm):
    nct = N_CTX // tm
    per_batch = DEC_SEQ // tm
    return jnp.where(i < nct, 0, 1 + (i - nct) // per_batch)


def _rms(x, g):
    ms = jnp.mean(x * x, axis=-1, keepdims=True)
    return x * lax.rsqrt(ms + EPS) * g


def _dot(a, b):
    return jnp.dot(a, b, preferred_element_type=F32)


def _dot_nt(a, b):
    return lax.dot_general(a, b, (((1,), (1,)), ((), ())), preferred_element_type=F32)


def _dot_tn(a, b):
    return lax.dot_general(a, b, (((0,), (0,)), ((), ())), preferred_element_type=F32)


def _split_bf16(x):
    hi = x.astype(BF16)
    return hi, (x - hi.astype(F32)).astype(BF16)


def _mod_kernel(c_ref, w_ref, b_ref, o_ref):
    c = c_ref[...]
    s_hi, s_lo = _split_bf16(c * jax.nn.sigmoid(c))
    w_hi, w_lo = _split_bf16(w_ref[...])
    o_ref[...] = _dot(s_hi, w_hi) + (_dot(s_lo, w_hi) + _dot(s_hi, w_lo)) + b_ref[...]


def _modulation(cond, w_mod, b_mod):
    tn = D_MODEL
    n_mod = N_MOD * D_MODEL
    out = pl.pallas_call(
        _mod_kernel,
        grid=(DEPTH, n_mod // tn),
        in_specs=[
            pl.BlockSpec((COND_ROWS, D_MODEL), lambda l, j: (0, 0)),
            pl.BlockSpec((None, D_MODEL, tn), lambda l, j: (l, 0, j)),
            pl.BlockSpec((None, 1, tn), lambda l, j: (l, 0, j)),
        ],
        out_specs=pl.BlockSpec((None, COND_ROWS, tn), lambda l, j: (l, 0, j)),
        out_shape=jax.ShapeDtypeStruct((DEPTH, COND_ROWS, n_mod), F32),
        compiler_params=_cparams(("parallel", "parallel")),
        name="adaln_mod",
    )(cond, w_mod, b_mod.reshape(DEPTH, 1, n_mod))
    return out.reshape(DEPTH, COND_ROWS, 3, 3, D_MODEL)


def _ffn_kernel(split_x, *refs):
    if split_x:
        xc_ref, xl_ref, *refs = refs
    else:
        x_ref, *refs = refs
    mod_ref, g_ref, wg_ref, wu_ref, wd_ref, o_ref, h_ref, acc_ref, g_scr, u_scr = refs
    tm = o_ref.shape[0]

    def x_rows(rs):
        if split_x:
            return jnp.where(pl.program_id(0) < N_CTX // tm, xc_ref[rs, :], xl_ref[rs, :])
        return x_ref[rs, :]

    def cols(j):
        return pl.ds(pl.multiple_of(j * TF_FFN, TF_FFN), TF_FFN)

    def gate_up(h, j):
        return _dot(h, wg_ref[:, cols(j)]), _dot(h, wu_ref[:, cols(j)])

    def down(g, u, j):
        a = (g * jax.nn.sigmoid(g) * u).astype(BF16)
        return _dot(a, wd_ref[cols(j), :])

    row_blocks = [slice(r, r + FFN_ROW_BLOCK) for r in range(0, tm, FFN_ROW_BLOCK)]
    gain = g_ref[...] * (1.0 + mod_ref[1:2, :])
    for rs in row_blocks:
        x = x_rows(rs)
        ms = jnp.mean(x * x, axis=-1, keepdims=True)
        h = (x * lax.rsqrt(ms + EPS) * gain + mod_ref[0:1, :]).astype(BF16)
        h_ref[rs, :] = h
        g_scr[rs, :], u_scr[rs, :] = gate_up(h, 0)
    acc_ref[...] = jnp.zeros_like(acc_ref)

    def chunk_pair(jj, carry):
        j = 2 * jj
        h = h_ref[...]
        g1, u1 = gate_up(h, j + 1)
        acc_ref[...] += down(g_scr[...], u_scr[...], j)
        g2, u2 = gate_up(h, j + 2)
        acc_ref[...] += down(g1, u1, j + 1)
        g_scr[...], u_scr[...] = g2, u2
        return carry

    lax.fori_loop(0, FFN_CHUNKS // 2, chunk_pair, None, unroll=True)
    res_gate = FFN_RES * mod_ref[2:3, :]
    for rs in row_blocks:
        y = acc_ref[rs, :] + down(g_scr[rs, :], u_scr[rs, :], FFN_CHUNKS - 1)
        o_ref[rs, :] = x_rows(rs) + res_gate * y


def _ffn(x, mod, layer, sub, g, wg, wu, wd):
    tm = TM_FFN
    nct = N_CTX // tm
    which = sub // 2
    resident = lambda w: pl.BlockSpec((None, None) + w.shape[2:], lambda i: (layer, which, 0, 0),
                                      pipeline_mode=pl.Buffered(1))
    split_x = isinstance(x, tuple)
    if split_x:
        x_specs = [pl.BlockSpec((tm, D_MODEL), lambda i: (jnp.minimum(i, nct - 1), 0)),
                   pl.BlockSpec((tm, D_MODEL), lambda i: (jnp.maximum(i - nct, 0), 0))]
        xs = list(x)
    else:
        x_specs = [pl.BlockSpec((tm, D_MODEL), lambda i: (i, 0))]
        xs = [x]
    return pl.pallas_call(
        functools.partial(_ffn_kernel, split_x),
        grid=(N_TOK // tm,),
        in_specs=x_specs + [
            pl.BlockSpec((None, None, None, 3, D_MODEL), lambda i: (layer, _cond_row(i, tm), sub, 0, 0)),
            pl.BlockSpec((1, D_MODEL), lambda i: (0, 0)),
            resident(wg), resident(wu), resident(wd),
        ],
        out_specs=pl.BlockSpec((tm, D_MODEL), lambda i: (i, 0)),
        out_shape=jax.ShapeDtypeStruct((N_TOK, D_MODEL), F32),
        scratch_shapes=[pltpu.VMEM((tm, D_MODEL), BF16), pltpu.VMEM((tm, D_MODEL), F32),
                        pltpu.VMEM((tm, TF_FFN), F32), pltpu.VMEM((tm, TF_FFN), F32)],
        compiler_params=_cparams(("parallel",)),
        name="ffn",
    )(*xs, mod, g.reshape(1, D_MODEL), wg, wu, wd)


def _mixer_h(x_ref, mod_ref, g_ref):
    y = _rms(x_ref[...], g_ref[...])
    return (y * (1.0 + mod_ref[1:2, :]) + mod_ref[0:1, :]).astype(BF16)


ATTN_WIDTHS = (A_HEADS * HEAD_DIM, A_KV_HEADS * HEAD_DIM, A_KV_HEADS * HEAD_DIM,
               B_HEADS * HEAD_DIM, B_HEADS * HEAD_DIM, B_HEADS * HEAD_DIM)


def _attn_in_kernel(lat, x_ref, mod_ref, g_ref, w_ref, gq_ref, gk_ref, *refs):
    if lat:
        cos_ref, sin_ref, *o_refs = refs
    else:
        o_refs = refs[4:]
    h = _mixer_h(x_ref, mod_ref, g_ref)
    nb = x_ref.shape[0] // SEQ
    first = _head_masks()
    scale = HEAD_DIM ** -0.5
    offs = np.cumsum((0,) + ATTN_WIDTHS)

    def seg(i):
        return _dot(h, w_ref[:, offs[i]:offs[i + 1]])

    def normed_pairs(y, g2):
        for p in range(y.shape[1] // PAIR):
            z = _pair_rms(y[:, p * PAIR:(p + 1) * PAIR], g2, first)
            if lat:
                z = _pair_rope(z, cos_ref[...], sin_ref[...])
            yield slice(p * PAIR, (p + 1) * PAIR), z

    for sl, z in normed_pairs(seg(0), gq_ref[...]):
        o_refs[0][:, sl] = (z * scale).astype(BF16)
    for sl, z in normed_pairs(seg(1), gk_ref[...]):
        o_refs[1][:, sl] = z.astype(BF16)
        if not lat:
            o_refs[6][:, :, sl] = z.reshape(nb, SEQ, PAIR)
    for i in (2, 3, 4, 5):
        y = seg(i)
        o_refs[i][...] = ((y * scale) if i == 3 else y).astype(BF16)
        if not lat and i != 3:
            o_refs[{2: 7, 4: 8, 5: 9}[i]][...] = y.reshape(nb, SEQ, y.shape[1])


def _gla_in_kernel(widths, x_ref, mod_ref, g_ref, w_ref, wgup_ref, bgup_ref, *o_refs):
    h = _mixer_h(x_ref, mod_ref, g_ref)
    offs = np.cumsum((0,) + tuple(widths))

    def project(i, lo=0, hi=None):
        hi = widths[i] if hi is None else hi
        o_refs[i][:, lo:hi] = _dot(h, w_ref[:, offs[i] + lo:offs[i] + hi]).astype(o_refs[i].dtype)

    d = _dot(h, w_ref[:, offs[4]:offs[4] + GLA_D_PAD]).astype(BF16)
    project(0)
    pre = _dot(d, wgup_ref[...]) + bgup_ref[...]
    project(1)
    lg = (jnp.minimum(pre, 0.0) - jnp.log1p(jnp.exp(-jnp.abs(pre)))) * (1.0 / GLA_GATE_NORM)
    o_refs[4][...] = lg[:, :GLA_KDIM]
    o_refs[5][...] = lg[:, GLA_KDIM:]
    for i in (2, 3):
        half = widths[i] // 2
        project(i, 0, half)
        project(i, half, widths[i])


def _mixer_in_specs(layer, tm, n_in, tile0=0):
    return [
        pl.BlockSpec((tm, D_MODEL), lambda i: (tile0 + i, 0)),
        pl.BlockSpec((None, None, None, 3, D_MODEL), lambda i: (layer, _cond_row(tile0 + i, tm), 1, 0, 0)),
        pl.BlockSpec((1, D_MODEL), lambda i: (0, 0)),
        pl.BlockSpec((D_MODEL, n_in), lambda i: (0, 0)),
    ]


def _attn_in(x, mod, layer, g, w_in, gq, gk, *, lat, rope_tabs=None, layer_i=0, caches=None):
    tm = TM_PROJ
    n_rows, tile0 = (N_LAT, N_CTX // tm) if lat else (N_CTX, 0)
    in_specs = _mixer_in_specs(layer, tm, sum(ATTN_WIDTHS), tile0)
    in_specs += [pl.BlockSpec((1, PAIR), lambda i: (0, 0))] * 2
    args = [x, mod, g.reshape(1, D_MODEL), w_in, gq, gk]
    out_specs = [pl.BlockSpec((tm, wd), lambda i: (i, 0)) for wd in ATTN_WIDTHS]
    out_shape = [jax.ShapeDtypeStruct((n_rows, wd), BF16) for wd in ATTN_WIDTHS]
    aliases = {}
    if lat:
        per_seq = DEC_SEQ // tm
        in_specs += [pl.BlockSpec((tm, PAIR), lambda i: (i % per_seq, 0))] * 2
        args += list(rope_tabs)
    else:
        for c in caches:
            aliases[len(args)] = len(out_specs)
            in_specs.append(pl.BlockSpec(memory_space=pl.ANY))
            args.append(c)
            out_specs.append(pl.BlockSpec((tm // SEQ, None, SEQ, c.shape[-1]), lambda i: (i, layer_i, 0, 0)))
            out_shape.append(jax.ShapeDtypeStruct(c.shape, c.dtype))
    return pl.pallas_call(
        functools.partial(_attn_in_kernel, lat),
        grid=(n_rows // tm,),
        in_specs=in_specs,
        out_specs=out_specs,
        out_shape=out_shape,
        input_output_aliases=aliases,
        compiler_params=_cparams(("parallel",)),
        name="attn_in_lat" if lat else "attn_in_ctx",
    )(*args)


def _gla_in(x, mod, layer, g, w_in, w_gup, b_gup):
    tm = TM_GLA_IN
    widths = (GLA_KDIM, GLA_KDIM, GLA_VDIM, GLA_VDIM)
    out_w = widths + (GLA_KDIM, GLA_KDIM)
    out_dt = (BF16, BF16, BF16, BF16, F32, F32)
    n_in = sum(widths) + GLA_D_PAD
    return pl.pallas_call(
        functools.partial(_gla_in_kernel, widths),
        grid=(N_TOK // tm,),
        in_specs=_mixer_in_specs(layer, tm, n_in) + [
            pl.BlockSpec((GLA_D_PAD, 2 * GLA_KDIM), lambda i: (0, 0)),
            pl.BlockSpec((1, 2 * GLA_KDIM), lambda i: (0, 0)),
        ],
        out_specs=[pl.BlockSpec((tm, wd), lambda i: (i, 0)) for wd in out_w],
        out_shape=[jax.ShapeDtypeStruct((N_TOK, wd), dt) for wd, dt in zip(out_w, out_dt)],
        compiler_params=_cparams(("parallel",)),
        name="gla_in",
    )(x, mod, g.reshape(1, D_MODEL), w_in, w_gup, b_gup)


def _head_masks():
    lane = lax.broadcasted_iota(jnp.int32, (1, PAIR), 1)
    return lane < HEAD_DIM


def _pair_rms(x, g2, first):
    sq = x * x
    s0 = jnp.sum(jnp.where(first, sq, 0.0), axis=-1, keepdims=True)
    s1 = jnp.sum(jnp.where(first, 0.0, sq), axis=-1, keepdims=True)
    ms = jnp.where(first, s0, s1) * (1.0 / HEAD_DIM)
    return x * lax.rsqrt(ms + EPS) * g2


def _pair_rope(x, cos, sin_signed):
    lane = lax.broadcasted_iota(jnp.int32, x.shape, 1)
    quarter = HEAD_DIM // 4
    partner = jnp.where(lane % (2 * quarter) < quarter,
                        pltpu.roll(x, PAIR - quarter, 1), pltpu.roll(x, quarter, 1))
    return x * cos + partner * sin_signed


def _na_key_rows(tq):
    rows = DEC_SEQ // GRID_W
    wh = min(NA_WIN_H, rows)
    r = tq // GRID_W
    start = lambda qr: max(0, min(qr - wh // 2, rows - wh))
    lo = [start(qi * r) for qi in range(rows // r)]
    hi = [start(qi * r + r - 1) + wh for qi in range(rows // r)]
    span = max(h - l for l, h in zip(lo, hi))
    span += span % 2
    return span, [min(l, rows - span) for l in lo]


def _fill_na_bias(tab_ref, bias_scr, qi, tq, first, low, span):
    rows = DEC_SEQ // GRID_W
    wh = min(NA_WIN_H, rows)
    neg = -jnp.inf
    for a in range(tq // GRID_W):
        qr = qi * (tq // GRID_W) + a
        rs = jnp.clip(qr - wh // 2, 0, rows - wh)
        for m in range(span // 2):
            kr0, kr1 = low + 2 * m, low + 2 * m + 1
            ok0 = jnp.logical_and(kr0 >= rs, kr0 < rs + wh)
            ok1 = jnp.logical_and(kr1 >= rs, kr1 < rs + wh)
            d0 = jnp.clip(kr0 - qr + NA_WIN_H - 1, 0, 2 * NA_WIN_H - 2)
            d1 = jnp.clip(kr1 - qr + NA_WIN_H - 1, 0, 2 * NA_WIN_H - 2)
            for hh in range(2):
                t0 = jnp.where(ok0, tab_ref[hh, d0], neg)
                t1 = jnp.where(ok1, tab_ref[hh, d1], neg)
                bias_scr[hh * tq + a * GRID_W:hh * tq + (a + 1) * GRID_W, m * PAIR:(m + 1) * PAIR] = (
                    jnp.where(first, t0, t1))


def _attn_kernel(has_ctx, has_bias, n_seq, *refs):
    it = iter(refs)
    q_ref, k_ref, v_ref = next(it), next(it), next(it)
    kc_ref, vc_ref = (next(it), next(it)) if has_ctx else (None, None)
    tab_ref = next(it) if has_bias else None
    o_ref = next(it)
    bias_scr = next(it) if has_bias else None

    first = _head_masks()
    tq = q_ref.shape[0] // n_seq
    tk = k_ref.shape[0] // n_seq
    n_half = q_ref.shape[1] // PAIR
    rows = 2 * n_half * tq
    sub = min(rows, ATTN_SUB_ROWS)

    if has_bias:
        span, lows = _na_key_rows(tq)
        qi = pl.program_id(1)
        low = sum(jnp.where(qi == n, l, 0) for n, l in enumerate(lows))
        key_rows = pl.ds(pl.multiple_of(low * GRID_W, GRID_W), span * GRID_W)

        @pl.when(pl.program_id(2) == 0)
        def _():
            _fill_na_bias(tab_ref, bias_scr, qi, tq, first, low, span)

    zero = jnp.zeros((tq, PAIR), BF16)
    qs, ks, vs = [], [], []
    for sq in range(n_seq):
        stacked = []
        for j in range(n_half):
            qj = q_ref[sq * tq:(sq + 1) * tq, j * PAIR:(j + 1) * PAIR]
            stacked += [jnp.where(first, qj, zero), jnp.where(first, zero, qj)]
        qs.append(jnp.concatenate(stacked, axis=0))
        if has_bias:
            ks.append(k_ref[key_rows, :])
            vs.append(v_ref[key_rows, :])
        else:
            ks.append(k_ref[sq * tk:(sq + 1) * tk, :])
            vs.append(v_ref[sq * tk:(sq + 1) * tk, :])
    kc = kc_ref[...].astype(BF16) if has_ctx else None
    vc = vc_ref[...].astype(BF16) if has_ctx else None

    def scores(sq, i):
        qi = qs[sq][i * sub:(i + 1) * sub]
        return _dot_nt(qi, ks[sq]), (_dot_nt(qi, kc) if has_ctx else None)

    def finish(sq, i, s, sc):
        if has_bias:
            s = s + bias_scr[i * sub:(i + 1) * sub, :]
        m = jnp.max(s, axis=-1, keepdims=True)
        if has_ctx:
            m = jnp.maximum(m, jnp.max(sc, axis=-1, keepdims=True))
        p = jnp.exp(s - m)
        l = jnp.sum(p, axis=-1, keepdims=True)
        o = _dot(p.astype(BF16), vs[sq])
        if has_ctx:
            pc = jnp.exp(sc - m)
            l = l + jnp.sum(pc, axis=-1, keepdims=True)
            o = o + _dot(pc.astype(BF16), vc)
        return o / l

    groups = [(sq, i) for sq in range(n_seq) for i in range(rows // sub)]
    outs = [[] for _ in range(n_seq)]
    cur = scores(*groups[0])
    for n, (sq, i) in enumerate(groups):
        nxt = scores(*groups[n + 1]) if n + 1 < len(groups) else None
        outs[sq].append(finish(sq, i, *cur))
        cur = nxt
    for sq in range(n_seq):
        o = jnp.concatenate(outs[sq], axis=0) if len(outs[sq]) > 1 else outs[sq][0]
        for j in range(n_half):
            o_ref[sq * tq:(sq + 1) * tq, j * PAIR:(j + 1) * PAIR] = jnp.where(
                first, o[2 * j * tq:(2 * j + 1) * tq], o[(2 * j + 1) * tq:(2 * j + 2) * tq]).astype(BF16)


def _attention(q, k, v, *, lat, tq, n_seq=1, layer_i=0, cache_k=None, cache_v=None, bias_tab=None):
    has_bias = bias_tab is not None
    n_pairs = k.shape[1] // PAIR
    n_half = q.shape[1] // k.shape[1]
    nb, t = (DEC_BATCH, DEC_SEQ) if lat else (BATCH, SEQ)
    nq = t // tq
    assert n_seq == 1 or (nq == 1 and not lat)
    n_keys = _na_key_rows(tq)[0] * GRID_W if has_bias else t
    in_specs = [
        pl.BlockSpec((n_seq * tq, n_half * PAIR), lambda p, qi, b: (b * nq + qi, p)),
        pl.BlockSpec((n_seq * t, PAIR), lambda p, qi, b: (b, p)),
        pl.BlockSpec((n_seq * t, PAIR), lambda p, qi, b: (b, p)),
    ]
    args = [q, k, v]
    if lat:
        in_specs += [pl.BlockSpec((None, None, PAST_LEN, PAIR), lambda p, qi, b: (b, layer_i, 0, p))] * 2
        args += [cache_k, cache_v]
    if has_bias:
        in_specs.append(pl.BlockSpec((2,) + bias_tab.shape[1:], lambda p, qi, b: (p, 0, 0, 0)))
        args.append(bias_tab)
    return pl.pallas_call(
        functools.partial(_attn_kernel, lat, has_bias, n_seq),
        grid=(n_pairs, nq, nb // n_seq),
        in_specs=in_specs,
        out_specs=pl.BlockSpec((n_seq * tq, n_half * PAIR), lambda p, qi, b: (b * nq + qi, p)),
        out_shape=jax.ShapeDtypeStruct(q.shape, BF16),
        scratch_shapes=[pltpu.VMEM((2 * tq, n_keys), F32)] if has_bias else [],
        compiler_params=_cparams(("parallel", "parallel", "arbitrary")),
        name="attn_" + ("lat" if lat else "ctx") + ("_a" if n_half == 2 else "_b"),
    )(*args)


def _rope_tables():
    half = HEAD_DIM // 2
    freqs = ROPE_THETA ** (-np.arange(0, half, 2, dtype=np.float32) / half)
    t = np.arange(DEC_SEQ)

    def tab(pos):
        ang = pos.astype(np.float32)[:, None] * freqs[None, :]
        cos = np.concatenate([np.cos(ang), np.cos(ang)], -1)
        sin = np.concatenate([-np.sin(ang), np.sin(ang)], -1)
        return cos, sin

    cr, sr = tab(t // GRID_W)
    cc, sc = tab(t % GRID_W)
    cos = np.concatenate([cr, cc] * 2, -1).astype(np.float32)
    sin = np.concatenate([sr, sc] * 2, -1).astype(np.float32)
    return jnp.asarray(cos), jnp.asarray(sin)


def _na_bias_table(rel_bias):
    w = GRID_W
    n_rel = 2 * NA_WIN_W - 1
    left = w - NA_WIN_W
    rp = jnp.pad(rel_bias, ((0, 0), (0, 0), (left, 2 * w - n_rel - left)))
    flat = jnp.tile(rp, (1, 1, w))
    m = flat[..., :w * (2 * w - 1)].reshape(rel_bias.shape[0], rel_bias.shape[1], w, 2 * w - 1)[..., w - 1:]
    c = np.arange(w)
    ws = np.clip(c - NA_WIN_W // 2, 0, w - NA_WIN_W)
    in_col = (c[None, :] >= ws[:, None]) & (c[None, :] < ws[:, None] + NA_WIN_W)
    m = jnp.where(jnp.asarray(in_col)[None, None], m, -jnp.inf)
    return jnp.concatenate([m, m], axis=-1)


def _chunk_scan(x, reverse):
    t = x.shape[0]
    pos = lax.broadcasted_iota(jnp.int32, x.shape, 0) % GLA_CHUNK
    s = 1
    while s < GLA_CHUNK:
        if reverse:
            x = x + jnp.where(pos < GLA_CHUNK - s, pltpu.roll(x, t - s, 0), 0.0)
        else:
            x = x + jnp.where(pos >= s, pltpu.roll(x, s, 0), 0.0)
        s *= 2
    return x


def _gla_kernel(has_s0, emit_s, heads, *refs):
    it = iter(refs)
    q_ref, k_ref, v_ref, lgf_ref, lgb_ref = (next(it) for _ in range(5))
    s0_ref = next(it) if has_s0 else None
    if emit_s:
        next(it)
    o_ref = next(it)
    sf_ref = next(it) if emit_s else None
    qf_scr, qb_scr, kin_scr, kout_scr, bf_scr, bb_scr, a_scr, u_scr, st_scr = (next(it) for _ in range(9))

    c_len, dk, dv = GLA_CHUNK, GLA_DK, GLA_DV
    t = q_ref.shape[0]
    n = t // c_len
    scale = dk ** -0.5
    row = lax.broadcasted_iota(jnp.int32, (c_len, c_len), 0)
    col = lax.broadcasted_iota(jnp.int32, (c_len, c_len), 1)
    zero = jnp.zeros((t, dk), BF16)

    def rows(c):
        return pl.ds(pl.multiple_of(c * c_len, c_len), c_len)

    def one_head(hd, kl, vl):
        lgf, lgb = lgf_ref[:, kl], lgb_ref[:, kl]
        bf = _chunk_scan(lgf, False)
        bb = _chunk_scan(lgb, True)
        q = q_ref[:, kl].astype(F32) * scale
        k = k_ref[:, kl].astype(F32)
        qf_scr[:, :dk] = (q * jnp.exp(bf)).astype(BF16)
        qf_scr[:, dk:] = zero
        qb_scr[:, :dk] = zero
        qb_scr[:, dk:] = (q * jnp.exp(bb)).astype(BF16)
        kin_scr[:, :dk] = (k * jnp.exp(-bf)).astype(BF16)
        kin_scr[:, dk:] = (k * jnp.exp(-bb)).astype(BF16)
        bf3, bb3 = bf.reshape(n, c_len, dk), bb.reshape(n, c_len, dk)
        kout_scr[:, :dk] = (k * jnp.exp((bf3[:, c_len - 1:, :] - bf3).reshape(t, dk))).astype(BF16)
        kout_scr[:, dk:] = (k * jnp.exp((bb3[:, :1, :] - bb3).reshape(t, dk))).astype(BF16)
        bf_scr[...] = bf
        bb_scr[...] = bb

        def score_pass(c, _):
            r = rows(c)
            a2 = _dot_nt(jnp.concatenate([qf_scr[r, :], qb_scr[r, :]], axis=0), kin_scr[r, :])
            a = jnp.where(col <= row, a2[:c_len], 0.0) + jnp.where(col >= row, a2[c_len:], 0.0)
            a_scr[r, :] = a.astype(BF16)
            u = _dot_tn(v_ref[r, vl], kout_scr[r, :])
            u_scr[c, :, :dk] = u[:, :dk]
            u_scr[n - 1 - c, :, dk:] = u[:, dk:]
            return _

        def value_pass(c, _):
            r = rows(c)
            o_ref[r, vl] = _dot(a_scr[r, :], v_ref[r, vl])
            return _

        lax.fori_loop(0, n, score_pass, None, unroll=GLA_UNROLL)
        lax.fori_loop(0, n, value_pass, None, unroll=GLA_UNROLL)

        if has_s0:
            st_scr[:, :dk] = s0_ref[0, hd].T
            st_scr[:, dk:] = s0_ref[1, hd].T
        else:
            st_scr[...] = jnp.zeros_like(st_scr)

        def sweep(i, _):
            rf, rb = rows(i), rows(n - 1 - i)
            st = st_scr[...]
            res = _dot_nt(jnp.concatenate([qf_scr[rf, :], qb_scr[rb, :]], axis=0), st.astype(BF16))
            o_ref[rf, vl] += res[:c_len]
            o_ref[rb, vl] += res[c_len:]
            b_edge = jnp.concatenate([bf_scr[pl.ds(i * c_len + c_len - 1, 1), :],
                                      bb_scr[pl.ds((n - 1 - i) * c_len, 1), :]], axis=1)
            st_scr[...] = st * jnp.exp(b_edge) + u_scr[i]
            return _

        lax.fori_loop(0, n, sweep, None, unroll=GLA_UNROLL)

        if emit_s:
            sf_ref[0, hd] = st_scr[:, :dk].T
            sf_ref[1, hd] = st_scr[:, dk:].T

    if heads == 1:
        one_head(0, slice(0, dk), slice(0, dv))
    else:
        def head_step(hd, carry):
            one_head(hd, pl.ds(pl.multiple_of(hd * dk, dk), dk), pl.ds(pl.multiple_of(hd * dv, dv), dv))
            return carry

        lax.fori_loop(0, heads, head_step, None)


def _gla(q, k, v, lgf, lgb, *, lat, heads, layer_j=0, state=None, new_state=None):
    if lat:
        nb, t, row0 = DEC_BATCH, DEC_SEQ, N_CTX
    else:
        nb, t, row0 = BATCH, SEQ, 0
    off = row0 // t
    kspec = pl.BlockSpec((t, heads * GLA_DK), lambda b, h: (off + b, h))
    in_specs = [kspec, kspec, pl.BlockSpec((t, heads * GLA_DV), lambda b, h: (off + b, h)), kspec, kspec]
    args = [q, k, v, lgf, lgb]
    state_spec = pl.BlockSpec((None, None, 2, heads, GLA_DK, GLA_DV), lambda b, h: (b, layer_j, 0, h, 0, 0))
    if lat:
        in_specs.append(state_spec)
        args.append(state)
    out_specs = [pl.BlockSpec((t, heads * GLA_DV), lambda b, h: (b, h))]
    out_shape = [jax.ShapeDtypeStruct((nb * t, GLA_VDIM), F32)]
    aliases = {}
    if not lat:
        aliases[len(args)] = 1
        in_specs.append(pl.BlockSpec(memory_space=pl.ANY))
        args.append(new_state)
        out_specs.append(state_spec)
        out_shape.append(jax.ShapeDtypeStruct(new_state.shape, new_state.dtype))
    return pl.pallas_call(
        functools.partial(_gla_kernel, lat, not lat, heads),
        grid=(nb, GLA_HEADS // heads),
        in_specs=in_specs,
        out_specs=out_specs,
        out_shape=out_shape,
        input_output_aliases=aliases,
        scratch_shapes=[
            pltpu.VMEM((t, 2 * GLA_DK), BF16),
            pltpu.VMEM((t, 2 * GLA_DK), BF16),
            pltpu.VMEM((t, 2 * GLA_DK), BF16),
            pltpu.VMEM((t, 2 * GLA_DK), BF16),
            pltpu.VMEM((t, GLA_DK), F32),
            pltpu.VMEM((t, GLA_DK), F32),
            pltpu.VMEM((t, GLA_CHUNK), BF16),
            pltpu.VMEM((t // GLA_CHUNK, GLA_DV, 2 * GLA_DK), F32),
            pltpu.VMEM((GLA_DV, 2 * GLA_DK), F32),
        ],
        compiler_params=_cparams(("parallel", "parallel")),
        name="gla_lat" if lat else "gla_ctx",
    )(*args)


def _group_pick(tm, c_ref, l_ref):
    return jnp.where(pl.program_id(0) < N_CTX // tm, c_ref[...], l_ref[...])


def _attn_out_kernel(tm, x_ref, mod_ref, oac_ref, oal_ref, obc_ref, obl_ref, w_ref, o_ref):
    oa = _group_pick(tm, oac_ref, oal_ref)
    ob = _group_pick(tm, obc_ref, obl_ref)
    na = oa.shape[1]
    y = _dot(oa, w_ref[:na, :]) + _dot(ob, w_ref[na:, :])
    o_ref[...] = x_ref[...] + mod_ref[2:3, :] * y


def _gla_out_kernel(tm, x_ref, mod_ref, oc_ref, ol_ref, gate_ref, gn_ref, w_ref, o_ref):
    gn = gn_ref[...]
    o_pre = _group_pick(tm, oc_ref, ol_ref)
    parts = []
    for h in range(GLA_HEADS):
        sl = slice(h * GLA_DV, (h + 1) * GLA_DV)
        gt = gate_ref[:, sl].astype(F32)
        parts.append((_rms(o_pre[:, sl], gn) * (gt * jax.nn.sigmoid(gt))).astype(BF16))
    y = _dot(jnp.concatenate(parts, axis=-1), w_ref[...])
    o_ref[...] = x_ref[...] + mod_ref[2:3, :] * y


def _mixer_out(kernel, name, x, mod, layer, group_acts, acts, consts):
    tm = TM_PROJ
    nct = N_CTX // tm
    in_specs = [
        pl.BlockSpec((tm, D_MODEL), lambda i: (i, 0)),
        pl.BlockSpec((None, None, None, 3, D_MODEL), lambda i: (layer, _cond_row(i, tm), 1, 0, 0)),
    ]
    flat = []
    for a_c, a_l in group_acts:
        in_specs.append(pl.BlockSpec((tm, a_c.shape[1]), lambda i: (jnp.minimum(i, nct - 1), 0)))
        in_specs.append(pl.BlockSpec((tm, a_l.shape[1]), lambda i: (jnp.maximum(i - nct, 0), 0)))
        flat += [a_c, a_l]
    in_specs += [pl.BlockSpec((tm, a.shape[1]), lambda i: (i, 0)) for a in acts]
    in_specs += [pl.BlockSpec(c.shape, lambda i: (0, 0)) for c in consts]
    return pl.pallas_call(
        functools.partial(kernel, tm),
        grid=(N_TOK // tm,),
        in_specs=in_specs,
        out_specs=pl.BlockSpec((tm, D_MODEL), lambda i: (i, 0)),
        out_shape=jax.ShapeDtypeStruct((N_TOK, D_MODEL), F32),
        compiler_params=_cparams(("parallel",)),
        name=name,
    )(x, mod, *flat, *acts, *consts)


def _final_kernel(x_ref, g_ref, o_ref):
    o_ref[...] = _rms(x_ref[...], g_ref[...])


def _final_norm(x, g, row0, n_rows):
    tm = TM_PROJ
    off = row0 // tm
    return pl.pallas_call(
        _final_kernel,
        grid=(n_rows // tm,),
        in_specs=[pl.BlockSpec((tm, D_MODEL), lambda i: (off + i, 0)),
                  pl.BlockSpec((1, D_MODEL), lambda i: (0, 0))],
        out_specs=pl.BlockSpec((tm, D_MODEL), lambda i: (i, 0)),
        out_shape=jax.ShapeDtypeStruct((n_rows, D_MODEL), F32),
        compiler_params=_cparams(("parallel",)),
        name="final_norm",
    )(x, g.reshape(1, D_MODEL))


def _permute_heads(w, axis):
    grp = A_HEADS // A_KV_HEADS
    order = [(2 * p + j) * grp + g for p in range(A_KV_HEADS // 2) for g in range(grp) for j in range(2)]
    rest = tuple(np.delete(w.shape, axis))
    w = jnp.moveaxis(w, axis, 0).reshape((A_HEADS, HEAD_DIM) + rest)
    w = w[np.asarray(order)].reshape((A_HEADS * HEAD_DIM,) + rest)
    return jnp.moveaxis(w, 0, axis)


def kernel(x_prompt, x_sample, cache_a_k, cache_a_v, cache_b_k, cache_b_v, state_gla, c, c_ctx,
           w_mod, b_mod, g_norm, w_ffn_gate, w_ffn_up, w_ffn_down, w_attn_in, w_attn_out,
           g_qnorm, g_knorm, na_rel_bias, w_gla_in, w_gla_gup, b_gla_gup, g_gla_norm, w_gla_out, g_final):
    x = (x_prompt.reshape(N_CTX, D_MODEL), x_sample.reshape(N_LAT, D_MODEL))
    cond = jnp.concatenate([c_ctx[None, :], c, jnp.zeros((COND_ROWS - 1 - DEC_BATCH, D_MODEL), F32)], axis=0)
    mod = _modulation(cond, w_mod, b_mod)

    wg_all, wu_all, wd_all = (w.astype(BF16) for w in (w_ffn_gate, w_ffn_up, w_ffn_down))
    rope_tabs = _rope_tables()
    n_attn = w_attn_in.shape[0]
    ck_a = cache_a_k.reshape(DEC_BATCH, n_attn, PAST_LEN, A_KV_HEADS * HEAD_DIM)
    cv_a = cache_a_v.reshape(DEC_BATCH, n_attn, PAST_LEN, A_KV_HEADS * HEAD_DIM)
    ck_b = cache_b_k.reshape(DEC_BATCH, n_attn, PAST_LEN, B_HEADS * HEAD_DIM)
    cv_b = cache_b_v.reshape(DEC_BATCH, n_attn, PAST_LEN, B_HEADS * HEAD_DIM)

    caches = [jnp.zeros((BATCH, n_attn, SEQ, ATTN_WIDTHS[s]), F32) for s in (1, 2, 4, 5)]
    new_gla = jnp.zeros((BATCH, w_gla_in.shape[0], 2, GLA_HEADS, GLA_DK, GLA_DV), F32)
    for l in range(DEPTH):
        x = _ffn(x, mod, l, 0, g_norm[l, 0], wg_all, wu_all, wd_all)
        if l % 2 == 0:
            i = l // 2
            gq = jnp.tile(g_qnorm[i], 2).reshape(1, PAIR)
            gk = jnp.tile(g_knorm[i], 2).reshape(1, PAIR)
            n_qa = A_HEADS * HEAD_DIM
            w_in = w_attn_in[i].at[:, :n_qa].set(_permute_heads(w_attn_in[i][:, :n_qa], axis=1)).astype(BF16)
            w_out = w_attn_out[i].at[:n_qa].set(_permute_heads(w_attn_out[i][:n_qa], axis=0)).astype(BF16)
            qa_c, ka_c, va_c, qb_c, kb_c, vb_c, *caches = _attn_in(
                x, mod, l, g_norm[l, 1], w_in, gq, gk, lat=False, layer_i=i, caches=caches)
            qa_l, ka_l, va_l, qb_l, kb_l, vb_l = _attn_in(
                x, mod, l, g_norm[l, 1], w_in, gq, gk, lat=True, rope_tabs=rope_tabs)
            oa_c = _attention(qa_c, ka_c, va_c, lat=False, tq=SEQ, n_seq=CTX_SEQS_A)
            ob_c = _attention(qb_c, kb_c, vb_c, lat=False, tq=SEQ, n_seq=CTX_SEQS_B)
            oa_l = _attention(qa_l, ka_l, va_l, lat=True, tq=TQ_LAT_A, layer_i=i, cache_k=ck_a, cache_v=cv_a)
            ob_l = _attention(qb_l, kb_l, vb_l, lat=True, tq=TQ_LAT_B, layer_i=i, cache_k=ck_b, cache_v=cv_b,
                              bias_tab=_na_bias_table(na_rel_bias[i]))
            x = _mixer_out(_attn_out_kernel, "attn_out", x, mod, l, [(oa_c, oa_l), (ob_c, ob_l)], [], [w_out])
        else:
            j = l // 2
            w_in = jnp.pad(w_gla_in[j], ((0, 0), (0, GLA_D_PAD - 2 * GLA_RANK))).astype(BF16)
            w_gup = jnp.zeros((GLA_D_PAD, 2 * GLA_KDIM), F32)
            w_gup = w_gup.at[:GLA_RANK, :GLA_KDIM].set(w_gla_gup[j, 0])
            w_gup = w_gup.at[GLA_RANK:2 * GLA_RANK, GLA_KDIM:].set(w_gla_gup[j, 1]).astype(BF16)
            b_gup = b_gla_gup[j].reshape(1, 2 * GLA_KDIM)
            q, k, v, gt, lgf, lgb = _gla_in(x, mod, l, g_norm[l, 1], w_in, w_gup, b_gup)
            o_c, new_gla = _gla(q, k, v, lgf, lgb, lat=False, heads=GLA_HEADS, layer_j=j, new_state=new_gla)
            (o_l,) = _gla(q, k, v, lgf, lgb, lat=True, heads=1, layer_j=j, state=state_gla)
            x = _mixer_out(_gla_out_kernel, "gla_out", x, mod, l, [(o_c, o_l)], [gt],
                           [g_gla_norm[j].reshape(1, GLA_DV), w_gla_out[j].astype(BF16)])
        x = _ffn(x, mod, l, 2, g_norm[l, 2], wg_all, wu_all, wd_all)

    y_ctx = _final_norm(x, g_final, 0, N_CTX)
    y_lat = _final_norm(x, g_final, N_CTX, N_LAT)
    return (y_ctx.reshape(BATCH, SEQ, D_MODEL), y_lat.reshape(DEC_BATCH, DEC_SEQ, D_MODEL),
            caches[0].reshape(BATCH, n_attn, SEQ, A_KV_HEADS, HEAD_DIM),
            caches[1].reshape(BATCH, n_attn, SEQ, A_KV_HEADS, HEAD_DIM),
            caches[2].reshape(BATCH, n_attn, SEQ, B_HEADS, HEAD_DIM),
            caches[3].reshape(BATCH, n_attn, SEQ, B_HEADS, HEAD_DIM), new_gla)
```

```python
import functools

import numpy as np
import jax
import jax.numpy as jnp
from jax import lax
from jax.experimental import pallas as pl
from jax.experimental.pallas import tpu as pltpu

D_MODEL = 1024
BATCH = 32
SEQ = 256
DEPTH = 4
DEC_BATCH = 8
DEC_SEQ = 1024
PAST_LEN = 512
GRID_W = 64
HEAD_DIM = 64
A_HEADS = 8
A_KV_HEADS = 4
B_HEADS = 8
ROPE_THETA = 10000.0
NA_WIN_H = 8
NA_WIN_W = 16
GLA_HEADS = 4
GLA_DK = D_MODEL // 2 // GLA_HEADS
GLA_DV = D_MODEL // GLA_HEADS
GLA_KDIM = GLA_HEADS * GLA_DK
GLA_VDIM = GLA_HEADS * GLA_DV
GLA_RANK = 16
GLA_GATE_NORM = 16.0
GLA_CHUNK = 64
D_FF = ((8 * D_MODEL // 3 + 127) // 128) * 128
FFN_RES = 0.5
N_MOD = 9
EPS = 1e-6

N_CTX = BATCH * SEQ
N_LAT = DEC_BATCH * DEC_SEQ
N_TOK = N_CTX + N_LAT
COND_ROWS = 16
LANES = 128
PAIR = 2 * HEAD_DIM
GLA_D_PAD = LANES
VMEM_LIMIT = 56 * 1024 * 1024

F32 = jnp.float32
BF16 = jnp.bfloat16

TM_FFN = 1024
TF_FFN = 256
FFN_CHUNKS = D_FF // TF_FFN
FFN_ROW_BLOCK = 256
assert FFN_CHUNKS * TF_FFN == D_FF and FFN_CHUNKS % 2 == 1
TM_PROJ = 512
TM_OUT = 1024
TM_GLA_IN = 1024
TQ_LAT_A = 256
TQ_LAT_B = 512
CTX_SEQS_A = 2
CTX_SEQS_B = 4
ATTN_SUB_ROWS = 512
GLA_UNROLL = 8


def _cparams(sem):
    return pltpu.CompilerParams(dimension_semantics=sem, vmem_limit_bytes=VMEM_LIMIT)


def _cond_row(i, tm):
    nct = N_CTX // tm
    per_batch = DEC_SEQ // tm
    return jnp.where(i < nct, 0, 1 + (i - nct) // per_batch)


def _rms(x, g):
    ms = jnp.mean(x * x, axis=-1, keepdims=True)
    return x * lax.rsqrt(ms + EPS) * g


def _dot(a, b):
    return jnp.dot(a, b, preferred_element_type=F32)


def _dot_nt(a, b):
    return lax.dot_general(a, b, (((1,), (1,)), ((), ())), preferred_element_type=F32)


def _dot_tn(a, b):
    return lax.dot_general(a, b, (((0,), (0,)), ((), ())), preferred_element_type=F32)


def _split_bf16(x):
    hi = x.astype(BF16)
    return hi, (x - hi.astype(F32)).astype(BF16)


def _mod_kernel(c_ref, w_ref, b_ref, o_ref):
    c = c_ref[...]
    s_hi, s_lo = _split_bf16(c * jax.nn.sigmoid(c))
    w_hi, w_lo = _split_bf16(w_ref[...])
    o_ref[...] = _dot(s_hi, w_hi) + (_dot(s_lo, w_hi) + _dot(s_hi, w_lo)) + b_ref[...]


def _modulation(cond, w_mod, b_mod):
    tn = D_MODEL
    n_mod = N_MOD * D_MODEL
    out = pl.pallas_call(
        _mod_kernel,
        grid=(DEPTH, n_mod // tn),
        in_specs=[
            pl.BlockSpec((COND_ROWS, D_MODEL), lambda l, j: (0, 0)),
            pl.BlockSpec((None, D_MODEL, tn), lambda l, j: (l, 0, j)),
            pl.BlockSpec((None, 1, tn), lambda l, j: (l, 0, j)),
        ],
        out_specs=pl.BlockSpec((None, COND_ROWS, tn), lambda l, j: (l, 0, j)),
        out_shape=jax.ShapeDtypeStruct((DEPTH, COND_ROWS, n_mod), F32),
        compiler_params=_cparams(("parallel", "parallel")),
        name="adaln_mod",
    )(cond, w_mod, b_mod.reshape(DEPTH, 1, n_mod))
    return out.reshape(DEPTH, COND_ROWS, 3, 3, D_MODEL)


def _ffn_kernel(split_x, *refs):
    if split_x:
        xc_ref, xl_ref, *refs = refs
    else:
        x_ref, *refs = refs
    mod_ref, g_ref, wg_ref, wu_ref, wd_ref, o_ref, h_ref, acc_ref, g_scr, u_scr = refs
    tm = o_ref.shape[0]

    def x_rows(rs):
        if split_x:
            return jnp.where(pl.program_id(0) < N_CTX // tm, xc_ref[rs, :], xl_ref[rs, :])
        return x_ref[rs, :]

    def cols(j):
        return pl.ds(pl.multiple_of(j * TF_FFN, TF_FFN), TF_FFN)

    def gate_up(h, j):
        return _dot(h, wg_ref[:, cols(j)]), _dot(h, wu_ref[:, cols(j)])

    def down(g, u, j):
        a = (g * jax.nn.sigmoid(g) * u).astype(BF16)
        return _dot(a, wd_ref[cols(j), :])

    row_blocks = [slice(r, r + FFN_ROW_BLOCK) for r in range(0, tm, FFN_ROW_BLOCK)]
    gain = g_ref[...] * (1.0 + mod_ref[1:2, :])
    for rs in row_blocks:
        x = x_rows(rs)
        ms = jnp.mean(x * x, axis=-1, keepdims=True)
        h = (x * lax.rsqrt(ms + EPS) * gain + mod_ref[0:1, :]).astype(BF16)
        h_ref[rs, :] = h
        g_scr[rs, :], u_scr[rs, :] = gate_up(h, 0)
    acc_ref[...] = jnp.zeros_like(acc_ref)

    def chunk_pair(jj, carry):
        j = 2 * jj
        h = h_ref[...]
        g1, u1 = gate_up(h, j + 1)
        acc_ref[...] += down(g_scr[...], u_scr[...], j)
        g2, u2 = gate_up(h, j + 2)
        acc_ref[...] += down(g1, u1, j + 1)
        g_scr[...], u_scr[...] = g2, u2
        return carry

    lax.fori_loop(0, FFN_CHUNKS // 2, chunk_pair, None, unroll=True)
    res_gate = FFN_RES * mod_ref[2:3, :]
    for rs in row_blocks:
        y = acc_ref[rs, :] + down(g_scr[rs, :], u_scr[rs, :], FFN_CHUNKS - 1)
        o_ref[rs, :] = x_rows(rs) + res_gate * y


def _ffn(x, mod, layer, sub, g, wg, wu, wd):
    tm = TM_FFN
    nct = N_CTX // tm
    which = sub // 2
    resident = lambda w: pl.BlockSpec((None, None) + w.shape[2:], lambda i: (layer, which, 0, 0),
                                      pipeline_mode=pl.Buffered(1))
    split_x = isinstance(x, tuple)
    if split_x:
        x_specs = [pl.BlockSpec((tm, D_MODEL), lambda i: (jnp.minimum(i, nct - 1), 0)),
                   pl.BlockSpec((tm, D_MODEL), lambda i: (jnp.maximum(i - nct, 0), 0))]
        xs = list(x)
    else:
        x_specs = [pl.BlockSpec((tm, D_MODEL), lambda i: (i, 0))]
        xs = [x]
    return pl.pallas_call(
        functools.partial(_ffn_kernel, split_x),
        grid=(N_TOK // tm,),
        in_specs=x_specs + [
            pl.BlockSpec((None, None, None, 3, D_MODEL), lambda i: (layer, _cond_row(i, tm), sub, 0, 0)),
            pl.BlockSpec((1, D_MODEL), lambda i: (0, 0)),
            resident(wg), resident(wu), resident(wd),
        ],
        out_specs=pl.BlockSpec((tm, D_MODEL), lambda i: (i, 0)),
        out_shape=jax.ShapeDtypeStruct((N_TOK, D_MODEL), F32),
        scratch_shapes=[pltpu.VMEM((tm, D_MODEL), BF16), pltpu.VMEM((tm, D_MODEL), F32),
                        pltpu.VMEM((tm, TF_FFN), F32), pltpu.VMEM((tm, TF_FFN), F32)],
        compiler_params=_cparams(("parallel",)),
        name="ffn",
    )(*xs, mod, g.reshape(1, D_MODEL), wg, wu, wd)


def _mixer_h(x_ref, mod_ref, g_ref):
    y = _rms(x_ref[...], g_ref[...])
    return (y * (1.0 + mod_ref[1:2, :]) + mod_ref[0:1, :]).astype(BF16)


ATTN_WIDTHS = (A_HEADS * HEAD_DIM, A_KV_HEADS * HEAD_DIM, A_KV_HEADS * HEAD_DIM,
               B_HEADS * HEAD_DIM, B_HEADS * HEAD_DIM, B_HEADS * HEAD_DIM)


def _attn_in_kernel(lat, x_ref, mod_ref, g_ref, w_ref, gq_ref, gk_ref, *refs):
    if lat:
        cos_ref, sin_ref, *o_refs = refs
    else:
        o_refs = refs[4:]
    h = _mixer_h(x_ref, mod_ref, g_ref)
    nb = x_ref.shape[0] // SEQ
    first = _head_masks()
    scale = HEAD_DIM ** -0.5
    offs = np.cumsum((0,) + ATTN_WIDTHS)

    def seg(i):
        return _dot(h, w_ref[:, offs[i]:offs[i + 1]])

    def normed_pairs(y, g2):
        for p in range(y.shape[1] // PAIR):
            z = _pair_rms(y[:, p * PAIR:(p + 1) * PAIR], g2, first)
            if lat:
                z = _pair_rope(z, cos_ref[...], sin_ref[...])
            yield slice(p * PAIR, (p + 1) * PAIR), z

    for sl, z in normed_pairs(seg(0), gq_ref[...]):
        o_refs[0][:, sl] = (z * scale).astype(BF16)
    for sl, z in normed_pairs(seg(1), gk_ref[...]):
        o_refs[1][:, sl] = z.astype(BF16)
        if not lat:
            o_refs[6][:, :, sl] = z.reshape(nb, SEQ, PAIR)
    for i in (2, 3, 4, 5):
        y = seg(i)
        o_refs[i][...] = ((y * scale) if i == 3 else y).astype(BF16)
        if not lat and i != 3:
            o_refs[{2: 7, 4: 8, 5: 9}[i]][...] = y.reshape(nb, SEQ, y.shape[1])


def _gla_in_kernel(widths, x_ref, mod_ref, g_ref, w_ref, wgup_ref, bgup_ref, *o_refs):
    h = _mixer_h(x_ref, mod_ref, g_ref)
    offs = np.cumsum((0,) + tuple(widths))

    def project(i, lo=0, hi=None):
        hi = widths[i] if hi is None else hi
        o_refs[i][:, lo:hi] = _dot(h, w_ref[:, offs[i] + lo:offs[i] + hi]).astype(o_refs[i].dtype)

    d = _dot(h, w_ref[:, offs[4]:offs[4] + GLA_D_PAD]).astype(BF16)
    project(0)
    pre = _dot(d, wgup_ref[...]) + bgup_ref[...]
    project(1)
    lg = (jnp.minimum(pre, 0.0) - jnp.log1p(jnp.exp(-jnp.abs(pre)))) * (1.0 / GLA_GATE_NORM)
    o_refs[4][...] = lg[:, :GLA_KDIM]
    o_refs[5][...] = lg[:, GLA_KDIM:]
    for i in (2, 3):
        half = widths[i] // 2
        project(i, 0, half)
        project(i, half, widths[i])


def _mixer_in_specs(layer, tm, n_in, tile0=0):
    return [
        pl.BlockSpec((tm, D_MODEL), lambda i: (tile0 + i, 0)),
        pl.BlockSpec((None, None, None, 3, D_MODEL), lambda i: (layer, _cond_row(tile0 + i, tm), 1, 0, 0)),
        pl.BlockSpec((1, D_MODEL), lambda i: (0, 0)),
        pl.BlockSpec((D_MODEL, n_in), lambda i: (0, 0)),
    ]


def _attn_in(x, mod, layer, g, w_in, gq, gk, *, lat, rope_tabs=None, layer_i=0, caches=None):
    tm = TM_PROJ
    n_rows, tile0 = (N_LAT, N_CTX // tm) if lat else (N_CTX, 0)
    in_specs = _mixer_in_specs(layer, tm, sum(ATTN_WIDTHS), tile0)
    in_specs += [pl.BlockSpec((1, PAIR), lambda i: (0, 0))] * 2
    args = [x, mod, g.reshape(1, D_MODEL), w_in, gq, gk]
    out_specs = [pl.BlockSpec((tm, wd), lambda i: (i, 0)) for wd in ATTN_WIDTHS]
    out_shape = [jax.ShapeDtypeStruct((n_rows, wd), BF16) for wd in ATTN_WIDTHS]
    aliases = {}
    if lat:
        per_seq = DEC_SEQ // tm
        in_specs += [pl.BlockSpec((tm, PAIR), lambda i: (i % per_seq, 0))] * 2
        args += list(rope_tabs)
    else:
        for c in caches:
            aliases[len(args)] = len(out_specs)
            in_specs.append(pl.BlockSpec(memory_space=pl.ANY))
            args.append(c)
            out_specs.append(pl.BlockSpec((tm // SEQ, None, SEQ, c.shape[-1]), lambda i: (i, layer_i, 0, 0)))
            out_shape.append(jax.ShapeDtypeStruct(c.shape, c.dtype))
    return pl.pallas_call(
        functools.partial(_attn_in_kernel, lat),
        grid=(n_rows // tm,),
        in_specs=in_specs,
        out_specs=out_specs,
        out_shape=out_shape,
        input_output_aliases=aliases,
        compiler_params=_cparams(("parallel",)),
        name="attn_in_lat" if lat else "attn_in_ctx",
    )(*args)


def _gla_in(x, mod, layer, g, w_in, w_gup, b_gup):
    tm = TM_GLA_IN
    widths = (GLA_KDIM, GLA_KDIM, GLA_VDIM, GLA_VDIM)
    out_w = widths + (GLA_KDIM, GLA_KDIM)
    out_dt = (BF16, BF16, BF16, BF16, F32, F32)
    n_in = sum(widths) + GLA_D_PAD
    return pl.pallas_call(
        functools.partial(_gla_in_kernel, widths),
        grid=(N_TOK // tm,),
        in_specs=_mixer_in_specs(layer, tm, n_in) + [
            pl.BlockSpec((GLA_D_PAD, 2 * GLA_KDIM), lambda i: (0, 0)),
            pl.BlockSpec((1, 2 * GLA_KDIM), lambda i: (0, 0)),
        ],
        out_specs=[pl.BlockSpec((tm, wd), lambda i: (i, 0)) for wd in out_w],
        out_shape=[jax.ShapeDtypeStruct((N_TOK, wd), dt) for wd, dt in zip(out_w, out_dt)],
        compiler_params=_cparams(("parallel",)),
        name="gla_in",
    )(x, mod, g.reshape(1, D_MODEL), w_in, w_gup, b_gup)


def _head_masks():
    lane = lax.broadcasted_iota(jnp.int32, (1, PAIR), 1)
    return lane < HEAD_DIM


def _pair_rms(x, g2, first):
    sq = x * x
    s0 = jnp.sum(jnp.where(first, sq, 0.0), axis=-1, keepdims=True)
    s1 = jnp.sum(jnp.where(first, 0.0, sq), axis=-1, keepdims=True)
    ms = jnp.where(first, s0, s1) * (1.0 / HEAD_DIM)
    return x * lax.rsqrt(ms + EPS) * g2


def _pair_rope(x, cos, sin_signed):
    lane = lax.broadcasted_iota(jnp.int32, x.shape, 1)
    quarter = HEAD_DIM // 4
    partner = jnp.where(lane % (2 * quarter) < quarter,
                        pltpu.roll(x, PAIR - quarter, 1), pltpu.roll(x, quarter, 1))
    return x * cos + partner * sin_signed


def _na_key_rows(tq):
    rows = DEC_SEQ // GRID_W
    wh = min(NA_WIN_H, rows)
    r = tq // GRID_W
    start = lambda qr: max(0, min(qr - wh // 2, rows - wh))
    lo = [start(qi * r) for qi in range(rows // r)]
    hi = [start(qi * r + r - 1) + wh for qi in range(rows // r)]
    span = max(h - l for l, h in zip(lo, hi))
    span += span % 2
    return span, [min(l, rows - span) for l in lo]


def _fill_na_bias(tab_ref, bias_scr, qi, tq, first, low, span):
    rows = DEC_SEQ // GRID_W
    wh = min(NA_WIN_H, rows)
    neg = -jnp.inf
    for a in range(tq // GRID_W):
        qr = qi * (tq // GRID_W) + a
        rs = jnp.clip(qr - wh // 2, 0, rows - wh)
        for m in range(span // 2):
            kr0, kr1 = low + 2 * m, low + 2 * m + 1
            ok0 = jnp.logical_and(kr0 >= rs, kr0 < rs + wh)
            ok1 = jnp.logical_and(kr1 >= rs, kr1 < rs + wh)
            d0 = jnp.clip(kr0 - qr + NA_WIN_H - 1, 0, 2 * NA_WIN_H - 2)
            d1 = jnp.clip(kr1 - qr + NA_WIN_H - 1, 0, 2 * NA_WIN_H - 2)
            for hh in range(2):
                t0 = jnp.where(ok0, tab_ref[hh, d0], neg)
                t1 = jnp.where(ok1, tab_ref[hh, d1], neg)
                bias_scr[hh * tq + a * GRID_W:hh * tq + (a + 1) * GRID_W, m * PAIR:(m + 1) * PAIR] = (
                    jnp.where(first, t0, t1))


def _attn_kernel(has_ctx, has_bias, n_seq, *refs):
    it = iter(refs)
    q_ref, k_ref, v_ref = next(it), next(it), next(it)
    kc_ref, vc_ref = (next(it), next(it)) if has_ctx else (None, None)
    tab_ref = next(it) if has_bias else None
    o_ref = next(it)
    bias_scr = next(it) if has_bias else None

    first = _head_masks()
    tq = q_ref.shape[0] // n_seq
    tk = k_ref.shape[0] // n_seq
    n_half = q_ref.shape[1] // PAIR
    rows = 2 * n_half * tq
    sub = min(rows, ATTN_SUB_ROWS)

    if has_bias:
        span, lows = _na_key_rows(tq)
        qi = pl.program_id(1)
        low = sum(jnp.where(qi == n, l, 0) for n, l in enumerate(lows))
        key_rows = pl.ds(pl.multiple_of(low * GRID_W, GRID_W), span * GRID_W)

        @pl.when(pl.program_id(2) == 0)
        def _():
            _fill_na_bias(tab_ref, bias_scr, qi, tq, first, low, span)

    zero = jnp.zeros((tq, PAIR), BF16)
    qs, ks, vs = [], [], []
    for sq in range(n_seq):
        stacked = []
        for j in range(n_half):
            qj = q_ref[sq * tq:(sq + 1) * tq, j * PAIR:(j + 1) * PAIR]
            stacked += [jnp.where(first, qj, zero), jnp.where(first, zero, qj)]
        qs.append(jnp.concatenate(stacked, axis=0))
        if has_bias:
            ks.append(k_ref[key_rows, :])
            vs.append(v_ref[key_rows, :])
        else:
            ks.append(k_ref[sq * tk:(sq + 1) * tk, :])
            vs.append(v_ref[sq * tk:(sq + 1) * tk, :])
    kc = kc_ref[...].astype(BF16) if has_ctx else None
    vc = vc_ref[...].astype(BF16) if has_ctx else None

    def scores(sq, i):
        qi = qs[sq][i * sub:(i + 1) * sub]
        return _dot_nt(qi, ks[sq]), (_dot_nt(qi, kc) if has_ctx else None)

    def finish(sq, i, s, sc):
        if has_bias:
            s = s + bias_scr[i * sub:(i + 1) * sub, :]
        m = jnp.max(s, axis=-1, keepdims=True)
        if has_ctx:
            m = jnp.maximum(m, jnp.max(sc, axis=-1, keepdims=True))
        p = jnp.exp(s - m)
        l = jnp.sum(p, axis=-1, keepdims=True)
        o = _dot(p.astype(BF16), vs[sq])
        if has_ctx:
            pc = jnp.exp(sc - m)
            l = l + jnp.sum(pc, axis=-1, keepdims=True)
            o = o + _dot(pc.astype(BF16), vc)
        return o / l

    groups = [(sq, i) for sq in range(n_seq) for i in range(rows // sub)]
    outs = [[] for _ in range(n_seq)]
    cur = scores(*groups[0])
    for n, (sq, i) in enumerate(groups):
        nxt = scores(*groups[n + 1]) if n + 1 < len(groups) else None
        outs[sq].append(finish(sq, i, *cur))
        cur = nxt
    for sq in range(n_seq):
        o = jnp.concatenate(outs[sq], axis=0) if len(outs[sq]) > 1 else outs[sq][0]
        for j in range(n_half):
            o_ref[sq * tq:(sq + 1) * tq, j * PAIR:(j + 1) * PAIR] = jnp.where(
                first, o[2 * j * tq:(2 * j + 1) * tq], o[(2 * j + 1) * tq:(2 * j + 2) * tq]).astype(BF16)


def _attention(q, k, v, *, lat, tq, n_seq=1, layer_i=0, cache_k=None, cache_v=None, bias_tab=None):
    has_bias = bias_tab is not None
    n_pairs = k.shape[1] // PAIR
    n_half = q.shape[1] // k.shape[1]
    nb, t = (DEC_BATCH, DEC_SEQ) if lat else (BATCH, SEQ)
    nq = t // tq
    assert n_seq == 1 or (nq == 1 and not lat)
    n_keys = _na_key_rows(tq)[0] * GRID_W if has_bias else t
    in_specs = [
        pl.BlockSpec((n_seq * tq, n_half * PAIR), lambda p, qi, b: (b * nq + qi, p)),
        pl.BlockSpec((n_seq * t, PAIR), lambda p, qi, b: (b, p)),
        pl.BlockSpec((n_seq * t, PAIR), lambda p, qi, b: (b, p)),
    ]
    args = [q, k, v]
    if lat:
        in_specs += [pl.BlockSpec((None, None, PAST_LEN, PAIR), lambda p, qi, b: (b, layer_i, 0, p))] * 2
        args += [cache_k, cache_v]
    if has_bias:
        in_specs.append(pl.BlockSpec((2,) + bias_tab.shape[1:], lambda p, qi, b: (p, 0, 0, 0)))
        args.append(bias_tab)
    return pl.pallas_call(
        functools.partial(_attn_kernel, lat, has_bias, n_seq),
        grid=(n_pairs, nq, nb // n_seq),
        in_specs=in_specs,
        out_specs=pl.BlockSpec((n_seq * tq, n_half * PAIR), lambda p, qi, b: (b * nq + qi, p)),
        out_shape=jax.ShapeDtypeStruct(q.shape, BF16),
        scratch_shapes=[pltpu.VMEM((2 * tq, n_keys), F32)] if has_bias else [],
        compiler_params=_cparams(("parallel", "parallel", "arbitrary")),
        name="attn_" + ("lat" if lat else "ctx") + ("_a" if n_half == 2 else "_b"),
    )(*args)


def _rope_tables():
    half = HEAD_DIM // 2
    freqs = ROPE_THETA ** (-np.arange(0, half, 2, dtype=np.float32) / half)
    t = np.arange(DEC_SEQ)

    def tab(pos):
        ang = pos.astype(np.float32)[:, None] * freqs[None, :]
        cos = np.concatenate([np.cos(ang), np.cos(ang)], -1)
        sin = np.concatenate([-np.sin(ang), np.sin(ang)], -1)
        return cos, sin

    cr, sr = tab(t // GRID_W)
    cc, sc = tab(t % GRID_W)
    cos = np.concatenate([cr, cc] * 2, -1).astype(np.float32)
    sin = np.concatenate([sr, sc] * 2, -1).astype(np.float32)
    return jnp.asarray(cos), jnp.asarray(sin)


def _na_bias_table(rel_bias):
    w = GRID_W
    n_rel = 2 * NA_WIN_W - 1
    left = w - NA_WIN_W
    rp = jnp.pad(rel_bias, ((0, 0), (0, 0), (left, 2 * w - n_rel - left)))
    flat = jnp.tile(rp, (1, 1, w))
    m = flat[..., :w * (2 * w - 1)].reshape(rel_bias.shape[0], rel_bias.shape[1], w, 2 * w - 1)[..., w - 1:]
    c = np.arange(w)
    ws = np.clip(c - NA_WIN_W // 2, 0, w - NA_WIN_W)
    in_col = (c[None, :] >= ws[:, None]) & (c[None, :] < ws[:, None] + NA_WIN_W)
    m = jnp.where(jnp.asarray(in_col)[None, None], m, -jnp.inf)
    return jnp.concatenate([m, m], axis=-1)


def _chunk_scan(x, reverse):
    t = x.shape[0]
    pos = lax.broadcasted_iota(jnp.int32, x.shape, 0) % GLA_CHUNK
    s = 1
    while s < GLA_CHUNK:
        if reverse:
            x = x + jnp.where(pos < GLA_CHUNK - s, pltpu.roll(x, t - s, 0), 0.0)
        else:
            x = x + jnp.where(pos >= s, pltpu.roll(x, s, 0), 0.0)
        s *= 2
    return x


def _gla_kernel(has_s0, emit_s, heads, *refs):
    it = iter(refs)
    q_ref, k_ref, v_ref, lgf_ref, lgb_ref = (next(it) for _ in range(5))
    s0_ref = next(it) if has_s0 else None
    if emit_s:
        next(it)
    o_ref = next(it)
    sf_ref = next(it) if emit_s else None
    qf_scr, qb_scr, kin_scr, kout_scr, bf_scr, bb_scr, a_scr, u_scr, st_scr, o_scr = (next(it) for _ in range(10))

    c_len, dk, dv = GLA_CHUNK, GLA_DK, GLA_DV
    t = q_ref.shape[0]
    n = t // c_len
    scale = dk ** -0.5
    row = lax.broadcasted_iota(jnp.int32, (c_len, c_len), 0)
    col = lax.broadcasted_iota(jnp.int32, (c_len, c_len), 1)
    zero = jnp.zeros((t, dk), BF16)

    def rows(c):
        return pl.ds(pl.multiple_of(c * c_len, c_len), c_len)

    def one_head(hd, kl, vl):
        lgf, lgb = lgf_ref[:, kl], lgb_ref[:, kl]
        bf = _chunk_scan(lgf, False)
        bb = _chunk_scan(lgb, True)
        q = q_ref[:, kl].astype(F32) * scale
        k = k_ref[:, kl].astype(F32)
        qf_scr[:, :dk] = (q * jnp.exp(bf)).astype(BF16)
        qf_scr[:, dk:] = zero
        qb_scr[:, :dk] = zero
        qb_scr[:, dk:] = (q * jnp.exp(bb)).astype(BF16)
        kin_scr[:, :dk] = (k * jnp.exp(-bf)).astype(BF16)
        kin_scr[:, dk:] = (k * jnp.exp(-bb)).astype(BF16)
        bf3, bb3 = bf.reshape(n, c_len, dk), bb.reshape(n, c_len, dk)
        kout_scr[:, :dk] = (k * jnp.exp((bf3[:, c_len - 1:, :] - bf3).reshape(t, dk))).astype(BF16)
        kout_scr[:, dk:] = (k * jnp.exp((bb3[:, :1, :] - bb3).reshape(t, dk))).astype(BF16)
        bf_scr[...] = bf
        bb_scr[...] = bb

        def score_pass(c, _):
            r = rows(c)
            a2 = _dot_nt(jnp.concatenate([qf_scr[r, :], qb_scr[r, :]], axis=0), kin_scr[r, :])
            a = jnp.where(col <= row, a2[:c_len], 0.0) + jnp.where(col >= row, a2[c_len:], 0.0)
            a_scr[r, :] = a.astype(BF16)
            u = _dot_tn(v_ref[r, vl], kout_scr[r, :])
            u_scr[c, :, :dk] = u[:, :dk]
            u_scr[n - 1 - c, :, dk:] = u[:, dk:]
            return _

        def value_pass(c, _):
            r = rows(c)
            o_scr[r, :] = _dot(a_scr[r, :], v_ref[r, vl])
            return _

        lax.fori_loop(0, n, score_pass, None, unroll=GLA_UNROLL)
        lax.fori_loop(0, n, value_pass, None, unroll=GLA_UNROLL)

        if has_s0:
            st_scr[:, :dk] = s0_ref[0, hd].T
            st_scr[:, dk:] = s0_ref[1, hd].T
        else:
            st_scr[...] = jnp.zeros_like(st_scr)

        def sweep(i, _):
            rf, rb = rows(i), rows(n - 1 - i)
            st = st_scr[...]
            res = _dot_nt(jnp.concatenate([qf_scr[rf, :], qb_scr[rb, :]], axis=0), st.astype(BF16))
            o_scr[rf, :] += res[:c_len]
            o_scr[rb, :] += res[c_len:]
            b_edge = jnp.concatenate([bf_scr[pl.ds(i * c_len + c_len - 1, 1), :],
                                      bb_scr[pl.ds((n - 1 - i) * c_len, 1), :]], axis=1)
            st_scr[...] = st * jnp.exp(b_edge) + u_scr[i]
            return _

        lax.fori_loop(0, n, sweep, None, unroll=GLA_UNROLL)
        o_ref[:, vl] = o_scr[...].astype(o_ref.dtype)

        if emit_s:
            sf_ref[0, hd] = st_scr[:, :dk].T
            sf_ref[1, hd] = st_scr[:, dk:].T

    if heads == 1:
        one_head(0, slice(0, dk), slice(0, dv))
    else:
        def head_step(hd, carry):
            one_head(hd, pl.ds(pl.multiple_of(hd * dk, dk), dk), pl.ds(pl.multiple_of(hd * dv, dv), dv))
            return carry

        lax.fori_loop(0, heads, head_step, None)


def _gla(q, k, v, lgf, lgb, *, lat, heads, layer_j=0, state=None, new_state=None):
    if lat:
        nb, t, row0 = DEC_BATCH, DEC_SEQ, N_CTX
    else:
        nb, t, row0 = BATCH, SEQ, 0
    off = row0 // t
    kspec = pl.BlockSpec((t, heads * GLA_DK), lambda b, h: (off + b, h))
    in_specs = [kspec, kspec, pl.BlockSpec((t, heads * GLA_DV), lambda b, h: (off + b, h)), kspec, kspec]
    args = [q, k, v, lgf, lgb]
    state_spec = pl.BlockSpec((None, None, 2, heads, GLA_DK, GLA_DV), lambda b, h: (b, layer_j, 0, h, 0, 0))
    if lat:
        in_specs.append(state_spec)
        args.append(state)
    out_specs = [pl.BlockSpec((t, heads * GLA_DV), lambda b, h: (b, h))]
    out_shape = [jax.ShapeDtypeStruct((nb * t, GLA_VDIM), BF16)]
    aliases = {}
    if not lat:
        aliases[len(args)] = 1
        in_specs.append(pl.BlockSpec(memory_space=pl.ANY))
        args.append(new_state)
        out_specs.append(state_spec)
        out_shape.append(jax.ShapeDtypeStruct(new_state.shape, new_state.dtype))
    return pl.pallas_call(
        functools.partial(_gla_kernel, lat, not lat, heads),
        grid=(nb, GLA_HEADS // heads),
        in_specs=in_specs,
        out_specs=out_specs,
        out_shape=out_shape,
        input_output_aliases=aliases,
        scratch_shapes=[
            pltpu.VMEM((t, 2 * GLA_DK), BF16),
            pltpu.VMEM((t, 2 * GLA_DK), BF16),
            pltpu.VMEM((t, 2 * GLA_DK), BF16),
            pltpu.VMEM((t, 2 * GLA_DK), BF16),
            pltpu.VMEM((t, GLA_DK), F32),
            pltpu.VMEM((t, GLA_DK), F32),
            pltpu.VMEM((t, GLA_CHUNK), BF16),
            pltpu.VMEM((t // GLA_CHUNK, GLA_DV, 2 * GLA_DK), F32),
            pltpu.VMEM((GLA_DV, 2 * GLA_DK), F32),
            pltpu.VMEM((t, GLA_DV), F32),
        ],
        compiler_params=_cparams(("parallel", "parallel")),
        name="gla_lat" if lat else "gla_ctx",
    )(*args)


def _group_pick(tm, c_ref, l_ref):
    return jnp.where(pl.program_id(0) < N_CTX // tm, c_ref[...], l_ref[...])


def _attn_out_kernel(tm, x_ref, mod_ref, oac_ref, oal_ref, obc_ref, obl_ref, w_ref, o_ref):
    oa = _group_pick(tm, oac_ref, oal_ref)
    ob = _group_pick(tm, obc_ref, obl_ref)
    na = oa.shape[1]
    y = _dot(oa, w_ref[:na, :]) + _dot(ob, w_ref[na:, :])
    o_ref[...] = x_ref[...] + mod_ref[2:3, :] * y


def _gla_out_kernel(tm, x_ref, mod_ref, oc_ref, ol_ref, gate_ref, gn_ref, w_ref, o_ref):
    gn = gn_ref[...]
    o_pre = _group_pick(tm, oc_ref, ol_ref).astype(F32)
    parts = []
    for h in range(GLA_HEADS):
        sl = slice(h * GLA_DV, (h + 1) * GLA_DV)
        gt = gate_ref[:, sl].astype(F32)
        parts.append((_rms(o_pre[:, sl], gn) * (gt * jax.nn.sigmoid(gt))).astype(BF16))
    y = _dot(jnp.concatenate(parts, axis=-1), w_ref[...])
    o_ref[...] = x_ref[...] + mod_ref[2:3, :] * y


def _mixer_out(kernel, name, x, mod, layer, group_acts, acts, consts):
    tm = TM_OUT
    nct = N_CTX // tm
    in_specs = [
        pl.BlockSpec((tm, D_MODEL), lambda i: (i, 0)),
        pl.BlockSpec((None, None, None, 3, D_MODEL), lambda i: (layer, _cond_row(i, tm), 1, 0, 0)),
    ]
    flat = []
    for a_c, a_l in group_acts:
        in_specs.append(pl.BlockSpec((tm, a_c.shape[1]), lambda i: (jnp.minimum(i, nct - 1), 0)))
        in_specs.append(pl.BlockSpec((tm, a_l.shape[1]), lambda i: (jnp.maximum(i - nct, 0), 0)))
        flat += [a_c, a_l]
    in_specs += [pl.BlockSpec((tm, a.shape[1]), lambda i: (i, 0)) for a in acts]
    in_specs += [pl.BlockSpec(c.shape, lambda i: (0, 0)) for c in consts]
    return pl.pallas_call(
        functools.partial(kernel, tm),
        grid=(N_TOK // tm,),
        in_specs=in_specs,
        out_specs=pl.BlockSpec((tm, D_MODEL), lambda i: (i, 0)),
        out_shape=jax.ShapeDtypeStruct((N_TOK, D_MODEL), F32),
        compiler_params=_cparams(("parallel",)),
        name=name,
    )(x, mod, *flat, *acts, *consts)


def _final_kernel(x_ref, g_ref, o_ref):
    o_ref[...] = _rms(x_ref[...], g_ref[...])


def _final_norm(x, g, row0, n_rows):
    tm = TM_OUT
    off = row0 // tm
    return pl.pallas_call(
        _final_kernel,
        grid=(n_rows // tm,),
        in_specs=[pl.BlockSpec((tm, D_MODEL), lambda i: (off + i, 0)),
                  pl.BlockSpec((1, D_MODEL), lambda i: (0, 0))],
        out_specs=pl.BlockSpec((tm, D_MODEL), lambda i: (i, 0)),
        out_shape=jax.ShapeDtypeStruct((n_rows, D_MODEL), F32),
        compiler_params=_cparams(("parallel",)),
        name="final_norm",
    )(x, g.reshape(1, D_MODEL))


def _permute_heads(w, axis):
    grp = A_HEADS // A_KV_HEADS
    order = [(2 * p + j) * grp + g for p in range(A_KV_HEADS // 2) for g in range(grp) for j in range(2)]
    rest = tuple(np.delete(w.shape, axis))
    w = jnp.moveaxis(w, axis, 0).reshape((A_HEADS, HEAD_DIM) + rest)
    w = w[np.asarray(order)].reshape((A_HEADS * HEAD_DIM,) + rest)
    return jnp.moveaxis(w, 0, axis)


def kernel(x_prompt, x_sample, cache_a_k, cache_a_v, cache_b_k, cache_b_v, state_gla, c, c_ctx,
           w_mod, b_mod, g_norm, w_ffn_gate, w_ffn_up, w_ffn_down, w_attn_in, w_attn_out,
           g_qnorm, g_knorm, na_rel_bias, w_gla_in, w_gla_gup, b_gla_gup, g_gla_norm, w_gla_out, g_final):
    x = (x_prompt.reshape(N_CTX, D_MODEL), x_sample.reshape(N_LAT, D_MODEL))
    cond = jnp.concatenate([c_ctx[None, :], c, jnp.zeros((COND_ROWS - 1 - DEC_BATCH, D_MODEL), F32)], axis=0)
    mod = _modulation(cond, w_mod, b_mod)

    wg_all, wu_all, wd_all = (w.astype(BF16) for w in (w_ffn_gate, w_ffn_up, w_ffn_down))
    rope_tabs = _rope_tables()
    n_attn = w_attn_in.shape[0]
    ck_a = cache_a_k.reshape(DEC_BATCH, n_attn, PAST_LEN, A_KV_HEADS * HEAD_DIM)
    cv_a = cache_a_v.reshape(DEC_BATCH, n_attn, PAST_LEN, A_KV_HEADS * HEAD_DIM)
    ck_b = cache_b_k.reshape(DEC_BATCH, n_attn, PAST_LEN, B_HEADS * HEAD_DIM)
    cv_b = cache_b_v.reshape(DEC_BATCH, n_attn, PAST_LEN, B_HEADS * HEAD_DIM)

    caches = [jnp.zeros((BATCH, n_attn, SEQ, ATTN_WIDTHS[s]), F32) for s in (1, 2, 4, 5)]
    new_gla = jnp.zeros((BATCH, w_gla_in.shape[0], 2, GLA_HEADS, GLA_DK, GLA_DV), F32)
    for l in range(DEPTH):
        x = _ffn(x, mod, l, 0, g_norm[l, 0], wg_all, wu_all, wd_all)
        if l % 2 == 0:
            i = l // 2
            gq = jnp.tile(g_qnorm[i], 2).reshape(1, PAIR)
            gk = jnp.tile(g_knorm[i], 2).reshape(1, PAIR)
            n_qa = A_HEADS * HEAD_DIM
            w_in = w_attn_in[i].at[:, :n_qa].set(_permute_heads(w_attn_in[i][:, :n_qa], axis=1)).astype(BF16)
            w_out = w_attn_out[i].at[:n_qa].set(_permute_heads(w_attn_out[i][:n_qa], axis=0)).astype(BF16)
            qa_c, ka_c, va_c, qb_c, kb_c, vb_c, *caches = _attn_in(
                x, mod, l, g_norm[l, 1], w_in, gq, gk, lat=False, layer_i=i, caches=caches)
            qa_l, ka_l, va_l, qb_l, kb_l, vb_l = _attn_in(
                x, mod, l, g_norm[l, 1], w_in, gq, gk, lat=True, rope_tabs=rope_tabs)
            oa_c = _attention(qa_c, ka_c, va_c, lat=False, tq=SEQ, n_seq=CTX_SEQS_A)
            ob_c = _attention(qb_c, kb_c, vb_c, lat=False, tq=SEQ, n_seq=CTX_SEQS_B)
            oa_l = _attention(qa_l, ka_l, va_l, lat=True, tq=TQ_LAT_A, layer_i=i, cache_k=ck_a, cache_v=cv_a)
            ob_l = _attention(qb_l, kb_l, vb_l, lat=True, tq=TQ_LAT_B, layer_i=i, cache_k=ck_b, cache_v=cv_b,
                              bias_tab=_na_bias_table(na_rel_bias[i]))
            x = _mixer_out(_attn_out_kernel, "attn_out", x, mod, l, [(oa_c, oa_l), (ob_c, ob_l)], [], [w_out])
        else:
            j = l // 2
            w_in = jnp.pad(w_gla_in[j], ((0, 0), (0, GLA_D_PAD - 2 * GLA_RANK))).astype(BF16)
            w_gup = jnp.zeros((GLA_D_PAD, 2 * GLA_KDIM), F32)
            w_gup = w_gup.at[:GLA_RANK, :GLA_KDIM].set(w_gla_gup[j, 0])
            w_gup = w_gup.at[GLA_RANK:2 * GLA_RANK, GLA_KDIM:].set(w_gla_gup[j, 1]).astype(BF16)
            b_gup = b_gla_gup[j].reshape(1, 2 * GLA_KDIM)
            q, k, v, gt, lgf, lgb = _gla_in(x, mod, l, g_norm[l, 1], w_in, w_gup, b_gup)
            o_c, new_gla = _gla(q, k, v, lgf, lgb, lat=False, heads=GLA_HEADS, layer_j=j, new_state=new_gla)
            (o_l,) = _gla(q, k, v, lgf, lgb, lat=True, heads=1, layer_j=j, state=state_gla)
            x = _mixer_out(_gla_out_kernel, "gla_out", x, mod, l, [(o_c, o_l)], [gt],
                           [g_gla_norm[j].reshape(1, GLA_DV), w_gla_out[j].astype(BF16)])
        x = _ffn(x, mod, l, 2, g_norm[l, 2], wg_all, wu_all, wd_all)

    y_ctx = _final_norm(x, g_final, 0, N_CTX)
    y_lat = _final_norm(x, g_final, N_CTX, N_LAT)
    return (y_ctx.reshape(BATCH, SEQ, D_MODEL), y_lat.reshape(DEC_BATCH, DEC_SEQ, D_MODEL),
            caches[0].reshape(BATCH, n_attn, SEQ, A_KV_HEADS, HEAD_DIM),
            caches[1].reshape(BATCH, n_attn, SEQ, A_KV_HEADS, HEAD_DIM),
            caches[2].reshape(BATCH, n_attn, SEQ, B_HEADS, HEAD_DIM),
            caches[3].reshape(BATCH, n_attn, SEQ, B_HEADS, HEAD_DIM), new_gla)
```

```python
import functools

import numpy as np
import jax
import jax.numpy as jnp
from jax import lax
from jax.experimental import pallas as pl
from jax.experimental.pallas import tpu as pltpu

D_MODEL = 1024
BATCH = 32
SEQ = 256
DEPTH = 4
DEC_BATCH = 8
DEC_SEQ = 1024
PAST_LEN = 512
GRID_W = 64
HEAD_DIM = 64
A_HEADS = 8
A_KV_HEADS = 4
B_HEADS = 8
ROPE_THETA = 10000.0
NA_WIN_H = 8
NA_WIN_W = 16
GLA_HEADS = 4
GLA_DK = D_MODEL // 2 // GLA_HEADS
GLA_DV = D_MODEL // GLA_HEADS
GLA_KDIM = GLA_HEADS * GLA_DK
GLA_VDIM = GLA_HEADS * GLA_DV
GLA_RANK = 16
GLA_GATE_NORM = 16.0
GLA_CHUNK = 64
D_FF = ((8 * D_MODEL // 3 + 127) // 128) * 128
FFN_RES = 0.5
N_MOD = 9
EPS = 1e-6

N_CTX = BATCH * SEQ
N_LAT = DEC_BATCH * DEC_SEQ
N_TOK = N_CTX + N_LAT
COND_ROWS = 16
LANES = 128
PAIR = 2 * HEAD_DIM
GLA_D_PAD = LANES
VMEM_LIMIT = 56 * 1024 * 1024

F32 = jnp.float32
BF16 = jnp.bfloat16

TM_FFN = 1024
TF_FFN = 256
FFN_CHUNKS = D_FF // TF_FFN
FFN_ROW_BLOCK = 256
assert FFN_CHUNKS * TF_FFN == D_FF and FFN_CHUNKS % 2 == 1
TM_PROJ = 512
TM_OUT = 1024
TM_GLA_IN = 1024
TQ_LAT_A = 256
TQ_LAT_B = 512
CTX_SEQS_A = 2
CTX_SEQS_B = 4
ATTN_SUB_ROWS = 512
GLA_UNROLL = 8


def _cparams(sem):
    return pltpu.CompilerParams(dimension_semantics=sem, vmem_limit_bytes=VMEM_LIMIT)


def _cond_row(i, tm):
    nct = N_CTX // tm
    per_batch = DEC_SEQ // tm
    return jnp.where(i < nct, 0, 1 + (i - nct) // per_batch)


def _rms(x, g):
    ms = jnp.mean(x * x, axis=-1, keepdims=True)
    return x * lax.rsqrt(ms + EPS) * g


def _dot(a, b):
    return jnp.dot(a, b, preferred_element_type=F32)


def _dot_nt(a, b):
    return lax.dot_general(a, b, (((1,), (1,)), ((), ())), preferred_element_type=F32)


def _dot_tn(a, b):
    return lax.dot_general(a, b, (((0,), (0,)), ((), ())), preferred_element_type=F32)


def _split_bf16(x):
    hi = x.astype(BF16)
    return hi, (x - hi.astype(F32)).astype(BF16)


def _mod_kernel(c_ref, w_ref, b_ref, o_ref):
    c = c_ref[...]
    s_hi, s_lo = _split_bf16(c * jax.nn.sigmoid(c))
    w_hi, w_lo = _split_bf16(w_ref[...])
    o_ref[...] = _dot(s_hi, w_hi) + (_dot(s_lo, w_hi) + _dot(s_hi, w_lo)) + b_ref[...]


def _modulation(cond, w_mod, b_mod):
    tn = D_MODEL
    n_mod = N_MOD * D_MODEL
    out = pl.pallas_call(
        _mod_kernel,
        grid=(DEPTH, n_mod // tn),
        in_specs=[
            pl.BlockSpec((COND_ROWS, D_MODEL), lambda l, j: (0, 0)),
            pl.BlockSpec((None, D_MODEL, tn), lambda l, j: (l, 0, j)),
            pl.BlockSpec((None, 1, tn), lambda l, j: (l, 0, j)),
        ],
        out_specs=pl.BlockSpec((None, COND_ROWS, tn), lambda l, j: (l, 0, j)),
        out_shape=jax.ShapeDtypeStruct((DEPTH, COND_ROWS, n_mod), F32),
        compiler_params=_cparams(("parallel", "parallel")),
        name="adaln_mod",
    )(cond, w_mod, b_mod.reshape(DEPTH, 1, n_mod))
    return out.reshape(DEPTH, COND_ROWS, 3, 3, D_MODEL)


def _ffn_kernel(split_x, *refs):
    if split_x:
        xc_ref, xl_ref, *refs = refs
    else:
        x_ref, *refs = refs
    mod_ref, g_ref, wg_ref, wu_ref, wd_ref, o_ref, h_ref, acc_ref, g_scr, u_scr = refs
    tm = o_ref.shape[0]

    def x_rows(rs):
        if split_x:
            return jnp.where(pl.program_id(0) < N_CTX // tm, xc_ref[rs, :], xl_ref[rs, :])
        return x_ref[rs, :]

    def cols(j):
        return pl.ds(pl.multiple_of(j * TF_FFN, TF_FFN), TF_FFN)

    def gate_up(h, j):
        return _dot(h, wg_ref[:, cols(j)]), _dot(h, wu_ref[:, cols(j)])

    def down(g, u, j):
        a = (g * jax.nn.sigmoid(g) * u).astype(BF16)
        return _dot(a, wd_ref[cols(j), :])

    row_blocks = [slice(r, r + FFN_ROW_BLOCK) for r in range(0, tm, FFN_ROW_BLOCK)]
    gain = g_ref[...] * (1.0 + mod_ref[1:2, :])
    for rs in row_blocks:
        x = x_rows(rs)
        ms = jnp.mean(x * x, axis=-1, keepdims=True)
        h = (x * lax.rsqrt(ms + EPS) * gain + mod_ref[0:1, :]).astype(BF16)
        h_ref[rs, :] = h
        g_scr[rs, :], u_scr[rs, :] = gate_up(h, 0)
    acc_ref[...] = jnp.zeros_like(acc_ref)

    def chunk_pair(jj, carry):
        j = 2 * jj
        h = h_ref[...]
        g1, u1 = gate_up(h, j + 1)
        acc_ref[...] += down(g_scr[...], u_scr[...], j)
        g2, u2 = gate_up(h, j + 2)
        acc_ref[...] += down(g1, u1, j + 1)
        g_scr[...], u_scr[...] = g2, u2
        return carry

    lax.fori_loop(0, FFN_CHUNKS // 2, chunk_pair, None, unroll=True)
    res_gate = FFN_RES * mod_ref[2:3, :]
    for rs in row_blocks:
        y = acc_ref[rs, :] + down(g_scr[rs, :], u_scr[rs, :], FFN_CHUNKS - 1)
        o_ref[rs, :] = x_rows(rs) + res_gate * y


def _ffn(x, mod, layer, sub, g, wg, wu, wd):
    tm = TM_FFN
    nct = N_CTX // tm
    which = sub // 2
    resident = lambda w: pl.BlockSpec((None, None) + w.shape[2:], lambda i: (layer, which, 0, 0),
                                      pipeline_mode=pl.Buffered(1))
    split_x = isinstance(x, tuple)
    if split_x:
        x_specs = [pl.BlockSpec((tm, D_MODEL), lambda i: (jnp.minimum(i, nct - 1), 0)),
                   pl.BlockSpec((tm, D_MODEL), lambda i: (jnp.maximum(i - nct, 0), 0))]
        xs = list(x)
    else:
        x_specs = [pl.BlockSpec((tm, D_MODEL), lambda i: (i, 0))]
        xs = [x]
    return pl.pallas_call(
        functools.partial(_ffn_kernel, split_x),
        grid=(N_TOK // tm,),
        in_specs=x_specs + [
            pl.BlockSpec((None, None, None, 3, D_MODEL), lambda i: (layer, _cond_row(i, tm), sub, 0, 0)),
            pl.BlockSpec((1, D_MODEL), lambda i: (0, 0)),
            resident(wg), resident(wu), resident(wd),
        ],
        out_specs=pl.BlockSpec((tm, D_MODEL), lambda i: (i, 0)),
        out_shape=jax.ShapeDtypeStruct((N_TOK, D_MODEL), F32),
        scratch_shapes=[pltpu.VMEM((tm, D_MODEL), BF16), pltpu.VMEM((tm, D_MODEL), F32),
                        pltpu.VMEM((tm, TF_FFN), F32), pltpu.VMEM((tm, TF_FFN), F32)],
        compiler_params=_cparams(("parallel",)),
        name="ffn",
    )(*xs, mod, g.reshape(1, D_MODEL), wg, wu, wd)


def _mixer_h(x_ref, mod_ref, g_ref):
    y = _rms(x_ref[...], g_ref[...])
    return (y * (1.0 + mod_ref[1:2, :]) + mod_ref[0:1, :]).astype(BF16)


ATTN_WIDTHS = (A_HEADS * HEAD_DIM, A_KV_HEADS * HEAD_DIM, A_KV_HEADS * HEAD_DIM,
               B_HEADS * HEAD_DIM, B_HEADS * HEAD_DIM, B_HEADS * HEAD_DIM)


def _attn_in_kernel(lat, x_ref, mod_ref, g_ref, w_ref, gq_ref, gk_ref, *refs):
    if lat:
        cos_ref, sin_ref, *o_refs = refs
    else:
        o_refs = refs[4:]
    h = _mixer_h(x_ref, mod_ref, g_ref)
    nb = x_ref.shape[0] // SEQ
    first = _head_masks()
    scale = HEAD_DIM ** -0.5
    offs = np.cumsum((0,) + ATTN_WIDTHS)

    def seg(i):
        return _dot(h, w_ref[:, offs[i]:offs[i + 1]])

    def normed_pairs(y, g2):
        for p in range(y.shape[1] // PAIR):
            z = _pair_rms(y[:, p * PAIR:(p + 1) * PAIR], g2, first)
            if lat:
                z = _pair_rope(z, cos_ref[...], sin_ref[...])
            yield slice(p * PAIR, (p + 1) * PAIR), z

    for sl, z in normed_pairs(seg(0), gq_ref[...]):
        o_refs[0][:, sl] = (z * scale).astype(BF16)
    for sl, z in normed_pairs(seg(1), gk_ref[...]):
        o_refs[1][:, sl] = z.astype(BF16)
        if not lat:
            o_refs[6][:, :, sl] = z.reshape(nb, SEQ, PAIR)
    for i in (2, 3, 4, 5):
        y = seg(i)
        o_refs[i][...] = ((y * scale) if i == 3 else y).astype(BF16)
        if not lat and i != 3:
            o_refs[{2: 7, 4: 8, 5: 9}[i]][...] = y.reshape(nb, SEQ, y.shape[1])


def _gla_in_kernel(widths, x_ref, mod_ref, g_ref, w_ref, wgup_ref, bgup_ref, *o_refs):
    h = _mixer_h(x_ref, mod_ref, g_ref)
    offs = np.cumsum((0,) + tuple(widths))

    def project(i, lo=0, hi=None):
        hi = widths[i] if hi is None else hi
        o_refs[i][:, lo:hi] = _dot(h, w_ref[:, offs[i] + lo:offs[i] + hi]).astype(o_refs[i].dtype)

    d = _dot(h, w_ref[:, offs[4]:offs[4] + GLA_D_PAD]).astype(BF16)
    project(0)
    pre = _dot(d, wgup_ref[...]) + bgup_ref[...]
    project(1)
    lg = (jnp.minimum(pre, 0.0) - jnp.log1p(jnp.exp(-jnp.abs(pre)))) * (1.0 / GLA_GATE_NORM)
    o_refs[4][...] = lg[:, :GLA_KDIM]
    o_refs[5][...] = lg[:, GLA_KDIM:]
    for i in (2, 3):
        half = widths[i] // 2
        project(i, 0, half)
        project(i, half, widths[i])


def _mixer_in_specs(layer, tm, n_in, tile0=0):
    return [
        pl.BlockSpec((tm, D_MODEL), lambda i: (tile0 + i, 0)),
        pl.BlockSpec((None, None, None, 3, D_MODEL), lambda i: (layer, _cond_row(tile0 + i, tm), 1, 0, 0)),
        pl.BlockSpec((1, D_MODEL), lambda i: (0, 0)),
        pl.BlockSpec((D_MODEL, n_in), lambda i: (0, 0)),
    ]


def _attn_in(x, mod, layer, g, w_in, gq, gk, *, lat, rope_tabs=None, layer_i=0, caches=None):
    tm = TM_PROJ
    n_rows, tile0 = (N_LAT, N_CTX // tm) if lat else (N_CTX, 0)
    in_specs = _mixer_in_specs(layer, tm, sum(ATTN_WIDTHS), tile0)
    in_specs += [pl.BlockSpec((1, PAIR), lambda i: (0, 0))] * 2
    args = [x, mod, g.reshape(1, D_MODEL), w_in, gq, gk]
    out_specs = [pl.BlockSpec((tm, wd), lambda i: (i, 0)) for wd in ATTN_WIDTHS]
    out_shape = [jax.ShapeDtypeStruct((n_rows, wd), BF16) for wd in ATTN_WIDTHS]
    aliases = {}
    if lat:
        per_seq = DEC_SEQ // tm
        in_specs += [pl.BlockSpec((tm, PAIR), lambda i: (i % per_seq, 0))] * 2
        args += list(rope_tabs)
    else:
        for c in caches:
            aliases[len(args)] = len(out_specs)
            in_specs.append(pl.BlockSpec(memory_space=pl.ANY))
            args.append(c)
            out_specs.append(pl.BlockSpec((tm // SEQ, None, SEQ, c.shape[-1]), lambda i: (i, layer_i, 0, 0)))
            out_shape.append(jax.ShapeDtypeStruct(c.shape, c.dtype))
    return pl.pallas_call(
        functools.partial(_attn_in_kernel, lat),
        grid=(n_rows // tm,),
        in_specs=in_specs,
        out_specs=out_specs,
        out_shape=out_shape,
        input_output_aliases=aliases,
        compiler_params=_cparams(("parallel",)),
        name="attn_in_lat" if lat else "attn_in_ctx",
    )(*args)


def _gla_in(x, mod, layer, g, w_in, w_gup, b_gup):
    tm = TM_GLA_IN
    widths = (GLA_KDIM, GLA_KDIM, GLA_VDIM, GLA_VDIM)
    out_w = widths + (GLA_KDIM, GLA_KDIM)
    out_dt = (BF16, BF16, BF16, BF16, F32, F32)
    n_in = sum(widths) + GLA_D_PAD
    return pl.pallas_call(
        functools.partial(_gla_in_kernel, widths),
        grid=(N_TOK // tm,),
        in_specs=_mixer_in_specs(layer, tm, n_in) + [
            pl.BlockSpec((GLA_D_PAD, 2 * GLA_KDIM), lambda i: (0, 0)),
            pl.BlockSpec((1, 2 * GLA_KDIM), lambda i: (0, 0)),
        ],
        out_specs=[pl.BlockSpec((tm, wd), lambda i: (i, 0)) for wd in out_w],
        out_shape=[jax.ShapeDtypeStruct((N_TOK, wd), dt) for wd, dt in zip(out_w, out_dt)],
        compiler_params=_cparams(("parallel",)),
        name="gla_in",
    )(x, mod, g.reshape(1, D_MODEL), w_in, w_gup, b_gup)


def _head_masks():
    lane = lax.broadcasted_iota(jnp.int32, (1, PAIR), 1)
    return lane < HEAD_DIM


def _pair_rms(x, g2, first):
    sq = x * x
    s0 = jnp.sum(jnp.where(first, sq, 0.0), axis=-1, keepdims=True)
    s1 = jnp.sum(jnp.where(first, 0.0, sq), axis=-1, keepdims=True)
    ms = jnp.where(first, s0, s1) * (1.0 / HEAD_DIM)
    return x * lax.rsqrt(ms + EPS) * g2


def _pair_rope(x, cos, sin_signed):
    lane = lax.broadcasted_iota(jnp.int32, x.shape, 1)
    quarter = HEAD_DIM // 4
    partner = jnp.where(lane % (2 * quarter) < quarter,
                        pltpu.roll(x, PAIR - quarter, 1), pltpu.roll(x, quarter, 1))
    return x * cos + partner * sin_signed


def _na_key_rows(tq):
    rows = DEC_SEQ // GRID_W
    wh = min(NA_WIN_H, rows)
    r = tq // GRID_W
    start = lambda qr: max(0, min(qr - wh // 2, rows - wh))
    lo = [start(qi * r) for qi in range(rows // r)]
    hi = [start(qi * r + r - 1) + wh for qi in range(rows // r)]
    span = max(h - l for l, h in zip(lo, hi))
    span += span % 2
    return span, [min(l, rows - span) for l in lo]


def _fill_na_bias(tab_ref, bias_scr, qi, tq, first, low, span):
    rows = DEC_SEQ // GRID_W
    wh = min(NA_WIN_H, rows)
    neg = -jnp.inf
    for a in range(tq // GRID_W):
        qr = qi * (tq // GRID_W) + a
        rs = jnp.clip(qr - wh // 2, 0, rows - wh)
        for m in range(span // 2):
            kr0, kr1 = low + 2 * m, low + 2 * m + 1
            ok0 = jnp.logical_and(kr0 >= rs, kr0 < rs + wh)
            ok1 = jnp.logical_and(kr1 >= rs, kr1 < rs + wh)
            d0 = jnp.clip(kr0 - qr + NA_WIN_H - 1, 0, 2 * NA_WIN_H - 2)
            d1 = jnp.clip(kr1 - qr + NA_WIN_H - 1, 0, 2 * NA_WIN_H - 2)
            for hh in range(2):
                t0 = jnp.where(ok0, tab_ref[hh, d0], neg)
                t1 = jnp.where(ok1, tab_ref[hh, d1], neg)
                bias_scr[hh * tq + a * GRID_W:hh * tq + (a + 1) * GRID_W, m * PAIR:(m + 1) * PAIR] = (
                    jnp.where(first, t0, t1))


def _attn_kernel(has_ctx, has_bias, n_seq, *refs):
    it = iter(refs)
    q_ref, k_ref, v_ref = next(it), next(it), next(it)
    kc_ref, vc_ref = (next(it), next(it)) if has_ctx else (None, None)
    tab_ref = next(it) if has_bias else None
    o_ref = next(it)
    bias_scr = next(it) if has_bias else None

    first = _head_masks()
    tq = q_ref.shape[0] // n_seq
    tk = k_ref.shape[0] // n_seq
    n_half = q_ref.shape[1] // PAIR
    rows = 2 * n_half * tq
    sub = min(rows, ATTN_SUB_ROWS)

    if has_bias:
        span, lows = _na_key_rows(tq)
        qi = pl.program_id(1)
        low = sum(jnp.where(qi == n, l, 0) for n, l in enumerate(lows))
        key_rows = pl.ds(pl.multiple_of(low * GRID_W, GRID_W), span * GRID_W)

        @pl.when(pl.program_id(2) == 0)
        def _():
            _fill_na_bias(tab_ref, bias_scr, qi, tq, first, low, span)

    zero = jnp.zeros((tq, PAIR), BF16)
    qs, ks, vs = [], [], []
    for sq in range(n_seq):
        stacked = []
        for j in range(n_half):
            qj = q_ref[sq * tq:(sq + 1) * tq, j * PAIR:(j + 1) * PAIR]
            stacked += [jnp.where(first, qj, zero), jnp.where(first, zero, qj)]
        qs.append(jnp.concatenate(stacked, axis=0))
        if has_bias:
            ks.append(k_ref[key_rows, :])
            vs.append(v_ref[key_rows, :])
        else:
            ks.append(k_ref[sq * tk:(sq + 1) * tk, :])
            vs.append(v_ref[sq * tk:(sq + 1) * tk, :])
    kc = kc_ref[...].astype(BF16) if has_ctx else None
    vc = vc_ref[...].astype(BF16) if has_ctx else None

    def scores(sq, i):
        qi = qs[sq][i * sub:(i + 1) * sub]
        return _dot_nt(qi, ks[sq]), (_dot_nt(qi, kc) if has_ctx else None)

    def finish(sq, i, s, sc):
        if has_bias:
            s = s + bias_scr[i * sub:(i + 1) * sub, :]
        m = jnp.max(s, axis=-1, keepdims=True)
        if has_ctx:
            m = jnp.maximum(m, jnp.max(sc, axis=-1, keepdims=True))
        p = jnp.exp(s - m)
        l = jnp.sum(p, axis=-1, keepdims=True)
        o = _dot(p.astype(BF16), vs[sq])
        if has_ctx:
            pc = jnp.exp(sc - m)
            l = l + jnp.sum(pc, axis=-1, keepdims=True)
            o = o + _dot(pc.astype(BF16), vc)
        return o / l

    groups = [(sq, i) for sq in range(n_seq) for i in range(rows // sub)]
    outs = [[] for _ in range(n_seq)]
    cur = scores(*groups[0])
    for n, (sq, i) in enumerate(groups):
        nxt = scores(*groups[n + 1]) if n + 1 < len(groups) else None
        outs[sq].append(finish(sq, i, *cur))
        cur = nxt
    for sq in range(n_seq):
        o = jnp.concatenate(outs[sq], axis=0) if len(outs[sq]) > 1 else outs[sq][0]
        for j in range(n_half):
            o_ref[sq * tq:(sq + 1) * tq, j * PAIR:(j + 1) * PAIR] = jnp.where(
                first, o[2 * j * tq:(2 * j + 1) * tq], o[(2 * j + 1) * tq:(2 * j + 2) * tq]).astype(BF16)


def _attention(q, k, v, *, lat, tq, n_seq=1, layer_i=0, cache_k=None, cache_v=None, bias_tab=None):
    has_bias = bias_tab is not None
    n_pairs = k.shape[1] // PAIR
    n_half = q.shape[1] // k.shape[1]
    nb, t = (DEC_BATCH, DEC_SEQ) if lat else (BATCH, SEQ)
    nq = t // tq
    assert n_seq == 1 or (nq == 1 and not lat)
    n_keys = _na_key_rows(tq)[0] * GRID_W if has_bias else t
    in_specs = [
        pl.BlockSpec((n_seq * tq, n_half * PAIR), lambda p, qi, b: (b * nq + qi, p)),
        pl.BlockSpec((n_seq * t, PAIR), lambda p, qi, b: (b, p)),
        pl.BlockSpec((n_seq * t, PAIR), lambda p, qi, b: (b, p)),
    ]
    args = [q, k, v]
    if lat:
        in_specs += [pl.BlockSpec((None, None, PAST_LEN, PAIR), lambda p, qi, b: (b, layer_i, 0, p))] * 2
        args += [cache_k, cache_v]
    if has_bias:
        in_specs.append(pl.BlockSpec((2,) + bias_tab.shape[1:], lambda p, qi, b: (p, 0, 0, 0)))
        args.append(bias_tab)
    return pl.pallas_call(
        functools.partial(_attn_kernel, lat, has_bias, n_seq),
        grid=(n_pairs, nq, nb // n_seq),
        in_specs=in_specs,
        out_specs=pl.BlockSpec((n_seq * tq, n_half * PAIR), lambda p, qi, b: (b * nq + qi, p)),
        out_shape=jax.ShapeDtypeStruct(q.shape, BF16),
        scratch_shapes=[pltpu.VMEM((2 * tq, n_keys), F32)] if has_bias else [],
        compiler_params=_cparams(("parallel", "parallel", "arbitrary")),
        name="attn_" + ("lat" if lat else "ctx") + ("_a" if n_half == 2 else "_b"),
    )(*args)


def _rope_tables():
    half = HEAD_DIM // 2
    freqs = ROPE_THETA ** (-np.arange(0, half, 2, dtype=np.float32) / half)
    t = np.arange(DEC_SEQ)

    def tab(pos):
        ang = pos.astype(np.float32)[:, None] * freqs[None, :]
        cos = np.concatenate([np.cos(ang), np.cos(ang)], -1)
        sin = np.concatenate([-np.sin(ang), np.sin(ang)], -1)
        return cos, sin

    cr, sr = tab(t // GRID_W)
    cc, sc = tab(t % GRID_W)
    cos = np.concatenate([cr, cc] * 2, -1).astype(np.float32)
    sin = np.concatenate([sr, sc] * 2, -1).astype(np.float32)
    return jnp.asarray(cos), jnp.asarray(sin)


def _na_bias_table(rel_bias):
    w = GRID_W
    n_rel = 2 * NA_WIN_W - 1
    left = w - NA_WIN_W
    rp = jnp.pad(rel_bias, ((0, 0), (0, 0), (left, 2 * w - n_rel - left)))
    flat = jnp.tile(rp, (1, 1, w))
    m = flat[..., :w * (2 * w - 1)].reshape(rel_bias.shape[0], rel_bias.shape[1], w, 2 * w - 1)[..., w - 1:]
    c = np.arange(w)
    ws = np.clip(c - NA_WIN_W // 2, 0, w - NA_WIN_W)
    in_col = (c[None, :] >= ws[:, None]) & (c[None, :] < ws[:, None] + NA_WIN_W)
    m = jnp.where(jnp.asarray(in_col)[None, None], m, -jnp.inf)
    return jnp.concatenate([m, m], axis=-1)


def _chunk_scan(x, reverse):
    t = x.shape[0]
    pos = lax.broadcasted_iota(jnp.int32, x.shape, 0) % GLA_CHUNK
    s = 1
    while s < GLA_CHUNK:
        if reverse:
            x = x + jnp.where(pos < GLA_CHUNK - s, pltpu.roll(x, t - s, 0), 0.0)
        else:
            x = x + jnp.where(pos >= s, pltpu.roll(x, s, 0), 0.0)
        s *= 2
    return x


def _gla_kernel(has_s0, emit_s, heads, *refs):
    it = iter(refs)
    q_ref, k_ref, v_ref, lgf_ref, lgb_ref = (next(it) for _ in range(5))
    s0_ref = next(it) if has_s0 else None
    if emit_s:
        next(it)
    o_ref = next(it)
    sf_ref = next(it) if emit_s else None
    qf_scr, qb_scr, kin_scr, kout_scr, bf_scr, bb_scr, a_scr, u_scr, st_scr, o_scr = (next(it) for _ in range(10))

    c_len, dk, dv = GLA_CHUNK, GLA_DK, GLA_DV
    t = q_ref.shape[0]
    n = t // c_len
    scale = dk ** -0.5
    row = lax.broadcasted_iota(jnp.int32, (c_len, c_len), 0)
    col = lax.broadcasted_iota(jnp.int32, (c_len, c_len), 1)
    zero = jnp.zeros((t, dk), BF16)

    def rows(c):
        return pl.ds(pl.multiple_of(c * c_len, c_len), c_len)

    def one_head(hd, kl, vl):
        lgf, lgb = lgf_ref[:, kl], lgb_ref[:, kl]
        bf = _chunk_scan(lgf, False)
        bb = _chunk_scan(lgb, True)
        q = q_ref[:, kl].astype(F32) * scale
        k = k_ref[:, kl].astype(F32)
        qf_scr[:, :dk] = (q * jnp.exp(bf)).astype(BF16)
        qf_scr[:, dk:] = zero
        qb_scr[:, :dk] = zero
        qb_scr[:, dk:] = (q * jnp.exp(bb)).astype(BF16)
        kin_scr[:, :dk] = (k * jnp.exp(-bf)).astype(BF16)
        kin_scr[:, dk:] = (k * jnp.exp(-bb)).astype(BF16)
        bf3, bb3 = bf.reshape(n, c_len, dk), bb.reshape(n, c_len, dk)
        kout_scr[:, :dk] = (k * jnp.exp((bf3[:, c_len - 1:, :] - bf3).reshape(t, dk))).astype(BF16)
        kout_scr[:, dk:] = (k * jnp.exp((bb3[:, :1, :] - bb3).reshape(t, dk))).astype(BF16)
        bf_scr[...] = bf
        bb_scr[...] = bb

        def score_pass(c, _):
            r = rows(c)
            a2 = _dot_nt(jnp.concatenate([qf_scr[r, :], qb_scr[r, :]], axis=0), kin_scr[r, :])
            a = jnp.where(col <= row, a2[:c_len], 0.0) + jnp.where(col >= row, a2[c_len:], 0.0)
            a_scr[r, :] = a.astype(BF16)
            u = _dot_tn(v_ref[r, vl], kout_scr[r, :])
            u_scr[c, :, :dk] = u[:, :dk]
            u_scr[n - 1 - c, :, dk:] = u[:, dk:]
            return _

        def value_pass(c, _):
            r = rows(c)
            o_scr[r, :] = _dot(a_scr[r, :], v_ref[r, vl])
            return _

        lax.fori_loop(0, n, score_pass, None, unroll=GLA_UNROLL)
        lax.fori_loop(0, n, value_pass, None, unroll=GLA_UNROLL)

        if has_s0:
            st_scr[:, :dk] = s0_ref[0, hd].T
            st_scr[:, dk:] = s0_ref[1, hd].T
        else:
            st_scr[...] = jnp.zeros_like(st_scr)

        def sweep(i, _):
            rf, rb = rows(i), rows(n - 1 - i)
            st = st_scr[...]
            res = _dot_nt(jnp.concatenate([qf_scr[rf, :], qb_scr[rb, :]], axis=0), st.astype(BF16))
            o_scr[rf, :] += res[:c_len]
            o_scr[rb, :] += res[c_len:]
            b_edge = jnp.concatenate([bf_scr[pl.ds(i * c_len + c_len - 1, 1), :],
                                      bb_scr[pl.ds((n - 1 - i) * c_len, 1), :]], axis=1)
            st_scr[...] = st * jnp.exp(b_edge) + u_scr[i]
            return _

        lax.fori_loop(0, n, sweep, None, unroll=GLA_UNROLL)
        o_ref[:, vl] = o_scr[...].astype(o_ref.dtype)

        if emit_s:
            sf_ref[0, hd] = st_scr[:, :dk].T
            sf_ref[1, hd] = st_scr[:, dk:].T

    if heads == 1:
        one_head(0, slice(0, dk), slice(0, dv))
    else:
        def head_step(hd, carry):
            one_head(hd, pl.ds(pl.multiple_of(hd * dk, dk), dk), pl.ds(pl.multiple_of(hd * dv, dv), dv))
            return carry

        lax.fori_loop(0, heads, head_step, None)


def _gla(q, k, v, lgf, lgb, *, lat, heads, layer_j=0, state=None, new_state=None):
    if lat:
        nb, t, row0 = DEC_BATCH, DEC_SEQ, N_CTX
    else:
        nb, t, row0 = BATCH, SEQ, 0
    off = row0 // t
    kspec = pl.BlockSpec((t, heads * GLA_DK), lambda b, h: (off + b, h))
    in_specs = [kspec, kspec, pl.BlockSpec((t, heads * GLA_DV), lambda b, h: (off + b, h)), kspec, kspec]
    args = [q, k, v, lgf, lgb]
    state_spec = pl.BlockSpec((None, None, 2, heads, GLA_DK, GLA_DV), lambda b, h: (b, layer_j, 0, h, 0, 0))
    if lat:
        in_specs.append(state_spec)
        args.append(state)
    out_specs = [pl.BlockSpec((t, heads * GLA_DV), lambda b, h: (b, h))]
    out_shape = [jax.ShapeDtypeStruct((nb * t, GLA_VDIM), BF16)]
    aliases = {}
    if not lat:
        aliases[len(args)] = 1
        in_specs.append(pl.BlockSpec(memory_space=pl.ANY))
        args.append(new_state)
        out_specs.append(state_spec)
        out_shape.append(jax.ShapeDtypeStruct(new_state.shape, new_state.dtype))
    return pl.pallas_call(
        functools.partial(_gla_kernel, lat, not lat, heads),
        grid=(nb, GLA_HEADS // heads),
        in_specs=in_specs,
        out_specs=out_specs,
        out_shape=out_shape,
        input_output_aliases=aliases,
        scratch_shapes=[
            pltpu.VMEM((t, 2 * GLA_DK), BF16),
            pltpu.VMEM((t, 2 * GLA_DK), BF16),
            pltpu.VMEM((t, 2 * GLA_DK), BF16),
            pltpu.VMEM((t, 2 * GLA_DK), BF16),
            pltpu.VMEM((t, GLA_DK), F32),
            pltpu.VMEM((t, GLA_DK), F32),
            pltpu.VMEM((t, GLA_CHUNK), BF16),
            pltpu.VMEM((t // GLA_CHUNK, GLA_DV, 2 * GLA_DK), F32),
            pltpu.VMEM((GLA_DV, 2 * GLA_DK), F32),
            pltpu.VMEM((t, GLA_DV), F32),
        ],
        compiler_params=_cparams(("parallel", "parallel")),
        name="gla_lat" if lat else "gla_ctx",
    )(*args)


def _group_pick(tm, c_ref, l_ref):
    return jnp.where(pl.program_id(0) < N_CTX // tm, c_ref[...], l_ref[...])


def _attn_out_kernel(tm, x_ref, mod_ref, oac_ref, oal_ref, obc_ref, obl_ref, w_ref, o_ref):
    oa = _group_pick(tm, oac_ref, oal_ref)
    ob = _group_pick(tm, obc_ref, obl_ref)
    na = oa.shape[1]
    y = _dot(oa, w_ref[:na, :]) + _dot(ob, w_ref[na:, :])
    o_ref[...] = x_ref[...] + mod_ref[2:3, :] * y


def _gla_out_kernel(tm, x_ref, mod_ref, oc_ref, ol_ref, gate_ref, gn_ref, w_ref, o_ref):
    gn = gn_ref[...]
    o_pre = _group_pick(tm, oc_ref, ol_ref).astype(F32)
    parts = []
    for h in range(GLA_HEADS):
        sl = slice(h * GLA_DV, (h + 1) * GLA_DV)
        gt = gate_ref[:, sl].astype(F32)
        parts.append((_rms(o_pre[:, sl], gn) * (gt * jax.nn.sigmoid(gt))).astype(BF16))
    y = _dot(jnp.concatenate(parts, axis=-1), w_ref[...])
    o_ref[...] = x_ref[...] + mod_ref[2:3, :] * y


def _mixer_out(kernel, name, x, mod, layer, group_acts, acts, consts):
    tm = TM_OUT
    nct = N_CTX // tm
    in_specs = [
        pl.BlockSpec((tm, D_MODEL), lambda i: (i, 0)),
        pl.BlockSpec((None, None, None, 3, D_MODEL), lambda i: (layer, _cond_row(i, tm), 1, 0, 0)),
    ]
    flat = []
    for a_c, a_l in group_acts:
        in_specs.append(pl.BlockSpec((tm, a_c.shape[1]), lambda i: (jnp.minimum(i, nct - 1), 0)))
        in_specs.append(pl.BlockSpec((tm, a_l.shape[1]), lambda i: (jnp.maximum(i - nct, 0), 0)))
        flat += [a_c, a_l]
    in_specs += [pl.BlockSpec((tm, a.shape[1]), lambda i: (i, 0)) for a in acts]
    in_specs += [pl.BlockSpec(c.shape, lambda i: (0, 0)) for c in consts]
    return pl.pallas_call(
        functools.partial(kernel, tm),
        grid=(N_TOK // tm,),
        in_specs=in_specs,
        out_specs=pl.BlockSpec((tm, D_MODEL), lambda i: (i, 0)),
        out_shape=jax.ShapeDtypeStruct((N_TOK, D_MODEL), F32),
        compiler_params=_cparams(("parallel",)),
        name=name,
    )(x, mod, *flat, *acts, *consts)


def _final_kernel(x_ref, g_ref, o_ref):
    o_ref[...] = _rms(x_ref[...], g_ref[...])


def _final_norm(x, g, row0, n_rows):
    tm = TM_OUT
    off = row0 // tm
    return pl.pallas_call(
        _final_kernel,
        grid=(n_rows // tm,),
        in_specs=[pl.BlockSpec((tm, D_MODEL), lambda i: (off + i, 0)),
                  pl.BlockSpec((1, D_MODEL), lambda i: (0, 0))],
        out_specs=pl.BlockSpec((tm, D_MODEL), lambda i: (i, 0)),
        out_shape=jax.ShapeDtypeStruct((n_rows, D_MODEL), F32),
        compiler_params=_cparams(("parallel",)),
        name="final_norm",
    )(x, g.reshape(1, D_MODEL))


def _permute_heads(w, axis):
    grp = A_HEADS // A_KV_HEADS
    order = [(2 * p + j) * grp + g for p in range(A_KV_HEADS // 2) for g in range(grp) for j in range(2)]
    rest = tuple(np.delete(w.shape, axis))
    w = jnp.moveaxis(w, axis, 0).reshape((A_HEADS, HEAD_DIM) + rest)
    w = w[np.asarray(order)].reshape((A_HEADS * HEAD_DIM,) + rest)
    return jnp.moveaxis(w, 0, axis)


def kernel(x_prompt, x_sample, cache_a_k, cache_a_v, cache_b_k, cache_b_v, state_gla, c, c_ctx,
           w_mod, b_mod, g_norm, w_ffn_gate, w_ffn_up, w_ffn_down, w_attn_in, w_attn_out,
           g_qnorm, g_knorm, na_rel_bias, w_gla_in, w_gla_gup, b_gla_gup, g_gla_norm, w_gla_out, g_final):
    x = (x_prompt.reshape(N_CTX, D_MODEL), x_sample.reshape(N_LAT, D_MODEL))
    cond = jnp.concatenate([c_ctx[None, :], c, jnp.zeros((COND_ROWS - 1 - DEC_BATCH, D_MODEL), F32)], axis=0)
    mod = _modulation(cond, w_mod, b_mod)

    wg_all, wu_all, wd_all = (w.astype(BF16) for w in (w_ffn_gate, w_ffn_up, w_ffn_down))
    rope_tabs = _rope_tables()
    n_attn = w_attn_in.shape[0]
    ck_a = cache_a_k.reshape(DEC_BATCH, n_attn, PAST_LEN, A_KV_HEADS * HEAD_DIM)
    cv_a = cache_a_v.reshape(DEC_BATCH, n_attn, PAST_LEN, A_KV_HEADS * HEAD_DIM)
    ck_b = cache_b_k.reshape(DEC_BATCH, n_attn, PAST_LEN, B_HEADS * HEAD_DIM)
    cv_b = cache_b_v.reshape(DEC_BATCH, n_attn, PAST_LEN, B_HEADS * HEAD_DIM)

    caches = [jnp.full((BATCH, n_attn, SEQ, ATTN_WIDTHS[s]), float(n), F32) for n, s in enumerate((1, 2, 4, 5))]
    new_gla = jnp.zeros((BATCH, w_gla_in.shape[0], 2, GLA_HEADS, GLA_DK, GLA_DV), F32)
    for l in range(DEPTH):
        x = _ffn(x, mod, l, 0, g_norm[l, 0], wg_all, wu_all, wd_all)
        if l % 2 == 0:
            i = l // 2
            gq = jnp.tile(g_qnorm[i], 2).reshape(1, PAIR)
            gk = jnp.tile(g_knorm[i], 2).reshape(1, PAIR)
            n_qa = A_HEADS * HEAD_DIM
            w_in = w_attn_in[i].at[:, :n_qa].set(_permute_heads(w_attn_in[i][:, :n_qa], axis=1)).astype(BF16)
            w_out = w_attn_out[i].at[:n_qa].set(_permute_heads(w_attn_out[i][:n_qa], axis=0)).astype(BF16)
            qa_c, ka_c, va_c, qb_c, kb_c, vb_c, *caches = _attn_in(
                x, mod, l, g_norm[l, 1], w_in, gq, gk, lat=False, layer_i=i, caches=caches)
            qa_l, ka_l, va_l, qb_l, kb_l, vb_l = _attn_in(
                x, mod, l, g_norm[l, 1], w_in, gq, gk, lat=True, rope_tabs=rope_tabs)
            oa_c = _attention(qa_c, ka_c, va_c, lat=False, tq=SEQ, n_seq=CTX_SEQS_A)
            ob_c = _attention(qb_c, kb_c, vb_c, lat=False, tq=SEQ, n_seq=CTX_SEQS_B)
            oa_l = _attention(qa_l, ka_l, va_l, lat=True, tq=TQ_LAT_A, layer_i=i, cache_k=ck_a, cache_v=cv_a)
            ob_l = _attention(qb_l, kb_l, vb_l, lat=True, tq=TQ_LAT_B, layer_i=i, cache_k=ck_b, cache_v=cv_b,
                              bias_tab=_na_bias_table(na_rel_bias[i]))
            x = _mixer_out(_attn_out_kernel, "attn_out", x, mod, l, [(oa_c, oa_l), (ob_c, ob_l)], [], [w_out])
        else:
            j = l // 2
            w_in = jnp.pad(w_gla_in[j], ((0, 0), (0, GLA_D_PAD - 2 * GLA_RANK))).astype(BF16)
            w_gup = jnp.zeros((GLA_D_PAD, 2 * GLA_KDIM), F32)
            w_gup = w_gup.at[:GLA_RANK, :GLA_KDIM].set(w_gla_gup[j, 0])
            w_gup = w_gup.at[GLA_RANK:2 * GLA_RANK, GLA_KDIM:].set(w_gla_gup[j, 1]).astype(BF16)
            b_gup = b_gla_gup[j].reshape(1, 2 * GLA_KDIM)
            q, k, v, gt, lgf, lgb = _gla_in(x, mod, l, g_norm[l, 1], w_in, w_gup, b_gup)
            o_c, new_gla = _gla(q, k, v, lgf, lgb, lat=False, heads=GLA_HEADS, layer_j=j, new_state=new_gla)
            (o_l,) = _gla(q, k, v, lgf, lgb, lat=True, heads=1, layer_j=j, state=state_gla)
            x = _mixer_out(_gla_out_kernel, "gla_out", x, mod, l, [(o_c, o_l)], [gt],
                           [g_gla_norm[j].reshape(1, GLA_DV), w_gla_out[j].astype(BF16)])
        x = _ffn(x, mod, l, 2, g_norm[l, 2], wg_all, wu_all, wd_all)

    y_ctx = _final_norm(x, g_final, 0, N_CTX)
    y_lat = _final_norm(x, g_final, N_CTX, N_LAT)
    return (y_ctx.reshape(BATCH, SEQ, D_MODEL), y_lat.reshape(DEC_BATCH, DEC_SEQ, D_MODEL),
            caches[0].reshape(BATCH, n_attn, SEQ, A_KV_HEADS, HEAD_DIM),
            caches[1].reshape(BATCH, n_attn, SEQ, A_KV_HEADS, HEAD_DIM),
            caches[2].reshape(BATCH, n_attn, SEQ, B_HEADS, HEAD_DIM),
            caches[3].reshape(BATCH, n_attn, SEQ, B_HEADS, HEAD_DIM), new_gla)
```

```python
import functools

import numpy as np
import jax
import jax.numpy as jnp
from jax import lax
from jax.experimental import pallas as pl
from jax.experimental.pallas import tpu as pltpu

D_MODEL = 1024
BATCH = 32
SEQ = 256
DEPTH = 4
DEC_BATCH = 8
DEC_SEQ = 1024
PAST_LEN = 512
GRID_W = 64
HEAD_DIM = 64
A_HEADS = 8
A_KV_HEADS = 4
B_HEADS = 8
ROPE_THETA = 10000.0
NA_WIN_H = 8
NA_WIN_W = 16
GLA_HEADS = 4
GLA_DK = D_MODEL // 2 // GLA_HEADS
GLA_DV = D_MODEL // GLA_HEADS
GLA_KDIM = GLA_HEADS * GLA_DK
GLA_VDIM = GLA_HEADS * GLA_DV
GLA_RANK = 16
GLA_GATE_NORM = 16.0
GLA_CHUNK = 64
D_FF = ((8 * D_MODEL // 3 + 127) // 128) * 128
FFN_RES = 0.5
N_MOD = 9
EPS = 1e-6

N_CTX = BATCH * SEQ
N_LAT = DEC_BATCH * DEC_SEQ
N_TOK = N_CTX + N_LAT
COND_ROWS = 16
LANES = 128
PAIR = 2 * HEAD_DIM
GLA_D_PAD = LANES
VMEM_LIMIT = 56 * 1024 * 1024

F32 = jnp.float32
BF16 = jnp.bfloat16

TM_FFN = 1024
TF_FFN = 256
FFN_CHUNKS = D_FF // TF_FFN
FFN_ROW_BLOCK = 256
assert FFN_CHUNKS * TF_FFN == D_FF and FFN_CHUNKS % 2 == 1
TM_PROJ = 1024
MOD_TN = 2304
TM_OUT = 1024
TM_GLA_IN = 1024
TQ_LAT_A = 256
TQ_LAT_B = 512
CTX_SEQS_A = 2
CTX_SEQS_B = 4
ATTN_SUB_ROWS = 512
GLA_UNROLL = 8


def _cparams(sem):
    return pltpu.CompilerParams(dimension_semantics=sem, vmem_limit_bytes=VMEM_LIMIT)


def _cond_row(i, tm):
    nct = N_CTX // tm
    per_batch = DEC_SEQ // tm
    return jnp.where(i < nct, 0, 1 + (i - nct) // per_batch)


def _rms(x, g):
    ms = jnp.mean(x * x, axis=-1, keepdims=True)
    return x * lax.rsqrt(ms + EPS) * g


def _dot(a, b):
    return jnp.dot(a, b, preferred_element_type=F32)


def _dot_nt(a, b):
    return lax.dot_general(a, b, (((1,), (1,)), ((), ())), preferred_element_type=F32)


def _dot_tn(a, b):
    return lax.dot_general(a, b, (((0,), (0,)), ((), ())), preferred_element_type=F32)


def _split_bf16(x):
    hi = x.astype(BF16)
    return hi, (x - hi.astype(F32)).astype(BF16)


def _mod_kernel(c_ref, w_ref, b_ref, o_ref):
    c = c_ref[...]
    s_hi, s_lo = _split_bf16(c * jax.nn.sigmoid(c))
    w_hi, w_lo = _split_bf16(w_ref[...])
    o_ref[...] = _dot(s_hi, w_hi) + (_dot(s_lo, w_hi) + _dot(s_hi, w_lo)) + b_ref[...]


def _modulation(cond, w_mod, b_mod):
    tn = MOD_TN
    n_mod = N_MOD * D_MODEL
    out = pl.pallas_call(
        _mod_kernel,
        grid=(DEPTH, n_mod // tn),
        in_specs=[
            pl.BlockSpec((COND_ROWS, D_MODEL), lambda l, j: (0, 0)),
            pl.BlockSpec((None, D_MODEL, tn), lambda l, j: (l, 0, j)),
            pl.BlockSpec((None, 1, tn), lambda l, j: (l, 0, j)),
        ],
        out_specs=pl.BlockSpec((None, COND_ROWS, tn), lambda l, j: (l, 0, j)),
        out_shape=jax.ShapeDtypeStruct((DEPTH, COND_ROWS, n_mod), F32),
        compiler_params=_cparams(("parallel", "parallel")),
        name="adaln_mod",
    )(cond, w_mod, b_mod.reshape(DEPTH, 1, n_mod))
    return out.reshape(DEPTH, COND_ROWS, 3, 3, D_MODEL)


def _ffn_kernel(split_x, *refs):
    if split_x:
        xc_ref, xl_ref, *refs = refs
    else:
        x_ref, *refs = refs
    mod_ref, g_ref, wg_ref, wu_ref, wd_ref, o_ref, h_ref, acc_ref, g_scr, u_scr = refs
    tm = o_ref.shape[0]

    def x_rows(rs):
        if split_x:
            return jnp.where(pl.program_id(0) < N_CTX // tm, xc_ref[rs, :], xl_ref[rs, :])
        return x_ref[rs, :]

    def cols(j):
        return pl.ds(pl.multiple_of(j * TF_FFN, TF_FFN), TF_FFN)

    def gate_up(h, j):
        return _dot(h, wg_ref[:, cols(j)]), _dot(h, wu_ref[:, cols(j)])

    def down(g, u, j):
        a = (g * jax.nn.sigmoid(g) * u).astype(BF16)
        return _dot(a, wd_ref[cols(j), :])

    row_blocks = [slice(r, r + FFN_ROW_BLOCK) for r in range(0, tm, FFN_ROW_BLOCK)]
    gain = g_ref[...] * (1.0 + mod_ref[1:2, :])
    for rs in row_blocks:
        x = x_rows(rs)
        ms = jnp.mean(x * x, axis=-1, keepdims=True)
        h = (x * lax.rsqrt(ms + EPS) * gain + mod_ref[0:1, :]).astype(BF16)
        h_ref[rs, :] = h
        g_scr[rs, :], u_scr[rs, :] = gate_up(h, 0)
    acc_ref[...] = jnp.zeros_like(acc_ref)

    def chunk_pair(jj, carry):
        j = 2 * jj
        h = h_ref[...]
        g1, u1 = gate_up(h, j + 1)
        acc_ref[...] += down(g_scr[...], u_scr[...], j)
        g2, u2 = gate_up(h, j + 2)
        acc_ref[...] += down(g1, u1, j + 1)
        g_scr[...], u_scr[...] = g2, u2
        return carry

    lax.fori_loop(0, FFN_CHUNKS // 2, chunk_pair, None, unroll=True)
    res_gate = FFN_RES * mod_ref[2:3, :]
    for rs in row_blocks:
        y = acc_ref[rs, :] + down(g_scr[rs, :], u_scr[rs, :], FFN_CHUNKS - 1)
        o_ref[rs, :] = x_rows(rs) + res_gate * y


def _ffn(x, mod, layer, sub, g, wg, wu, wd):
    tm = TM_FFN
    nct = N_CTX // tm
    which = sub // 2
    resident = lambda w: pl.BlockSpec((None, None) + w.shape[2:], lambda i: (layer, which, 0, 0),
                                      pipeline_mode=pl.Buffered(1))
    split_x = isinstance(x, tuple)
    if split_x:
        x_specs = [pl.BlockSpec((tm, D_MODEL), lambda i: (jnp.minimum(i, nct - 1), 0)),
                   pl.BlockSpec((tm, D_MODEL), lambda i: (jnp.maximum(i - nct, 0), 0))]
        xs = list(x)
    else:
        x_specs = [pl.BlockSpec((tm, D_MODEL), lambda i: (i, 0))]
        xs = [x]
    return pl.pallas_call(
        functools.partial(_ffn_kernel, split_x),
        grid=(N_TOK // tm,),
        in_specs=x_specs + [
            pl.BlockSpec((None, None, None, 3, D_MODEL), lambda i: (layer, _cond_row(i, tm), sub, 0, 0)),
            pl.BlockSpec((1, D_MODEL), lambda i: (0, 0)),
            resident(wg), resident(wu), resident(wd),
        ],
        out_specs=pl.BlockSpec((tm, D_MODEL), lambda i: (i, 0)),
        out_shape=jax.ShapeDtypeStruct((N_TOK, D_MODEL), F32),
        scratch_shapes=[pltpu.VMEM((tm, D_MODEL), BF16), pltpu.VMEM((tm, D_MODEL), F32),
                        pltpu.VMEM((tm, TF_FFN), F32), pltpu.VMEM((tm, TF_FFN), F32)],
        compiler_params=_cparams(("parallel",)),
        name="ffn",
    )(*xs, mod, g.reshape(1, D_MODEL), wg, wu, wd)


def _mixer_h(x_ref, mod_ref, g_ref):
    y = _rms(x_ref[...], g_ref[...])
    return (y * (1.0 + mod_ref[1:2, :]) + mod_ref[0:1, :]).astype(BF16)


ATTN_WIDTHS = (A_HEADS * HEAD_DIM, A_KV_HEADS * HEAD_DIM, A_KV_HEADS * HEAD_DIM,
               B_HEADS * HEAD_DIM, B_HEADS * HEAD_DIM, B_HEADS * HEAD_DIM)


def _attn_in_kernel(lat, x_ref, mod_ref, g_ref, w_ref, gq_ref, gk_ref, *refs):
    if lat:
        cos_ref, sin_ref, *o_refs = refs
    else:
        o_refs = refs[4:]
    h = _mixer_h(x_ref, mod_ref, g_ref)
    nb = x_ref.shape[0] // SEQ
    first = _head_masks()
    scale = HEAD_DIM ** -0.5
    offs = np.cumsum((0,) + ATTN_WIDTHS)

    def seg(i):
        return _dot(h, w_ref[:, offs[i]:offs[i + 1]])

    def normed_pairs(y, g2):
        for p in range(y.shape[1] // PAIR):
            z = _pair_rms(y[:, p * PAIR:(p + 1) * PAIR], g2, first)
            if lat:
                z = _pair_rope(z, cos_ref[...], sin_ref[...])
            yield slice(p * PAIR, (p + 1) * PAIR), z

    for sl, z in normed_pairs(seg(0), gq_ref[...]):
        o_refs[0][:, sl] = (z * scale).astype(BF16)
    for sl, z in normed_pairs(seg(1), gk_ref[...]):
        o_refs[1][:, sl] = z.astype(BF16)
        if not lat:
            o_refs[6][:, :, sl] = z.reshape(nb, SEQ, PAIR)
    for i in (2, 3, 4, 5):
        y = seg(i)
        o_refs[i][...] = ((y * scale) if i == 3 else y).astype(BF16)
        if not lat and i != 3:
            o_refs[{2: 7, 4: 8, 5: 9}[i]][...] = y.reshape(nb, SEQ, y.shape[1])


def _gla_in_kernel(widths, x_ref, mod_ref, g_ref, w_ref, wgup_ref, bgup_ref, *o_refs):
    h = _mixer_h(x_ref, mod_ref, g_ref)
    offs = np.cumsum((0,) + tuple(widths))

    def project(i, lo=0, hi=None):
        hi = widths[i] if hi is None else hi
        o_refs[i][:, lo:hi] = _dot(h, w_ref[:, offs[i] + lo:offs[i] + hi]).astype(o_refs[i].dtype)

    d = _dot(h, w_ref[:, offs[4]:offs[4] + GLA_D_PAD]).astype(BF16)
    project(0)
    pre = _dot(d, wgup_ref[...]) + bgup_ref[...]
    project(1)
    lg = (jnp.minimum(pre, 0.0) - jnp.log1p(jnp.exp(-jnp.abs(pre)))) * (1.0 / GLA_GATE_NORM)
    o_refs[4][...] = lg[:, :GLA_KDIM]
    o_refs[5][...] = lg[:, GLA_KDIM:]
    for i in (2, 3):
        half = widths[i] // 2
        project(i, 0, half)
        project(i, half, widths[i])


def _mixer_in_specs(layer, tm, n_in, tile0=0):
    return [
        pl.BlockSpec((tm, D_MODEL), lambda i: (tile0 + i, 0)),
        pl.BlockSpec((None, None, None, 3, D_MODEL), lambda i: (layer, _cond_row(tile0 + i, tm), 1, 0, 0)),
        pl.BlockSpec((1, D_MODEL), lambda i: (0, 0)),
        pl.BlockSpec((D_MODEL, n_in), lambda i: (0, 0)),
    ]


def _attn_in(x, mod, layer, g, w_in, gq, gk, *, lat, rope_tabs=None, layer_i=0, caches=None):
    tm = TM_PROJ
    n_rows, tile0 = (N_LAT, N_CTX // tm) if lat else (N_CTX, 0)
    in_specs = _mixer_in_specs(layer, tm, sum(ATTN_WIDTHS), tile0)
    in_specs += [pl.BlockSpec((1, PAIR), lambda i: (0, 0))] * 2
    args = [x, mod, g.reshape(1, D_MODEL), w_in, gq, gk]
    out_specs = [pl.BlockSpec((tm, wd), lambda i: (i, 0)) for wd in ATTN_WIDTHS]
    out_shape = [jax.ShapeDtypeStruct((n_rows, wd), BF16) for wd in ATTN_WIDTHS]
    aliases = {}
    if lat:
        per_seq = DEC_SEQ // tm
        in_specs += [pl.BlockSpec((tm, PAIR), lambda i: (i % per_seq, 0))] * 2
        args += list(rope_tabs)
    else:
        for c in caches:
            aliases[len(args)] = len(out_specs)
            in_specs.append(pl.BlockSpec(memory_space=pl.ANY))
            args.append(c)
            out_specs.append(pl.BlockSpec((tm // SEQ, None, SEQ, c.shape[-1]), lambda i: (i, layer_i, 0, 0)))
            out_shape.append(jax.ShapeDtypeStruct(c.shape, c.dtype))
    return pl.pallas_call(
        functools.partial(_attn_in_kernel, lat),
        grid=(n_rows // tm,),
        in_specs=in_specs,
        out_specs=out_specs,
        out_shape=out_shape,
        input_output_aliases=aliases,
        compiler_params=_cparams(("parallel",)),
        name="attn_in_lat" if lat else "attn_in_ctx",
    )(*args)


def _gla_in(x, mod, layer, g, w_in, w_gup, b_gup):
    tm = TM_GLA_IN
    widths = (GLA_KDIM, GLA_KDIM, GLA_VDIM, GLA_VDIM)
    out_w = widths + (GLA_KDIM, GLA_KDIM)
    out_dt = (BF16, BF16, BF16, BF16, F32, F32)
    n_in = sum(widths) + GLA_D_PAD
    return pl.pallas_call(
        functools.partial(_gla_in_kernel, widths),
        grid=(N_TOK // tm,),
        in_specs=_mixer_in_specs(layer, tm, n_in) + [
            pl.BlockSpec((GLA_D_PAD, 2 * GLA_KDIM), lambda i: (0, 0)),
            pl.BlockSpec((1, 2 * GLA_KDIM), lambda i: (0, 0)),
        ],
        out_specs=[pl.BlockSpec((tm, wd), lambda i: (i, 0)) for wd in out_w],
        out_shape=[jax.ShapeDtypeStruct((N_TOK, wd), dt) for wd, dt in zip(out_w, out_dt)],
        compiler_params=_cparams(("parallel",)),
        name="gla_in",
    )(x, mod, g.reshape(1, D_MODEL), w_in, w_gup, b_gup)


def _head_masks():
    lane = lax.broadcasted_iota(jnp.int32, (1, PAIR), 1)
    return lane < HEAD_DIM


def _pair_rms(x, g2, first):
    sq = x * x
    s0 = jnp.sum(jnp.where(first, sq, 0.0), axis=-1, keepdims=True)
    s1 = jnp.sum(jnp.where(first, 0.0, sq), axis=-1, keepdims=True)
    ms = jnp.where(first, s0, s1) * (1.0 / HEAD_DIM)
    return x * lax.rsqrt(ms + EPS) * g2


def _pair_rope(x, cos, sin_signed):
    lane = lax.broadcasted_iota(jnp.int32, x.shape, 1)
    quarter = HEAD_DIM // 4
    partner = jnp.where(lane % (2 * quarter) < quarter,
                        pltpu.roll(x, PAIR - quarter, 1), pltpu.roll(x, quarter, 1))
    return x * cos + partner * sin_signed


def _na_key_rows(tq):
    rows = DEC_SEQ // GRID_W
    wh = min(NA_WIN_H, rows)
    r = tq // GRID_W
    start = lambda qr: max(0, min(qr - wh // 2, rows - wh))
    lo = [start(qi * r) for qi in range(rows // r)]
    hi = [start(qi * r + r - 1) + wh for qi in range(rows // r)]
    span = max(h - l for l, h in zip(lo, hi))
    span += span % 2
    return span, [min(l, rows - span) for l in lo]


def _fill_na_bias(tab_ref, bias_scr, qi, tq, first, low, span):
    rows = DEC_SEQ // GRID_W
    wh = min(NA_WIN_H, rows)
    neg = -jnp.inf
    for a in range(tq // GRID_W):
        qr = qi * (tq // GRID_W) + a
        rs = jnp.clip(qr - wh // 2, 0, rows - wh)
        for m in range(span // 2):
            kr0, kr1 = low + 2 * m, low + 2 * m + 1
            ok0 = jnp.logical_and(kr0 >= rs, kr0 < rs + wh)
            ok1 = jnp.logical_and(kr1 >= rs, kr1 < rs + wh)
            d0 = jnp.clip(kr0 - qr + NA_WIN_H - 1, 0, 2 * NA_WIN_H - 2)
            d1 = jnp.clip(kr1 - qr + NA_WIN_H - 1, 0, 2 * NA_WIN_H - 2)
            for hh in range(2):
                t0 = jnp.where(ok0, tab_ref[hh, d0], neg)
                t1 = jnp.where(ok1, tab_ref[hh, d1], neg)
                bias_scr[hh * tq + a * GRID_W:hh * tq + (a + 1) * GRID_W, m * PAIR:(m + 1) * PAIR] = (
                    jnp.where(first, t0, t1))


def _attn_kernel(has_ctx, has_bias, n_seq, *refs):
    it = iter(refs)
    q_ref, k_ref, v_ref = next(it), next(it), next(it)
    kc_ref, vc_ref = (next(it), next(it)) if has_ctx else (None, None)
    tab_ref = next(it) if has_bias else None
    o_ref = next(it)
    bias_scr = next(it) if has_bias else None

    first = _head_masks()
    tq = q_ref.shape[0] // n_seq
    tk = k_ref.shape[0] // n_seq
    n_half = q_ref.shape[1] // PAIR
    rows = 2 * n_half * tq
    sub = min(rows, ATTN_SUB_ROWS)

    if has_bias:
        span, lows = _na_key_rows(tq)
        qi = pl.program_id(1)
        low = sum(jnp.where(qi == n, l, 0) for n, l in enumerate(lows))
        key_rows = pl.ds(pl.multiple_of(low * GRID_W, GRID_W), span * GRID_W)

        @pl.when(pl.program_id(2) == 0)
        def _():
            _fill_na_bias(tab_ref, bias_scr, qi, tq, first, low, span)

    zero = jnp.zeros((tq, PAIR), BF16)
    qs, ks, vs = [], [], []
    for sq in range(n_seq):
        stacked = []
        for j in range(n_half):
            qj = q_ref[sq * tq:(sq + 1) * tq, j * PAIR:(j + 1) * PAIR]
            stacked += [jnp.where(first, qj, zero), jnp.where(first, zero, qj)]
        qs.append(jnp.concatenate(stacked, axis=0))
        if has_bias:
            ks.append(k_ref[key_rows, :])
            vs.append(v_ref[key_rows, :])
        else:
            ks.append(k_ref[sq * tk:(sq + 1) * tk, :])
            vs.append(v_ref[sq * tk:(sq + 1) * tk, :])
    kc = kc_ref[...].astype(BF16) if has_ctx else None
    vc = vc_ref[...].astype(BF16) if has_ctx else None

    def scores(sq, i):
        qi = qs[sq][i * sub:(i + 1) * sub]
        return _dot_nt(qi, ks[sq]), (_dot_nt(qi, kc) if has_ctx else None)

    def finish(sq, i, s, sc):
        if has_bias:
            s = s + bias_scr[i * sub:(i + 1) * sub, :]
        m = jnp.max(s, axis=-1, keepdims=True)
        if has_ctx:
            m = jnp.maximum(m, jnp.max(sc, axis=-1, keepdims=True))
        p = jnp.exp(s - m)
        l = jnp.sum(p, axis=-1, keepdims=True)
        o = _dot(p.astype(BF16), vs[sq])
        if has_ctx:
            pc = jnp.exp(sc - m)
            l = l + jnp.sum(pc, axis=-1, keepdims=True)
            o = o + _dot(pc.astype(BF16), vc)
        return o / l

    groups = [(sq, i) for sq in range(n_seq) for i in range(rows // sub)]
    outs = [[] for _ in range(n_seq)]
    cur = scores(*groups[0])
    for n, (sq, i) in enumerate(groups):
        nxt = scores(*groups[n + 1]) if n + 1 < len(groups) else None
        outs[sq].append(finish(sq, i, *cur))
        cur = nxt
    for sq in range(n_seq):
        o = jnp.concatenate(outs[sq], axis=0) if len(outs[sq]) > 1 else outs[sq][0]
        for j in range(n_half):
            o_ref[sq * tq:(sq + 1) * tq, j * PAIR:(j + 1) * PAIR] = jnp.where(
                first, o[2 * j * tq:(2 * j + 1) * tq], o[(2 * j + 1) * tq:(2 * j + 2) * tq]).astype(BF16)


def _attention(q, k, v, *, lat, tq, n_seq=1, layer_i=0, cache_k=None, cache_v=None, bias_tab=None):
    has_bias = bias_tab is not None
    n_pairs = k.shape[1] // PAIR
    n_half = q.shape[1] // k.shape[1]
    nb, t = (DEC_BATCH, DEC_SEQ) if lat else (BATCH, SEQ)
    nq = t // tq
    assert n_seq == 1 or (nq == 1 and not lat)
    n_keys = _na_key_rows(tq)[0] * GRID_W if has_bias else t
    in_specs = [
        pl.BlockSpec((n_seq * tq, n_half * PAIR), lambda p, qi, b: (b * nq + qi, p)),
        pl.BlockSpec((n_seq * t, PAIR), lambda p, qi, b: (b, p)),
        pl.BlockSpec((n_seq * t, PAIR), lambda p, qi, b: (b, p)),
    ]
    args = [q, k, v]
    if lat:
        in_specs += [pl.BlockSpec((None, None, PAST_LEN, PAIR), lambda p, qi, b: (b, layer_i, 0, p))] * 2
        args += [cache_k, cache_v]
    if has_bias:
        in_specs.append(pl.BlockSpec((2,) + bias_tab.shape[1:], lambda p, qi, b: (p, 0, 0, 0)))
        args.append(bias_tab)
    return pl.pallas_call(
        functools.partial(_attn_kernel, lat, has_bias, n_seq),
        grid=(n_pairs, nq, nb // n_seq),
        in_specs=in_specs,
        out_specs=pl.BlockSpec((n_seq * tq, n_half * PAIR), lambda p, qi, b: (b * nq + qi, p)),
        out_shape=jax.ShapeDtypeStruct(q.shape, BF16),
        scratch_shapes=[pltpu.VMEM((2 * tq, n_keys), F32)] if has_bias else [],
        compiler_params=_cparams(("parallel", "parallel", "arbitrary")),
        name="attn_" + ("lat" if lat else "ctx") + ("_a" if n_half == 2 else "_b"),
    )(*args)


def _rope_tables():
    half = HEAD_DIM // 2
    freqs = ROPE_THETA ** (-np.arange(0, half, 2, dtype=np.float32) / half)
    t = np.arange(DEC_SEQ)

    def tab(pos):
        ang = pos.astype(np.float32)[:, None] * freqs[None, :]
        cos = np.concatenate([np.cos(ang), np.cos(ang)], -1)
        sin = np.concatenate([-np.sin(ang), np.sin(ang)], -1)
        return cos, sin

    cr, sr = tab(t // GRID_W)
    cc, sc = tab(t % GRID_W)
    cos = np.concatenate([cr, cc] * 2, -1).astype(np.float32)
    sin = np.concatenate([sr, sc] * 2, -1).astype(np.float32)
    return jnp.asarray(cos), jnp.asarray(sin)


def _na_bias_table(rel_bias):
    w = GRID_W
    n_rel = 2 * NA_WIN_W - 1
    left = w - NA_WIN_W
    rp = jnp.pad(rel_bias, ((0, 0), (0, 0), (left, 2 * w - n_rel - left)))
    flat = jnp.tile(rp, (1, 1, w))
    m = flat[..., :w * (2 * w - 1)].reshape(rel_bias.shape[0], rel_bias.shape[1], w, 2 * w - 1)[..., w - 1:]
    c = np.arange(w)
    ws = np.clip(c - NA_WIN_W // 2, 0, w - NA_WIN_W)
    in_col = (c[None, :] >= ws[:, None]) & (c[None, :] < ws[:, None] + NA_WIN_W)
    m = jnp.where(jnp.asarray(in_col)[None, None], m, -jnp.inf)
    return jnp.concatenate([m, m], axis=-1)


def _chunk_scan(x, reverse):
    t = x.shape[0]
    pos = lax.broadcasted_iota(jnp.int32, x.shape, 0) % GLA_CHUNK
    s = 1
    while s < GLA_CHUNK:
        if reverse:
            x = x + jnp.where(pos < GLA_CHUNK - s, pltpu.roll(x, t - s, 0), 0.0)
        else:
            x = x + jnp.where(pos >= s, pltpu.roll(x, s, 0), 0.0)
        s *= 2
    return x


def _gla_kernel(has_s0, emit_s, heads, *refs):
    it = iter(refs)
    q_ref, k_ref, v_ref, lgf_ref, lgb_ref = (next(it) for _ in range(5))
    s0_ref = next(it) if has_s0 else None
    if emit_s:
        next(it)
    o_ref = next(it)
    sf_ref = next(it) if emit_s else None
    qf_scr, qb_scr, kin_scr, kout_scr, bf_scr, bb_scr, a_scr, u_scr, st_scr, o_scr = (next(it) for _ in range(10))

    c_len, dk, dv = GLA_CHUNK, GLA_DK, GLA_DV
    t = q_ref.shape[0]
    n = t // c_len
    scale = dk ** -0.5
    row = lax.broadcasted_iota(jnp.int32, (c_len, c_len), 0)
    col = lax.broadcasted_iota(jnp.int32, (c_len, c_len), 1)
    zero = jnp.zeros((t, dk), BF16)

    def rows(c):
        return pl.ds(pl.multiple_of(c * c_len, c_len), c_len)

    def one_head(hd, kl, vl):
        lgf, lgb = lgf_ref[:, kl], lgb_ref[:, kl]
        bf = _chunk_scan(lgf, False)
        bb = _chunk_scan(lgb, True)
        q = q_ref[:, kl].astype(F32) * scale
        k = k_ref[:, kl].astype(F32)
        qf_scr[:, :dk] = (q * jnp.exp(bf)).astype(BF16)
        qf_scr[:, dk:] = zero
        qb_scr[:, :dk] = zero
        qb_scr[:, dk:] = (q * jnp.exp(bb)).astype(BF16)
        kin_scr[:, :dk] = (k * jnp.exp(-bf)).astype(BF16)
        kin_scr[:, dk:] = (k * jnp.exp(-bb)).astype(BF16)
        bf3, bb3 = bf.reshape(n, c_len, dk), bb.reshape(n, c_len, dk)
        kout_scr[:, :dk] = (k * jnp.exp((bf3[:, c_len - 1:, :] - bf3).reshape(t, dk))).astype(BF16)
        kout_scr[:, dk:] = (k * jnp.exp((bb3[:, :1, :] - bb3).reshape(t, dk))).astype(BF16)
        bf_scr[...] = bf
        bb_scr[...] = bb

        def score_pass(c, _):
            r = rows(c)
            a2 = _dot_nt(jnp.concatenate([qf_scr[r, :], qb_scr[r, :]], axis=0), kin_scr[r, :])
            a = jnp.where(col <= row, a2[:c_len], 0.0) + jnp.where(col >= row, a2[c_len:], 0.0)
            a_scr[r, :] = a.astype(BF16)
            u = _dot_tn(v_ref[r, vl], kout_scr[r, :])
            u_scr[c, :, :dk] = u[:, :dk]
            u_scr[n - 1 - c, :, dk:] = u[:, dk:]
            return _

        def value_pass(c, _):
            r = rows(c)
            o_scr[r, :] = _dot(a_scr[r, :], v_ref[r, vl])
            return _

        lax.fori_loop(0, n, score_pass, None, unroll=GLA_UNROLL)
        lax.fori_loop(0, n, value_pass, None, unroll=GLA_UNROLL)

        if has_s0:
            st_scr[:, :dk] = s0_ref[0, hd].T
            st_scr[:, dk:] = s0_ref[1, hd].T
        else:
            st_scr[...] = jnp.zeros_like(st_scr)

        def sweep(i, _):
            rf, rb = rows(i), rows(n - 1 - i)
            st = st_scr[...]
            res = _dot_nt(jnp.concatenate([qf_scr[rf, :], qb_scr[rb, :]], axis=0), st.astype(BF16))
            o_scr[rf, :] += res[:c_len]
            o_scr[rb, :] += res[c_len:]
            b_edge = jnp.concatenate([bf_scr[pl.ds(i * c_len + c_len - 1, 1), :],
                                      bb_scr[pl.ds((n - 1 - i) * c_len, 1), :]], axis=1)
            st_scr[...] = st * jnp.exp(b_edge) + u_scr[i]
            return _

        lax.fori_loop(0, n, sweep, None, unroll=GLA_UNROLL)
        o_ref[:, vl] = o_scr[...].astype(o_ref.dtype)

        if emit_s:
            sf_ref[0, hd] = st_scr[:, :dk].T
            sf_ref[1, hd] = st_scr[:, dk:].T

    if heads == 1:
        one_head(0, slice(0, dk), slice(0, dv))
    else:
        def head_step(hd, carry):
            one_head(hd, pl.ds(pl.multiple_of(hd * dk, dk), dk), pl.ds(pl.multiple_of(hd * dv, dv), dv))
            return carry

        lax.fori_loop(0, heads, head_step, None)


def _gla(q, k, v, lgf, lgb, *, lat, heads, layer_j=0, state=None, new_state=None):
    if lat:
        nb, t, row0 = DEC_BATCH, DEC_SEQ, N_CTX
    else:
        nb, t, row0 = BATCH, SEQ, 0
    off = row0 // t
    kspec = pl.BlockSpec((t, heads * GLA_DK), lambda b, h: (off + b, h))
    in_specs = [kspec, kspec, pl.BlockSpec((t, heads * GLA_DV), lambda b, h: (off + b, h)), kspec, kspec]
    args = [q, k, v, lgf, lgb]
    state_spec = pl.BlockSpec((None, None, 2, heads, GLA_DK, GLA_DV), lambda b, h: (b, layer_j, 0, h, 0, 0))
    if lat:
        in_specs.append(state_spec)
        args.append(state)
    out_specs = [pl.BlockSpec((t, heads * GLA_DV), lambda b, h: (b, h))]
    out_shape = [jax.ShapeDtypeStruct((nb * t, GLA_VDIM), BF16)]
    aliases = {}
    if not lat:
        aliases[len(args)] = 1
        in_specs.append(pl.BlockSpec(memory_space=pl.ANY))
        args.append(new_state)
        out_specs.append(state_spec)
        out_shape.append(jax.ShapeDtypeStruct(new_state.shape, new_state.dtype))
    return pl.pallas_call(
        functools.partial(_gla_kernel, lat, not lat, heads),
        grid=(nb, GLA_HEADS // heads),
        in_specs=in_specs,
        out_specs=out_specs,
        out_shape=out_shape,
        input_output_aliases=aliases,
        scratch_shapes=[
            pltpu.VMEM((t, 2 * GLA_DK), BF16),
            pltpu.VMEM((t, 2 * GLA_DK), BF16),
            pltpu.VMEM((t, 2 * GLA_DK), BF16),
            pltpu.VMEM((t, 2 * GLA_DK), BF16),
            pltpu.VMEM((t, GLA_DK), F32),
            pltpu.VMEM((t, GLA_DK), F32),
            pltpu.VMEM((t, GLA_CHUNK), BF16),
            pltpu.VMEM((t // GLA_CHUNK, GLA_DV, 2 * GLA_DK), F32),
            pltpu.VMEM((GLA_DV, 2 * GLA_DK), F32),
            pltpu.VMEM((t, GLA_DV), F32),
        ],
        compiler_params=_cparams(("parallel", "parallel")),
        name="gla_lat" if lat else "gla_ctx",
    )(*args)


def _group_pick(tm, c_ref, l_ref):
    return jnp.where(pl.program_id(0) < N_CTX // tm, c_ref[...], l_ref[...])


def _attn_out_kernel(tm, x_ref, mod_ref, oac_ref, oal_ref, obc_ref, obl_ref, w_ref, o_ref):
    oa = _group_pick(tm, oac_ref, oal_ref)
    ob = _group_pick(tm, obc_ref, obl_ref)
    na = oa.shape[1]
    y = _dot(oa, w_ref[:na, :]) + _dot(ob, w_ref[na:, :])
    o_ref[...] = x_ref[...] + mod_ref[2:3, :] * y


def _gla_out_kernel(tm, x_ref, mod_ref, oc_ref, ol_ref, gate_ref, gn_ref, w_ref, o_ref):
    gn = gn_ref[...]
    o_pre = _group_pick(tm, oc_ref, ol_ref).astype(F32)
    parts = []
    for h in range(GLA_HEADS):
        sl = slice(h * GLA_DV, (h + 1) * GLA_DV)
        gt = gate_ref[:, sl].astype(F32)
        parts.append((_rms(o_pre[:, sl], gn) * (gt * jax.nn.sigmoid(gt))).astype(BF16))
    y = _dot(jnp.concatenate(parts, axis=-1), w_ref[...])
    o_ref[...] = x_ref[...] + mod_ref[2:3, :] * y


def _mixer_out(kernel, name, x, mod, layer, group_acts, acts, consts):
    tm = TM_OUT
    nct = N_CTX // tm
    in_specs = [
        pl.BlockSpec((tm, D_MODEL), lambda i: (i, 0)),
        pl.BlockSpec((None, None, None, 3, D_MODEL), lambda i: (layer, _cond_row(i, tm), 1, 0, 0)),
    ]
    flat = []
    for a_c, a_l in group_acts:
        in_specs.append(pl.BlockSpec((tm, a_c.shape[1]), lambda i: (jnp.minimum(i, nct - 1), 0)))
        in_specs.append(pl.BlockSpec((tm, a_l.shape[1]), lambda i: (jnp.maximum(i - nct, 0), 0)))
        flat += [a_c, a_l]
    in_specs += [pl.BlockSpec((tm, a.shape[1]), lambda i: (i, 0)) for a in acts]
    in_specs += [pl.BlockSpec(c.shape, lambda i: (0, 0)) for c in consts]
    return pl.pallas_call(
        functools.partial(kernel, tm),
        grid=(N_TOK // tm,),
        in_specs=in_specs,
        out_specs=pl.BlockSpec((tm, D_MODEL), lambda i: (i, 0)),
        out_shape=jax.ShapeDtypeStruct((N_TOK, D_MODEL), F32),
        compiler_params=_cparams(("parallel",)),
        name=name,
    )(x, mod, *flat, *acts, *consts)


def _final_kernel(x_ref, g_ref, o_ref):
    o_ref[...] = _rms(x_ref[...], g_ref[...])


def _final_norm(x, g, row0, n_rows):
    tm = TM_OUT
    off = row0 // tm
    return pl.pallas_call(
        _final_kernel,
        grid=(n_rows // tm,),
        in_specs=[pl.BlockSpec((tm, D_MODEL), lambda i: (off + i, 0)),
                  pl.BlockSpec((1, D_MODEL), lambda i: (0, 0))],
        out_specs=pl.BlockSpec((tm, D_MODEL), lambda i: (i, 0)),
        out_shape=jax.ShapeDtypeStruct((n_rows, D_MODEL), F32),
        compiler_params=_cparams(("parallel",)),
        name="final_norm",
    )(x, g.reshape(1, D_MODEL))


def _permute_heads(w, axis):
    grp = A_HEADS // A_KV_HEADS
    order = [(2 * p + j) * grp + g for p in range(A_KV_HEADS // 2) for g in range(grp) for j in range(2)]
    rest = tuple(np.delete(w.shape, axis))
    w = jnp.moveaxis(w, axis, 0).reshape((A_HEADS, HEAD_DIM) + rest)
    w = w[np.asarray(order)].reshape((A_HEADS * HEAD_DIM,) + rest)
    return jnp.moveaxis(w, 0, axis)


def kernel(x_prompt, x_sample, cache_a_k, cache_a_v, cache_b_k, cache_b_v, state_gla, c, c_ctx,
           w_mod, b_mod, g_norm, w_ffn_gate, w_ffn_up, w_ffn_down, w_attn_in, w_attn_out,
           g_qnorm, g_knorm, na_rel_bias, w_gla_in, w_gla_gup, b_gla_gup, g_gla_norm, w_gla_out, g_final):
    x = (x_prompt.reshape(N_CTX, D_MODEL), x_sample.reshape(N_LAT, D_MODEL))
    cond = jnp.concatenate([c_ctx[None, :], c, jnp.zeros((COND_ROWS - 1 - DEC_BATCH, D_MODEL), F32)], axis=0)
    mod = _modulation(cond, w_mod, b_mod)

    wg_all, wu_all, wd_all = (w.astype(BF16) for w in (w_ffn_gate, w_ffn_up, w_ffn_down))
    rope_tabs = _rope_tables()
    n_attn = w_attn_in.shape[0]
    ck_a = cache_a_k.reshape(DEC_BATCH, n_attn, PAST_LEN, A_KV_HEADS * HEAD_DIM)
    cv_a = cache_a_v.reshape(DEC_BATCH, n_attn, PAST_LEN, A_KV_HEADS * HEAD_DIM)
    ck_b = cache_b_k.reshape(DEC_BATCH, n_attn, PAST_LEN, B_HEADS * HEAD_DIM)
    cv_b = cache_b_v.reshape(DEC_BATCH, n_attn, PAST_LEN, B_HEADS * HEAD_DIM)

    caches = [jnp.zeros((BATCH, n_attn, SEQ, ATTN_WIDTHS[s]), F32) for s in (1, 2, 4, 5)]
    new_gla = jnp.zeros((BATCH, w_gla_in.shape[0], 2, GLA_HEADS, GLA_DK, GLA_DV), F32)
    for l in range(DEPTH):
        x = _ffn(x, mod, l, 0, g_norm[l, 0], wg_all, wu_all, wd_all)
        if l % 2 == 0:
            i = l // 2
            gq = jnp.tile(g_qnorm[i], 2).reshape(1, PAIR)
            gk = jnp.tile(g_knorm[i], 2).reshape(1, PAIR)
            n_qa = A_HEADS * HEAD_DIM
            w_in = w_attn_in[i].at[:, :n_qa].set(_permute_heads(w_attn_in[i][:, :n_qa], axis=1)).astype(BF16)
            w_out = w_attn_out[i].at[:n_qa].set(_permute_heads(w_attn_out[i][:n_qa], axis=0)).astype(BF16)
            qa_c, ka_c, va_c, qb_c, kb_c, vb_c, *caches = _attn_in(
                x, mod, l, g_norm[l, 1], w_in, gq, gk, lat=False, layer_i=i, caches=caches)
            qa_l, ka_l, va_l, qb_l, kb_l, vb_l = _attn_in(
                x, mod, l, g_norm[l, 1], w_in, gq, gk, lat=True, rope_tabs=rope_tabs)
            oa_c = _attention(qa_c, ka_c, va_c, lat=False, tq=SEQ, n_seq=CTX_SEQS_A)
            ob_c = _attention(qb_c, kb_c, vb_c, lat=False, tq=SEQ, n_seq=CTX_SEQS_B)
            oa_l = _attention(qa_l, ka_l, va_l, lat=True, tq=TQ_LAT_A, layer_i=i, cache_k=ck_a, cache_v=cv_a)
            ob_l = _attention(qb_l, kb_l, vb_l, lat=True, tq=TQ_LAT_B, layer_i=i, cache_k=ck_b, cache_v=cv_b,
                              bias_tab=_na_bias_table(na_rel_bias[i]))
            x = _mixer_out(_attn_out_kernel, "attn_out", x, mod, l, [(oa_c, oa_l), (ob_c, ob_l)], [], [w_out])
        else:
            j = l // 2
            w_in = jnp.pad(w_gla_in[j], ((0, 0), (0, GLA_D_PAD - 2 * GLA_RANK))).astype(BF16)
            w_gup = jnp.zeros((GLA_D_PAD, 2 * GLA_KDIM), F32)
            w_gup = w_gup.at[:GLA_RANK, :GLA_KDIM].set(w_gla_gup[j, 0])
            w_gup = w_gup.at[GLA_RANK:2 * GLA_RANK, GLA_KDIM:].set(w_gla_gup[j, 1]).astype(BF16)
            b_gup = b_gla_gup[j].reshape(1, 2 * GLA_KDIM)
            q, k, v, gt, lgf, lgb = _gla_in(x, mod, l, g_norm[l, 1], w_in, w_gup, b_gup)
            o_c, new_gla = _gla(q, k, v, lgf, lgb, lat=False, heads=GLA_HEADS, layer_j=j, new_state=new_gla)
            (o_l,) = _gla(q, k, v, lgf, lgb, lat=True, heads=1, layer_j=j, state=state_gla)
            x = _mixer_out(_gla_out_kernel, "gla_out", x, mod, l, [(o_c, o_l)], [gt],
                           [g_gla_norm[j].reshape(1, GLA_DV), w_gla_out[j].astype(BF16)])
        x = _ffn(x, mod, l, 2, g_norm[l, 2], wg_all, wu_all, wd_all)

    y_ctx = _final_norm(x, g_final, 0, N_CTX)
    y_lat = _final_norm(x, g_final, N_CTX, N_LAT)
    return (y_ctx.reshape(BATCH, SEQ, D_MODEL), y_lat.reshape(DEC_BATCH, DEC_SEQ, D_MODEL),
            caches[0].reshape(BATCH, n_attn, SEQ, A_KV_HEADS, HEAD_DIM),
            caches[1].reshape(BATCH, n_attn, SEQ, A_KV_HEADS, HEAD_DIM),
            caches[2].reshape(BATCH, n_attn, SEQ, B_HEADS, HEAD_DIM),
            caches[3].reshape(BATCH, n_attn, SEQ, B_HEADS, HEAD_DIM), new_gla)
```
